```python
import jax
import jax.numpy as jnp
from jax import lax
import numpy as np


D_MODEL = 1024
BATCH = 4
SEQ = 4096
DEPTH = 2

GRID_W = 64
CTX_LEN = 256
Q_BLOCK = 128
ROPE_THETA = 10000.0
EPS = 1e-6

MLA_HEADS = 8
MLA_NOPE = 64
MLA_ROPE = 32
MLA_V = 64
MLA_Q_LORA = 256
MLA_KV_LORA = 128
MLA_SCALE = (MLA_NOPE + MLA_ROPE) ** -0.5
GQA_HEADS = 8
GQA_KV_HEADS = 2
GQA_HEAD_DIM = 64
GQA_SCALE = GQA_HEAD_DIM ** -0.5
ATTN_IN = MLA_Q_LORA + MLA_KV_LORA + MLA_ROPE + (GQA_HEADS + 2 * GQA_KV_HEADS) * GQA_HEAD_DIM
ATTN_OUT = MLA_HEADS * MLA_V + GQA_HEADS * GQA_HEAD_DIM

SSM_HEADS = 16
SSM_HEAD_DIM = 64
SSM_GROUPS = 2
SSM_HPG = SSM_HEADS // SSM_GROUPS
SSM_STATE = 128
SSM_CONV = 5
SSD_CHUNK = 128
SSM_D = SSM_HEADS * SSM_HEAD_DIM
SSM_XBC = SSM_D + 2 * SSM_GROUPS * SSM_STATE
CONF_D = D_MODEL
CONF_K = 31
SSM_IN = SSM_D + SSM_XBC + SSM_HEADS + 2 * CONF_D
SSM_OUT = SSM_D + CONF_D

N_EXPERTS = 32
MOE_TOP_K = 4
MOE_FF = D_MODEL
SWIGLU_LIMIT = 7.0
SWIGLU_ALPHA = 1.702
MOE_BLOCK = 256

N_ATTN_LAYERS = (DEPTH + 1) // 2
N_SSM_LAYERS = DEPTH // 2

kernel_name = 'hybrid_mla_gqa_ssd_conformer_moe_dit'


def rms_norm(x, g):
    xf = x.astype(jnp.float32)
    y = xf * lax.rsqrt(jnp.mean(xf * xf, axis=-1, keepdims=True) + EPS)
    return (y * g.astype(jnp.float32)).astype(x.dtype)


def layer_norm(x, g, b):
    xf = x.astype(jnp.float32)
    mu = jnp.mean(xf, axis=-1, keepdims=True)
    var = jnp.mean(jnp.square(xf - mu), axis=-1, keepdims=True)
    y = (xf - mu) * lax.rsqrt(var + EPS)
    return (y * g.astype(jnp.float32) + b.astype(jnp.float32)).astype(x.dtype)


def modulate(h, shift, scale):
    return h * (1.0 + scale) + shift


def axial_rope_tables(n_tok, rot_dim):
    rows = n_tok // GRID_W
    row = jnp.broadcast_to(jnp.arange(rows, dtype=jnp.float32)[:, None], (rows, GRID_W)).reshape(-1)
    col = jnp.broadcast_to(jnp.arange(GRID_W, dtype=jnp.float32)[None, :], (rows, GRID_W)).reshape(-1)
    n_freq = rot_dim // 4
    inv_freq = ROPE_THETA ** (-jnp.arange(n_freq, dtype=jnp.float32) / n_freq)
    ang = jnp.concatenate([row[:, None] * inv_freq, col[:, None] * inv_freq], axis=-1)
    return jnp.cos(ang), jnp.sin(ang)


def apply_rope(x, cos, sin):
    xp = x.astype(jnp.float32).reshape(x.shape[:-1] + (x.shape[-1] // 2, 2))
    xe, xo = xp[..., 0], xp[..., 1]
    cs, sn = cos[None, :, None, :], sin[None, :, None, :]
    out = jnp.stack([xe * cs - xo * sn, xe * sn + xo * cs], axis=-1)
    return out.reshape(x.shape).astype(x.dtype)


def depthwise_conv(u, w, b):
    k = w.shape[0]
    y = lax.conv_general_dilated(u, w[:, None, :].astype(u.dtype), window_strides=(1,), padding=[(k // 2, k // 2)], dimension_numbers=('NWC', 'WIO', 'NWC'), feature_group_count=u.shape[-1])
    return y + b.astype(u.dtype)


def blocked_attention(q, k, v, scale):
    b, tq, kh, g, dk = q.shape
    nb = tq // Q_BLOCK
    qb = jnp.moveaxis(q.reshape(b, nb, Q_BLOCK, kh, g, dk), 1, 0)

    def one_block(qi):
        s = jnp.einsum('bqhgd,bkhd->bhgqk', qi, k, preferred_element_type=jnp.float32) * scale
        pr = jax.nn.softmax(s, axis=-1).astype(v.dtype)
        return jnp.einsum('bhgqk,bkhd->bqhgd', pr, v)

    o = lax.map(one_block, qb)
    return jnp.moveaxis(o, 0, 1).reshape(b, tq, kh, g, v.shape[-1])


def attn_project(h, w_in, g_cq, w_uq, g_ckv, w_ukv, g_q, g_k):
    b, t, _ = h.shape
    u = h @ w_in
    o1 = MLA_Q_LORA
    o2 = o1 + MLA_KV_LORA
    o3 = o2 + MLA_ROPE
    o4 = o3 + GQA_HEADS * GQA_HEAD_DIM
    o5 = o4 + GQA_KV_HEADS * GQA_HEAD_DIM
    c_q, c_kv, k_rope = u[..., :o1], u[..., o1:o2], u[..., o2:o3]
    q_g, k_g, v_g = u[..., o3:o4], u[..., o4:o5], u[..., o5:]
    q_m = (rms_norm(c_q, g_cq) @ w_uq).reshape(b, t, MLA_HEADS, MLA_NOPE + MLA_ROPE)
    kv_m = (rms_norm(c_kv, g_ckv) @ w_ukv).reshape(b, t, MLA_HEADS, MLA_NOPE + MLA_V)
    k_nope, v_m = kv_m[..., :MLA_NOPE], kv_m[..., MLA_NOPE:]
    k_rope = k_rope.reshape(b, t, 1, MLA_ROPE)
    q_g = rms_norm(q_g.reshape(b, t, GQA_HEADS, GQA_HEAD_DIM), g_q)
    k_g = rms_norm(k_g.reshape(b, t, GQA_KV_HEADS, GQA_HEAD_DIM), g_k)
    v_g = v_g.reshape(b, t, GQA_KV_HEADS, GQA_HEAD_DIM)
    return q_m, k_nope, k_rope, v_m, q_g, k_g, v_g


def mla_keys(k_nope, k_rope):
    return jnp.concatenate([k_nope, jnp.broadcast_to(k_rope, k_nope.shape[:-1] + (MLA_ROPE,))], axis=-1)


def attention_mixer(h_lat, h_ctx, w_in, g_cq, w_uq, g_ckv, w_ukv, g_q, g_k, w_out, ctx_out):
    s = h_lat.shape[1]
    q_m, k_nope, k_rope, v_m, q_g, k_g, v_g = attn_project(h_lat, w_in, g_cq, w_uq, g_ckv, w_ukv, g_q, g_k)
    cq_m, ck_nope, ck_rope, cv_m, cq_g, ck_g, cv_g = attn_project(h_ctx, w_in, g_cq, w_uq, g_ckv, w_ukv, g_q, g_k)
    cos_m, sin_m = axial_rope_tables(s, MLA_ROPE)
    cos_g, sin_g = axial_rope_tables(s, GQA_HEAD_DIM)
    q_m = jnp.concatenate([q_m[..., :MLA_NOPE], apply_rope(q_m[..., MLA_NOPE:], cos_m, sin_m)], axis=-1)
    k_rope = apply_rope(k_rope, cos_m, sin_m)
    q_g = apply_rope(q_g, cos_g, sin_g)
    k_g = apply_rope(k_g, cos_g, sin_g)
    ck_m = mla_keys(ck_nope, ck_rope)
    k_m_all = jnp.concatenate([ck_m, mla_keys(k_nope, k_rope)], axis=1)
    v_m_all = jnp.concatenate([cv_m, v_m], axis=1)
    k_g_all = jnp.concatenate([ck_g, k_g], axis=1)
    v_g_all = jnp.concatenate([cv_g, v_g], axis=1)

    def heads_out(qm, km, vm, qg, kg, vg):
        b, t = qm.shape[:2]
        o_m = blocked_attention(qm[:, :, :, None, :], km, vm, MLA_SCALE).reshape(b, t, MLA_HEADS * MLA_V)
        qg5 = qg.reshape(b, t, GQA_KV_HEADS, GQA_HEADS // GQA_KV_HEADS, GQA_HEAD_DIM)
        o_g = blocked_attention(qg5, kg, vg, GQA_SCALE).reshape(b, t, GQA_HEADS * GQA_HEAD_DIM)
        return jnp.concatenate([o_m, o_g], axis=-1) @ w_out

    out_lat = heads_out(q_m, k_m_all, v_m_all, q_g, k_g_all, v_g_all)
    out_ctx = heads_out(cq_m, ck_m, cv_m, cq_g, ck_g, cv_g) if ctx_out else None
    return out_lat, out_ctx


def ssd_chunked(x, dt, bm, cm, a, init_state, want_y):
    f32 = jnp.float32
    b, t, g, r, p = x.shape
    n = bm.shape[-1]
    nc = t // SSD_CHUNK
    xdt = (x.astype(f32) * dt[..., None]).reshape(b, nc, SSD_CHUNK, g, r, p)
    da = (dt * a).reshape(b, nc, SSD_CHUNK, g, r)
    bc = bm.astype(f32).reshape(b, nc, SSD_CHUNK, g, n)
    a_cs = jnp.cumsum(da, axis=2)
    a_end = a_cs[:, :, -1]
    chunk_states = jnp.einsum('bclgn,bclgr,bclgrp->bcgrpn', bc, jnp.exp(a_end[:, :, None] - a_cs), xdt)

    def carry_state(state, inp):
        decay, new = inp
        return state * decay[..., None, None] + new, state

    final, states_in = lax.scan(carry_state, init_state, (jnp.moveaxis(jnp.exp(a_end), 1, 0), jnp.moveaxis(chunk_states, 1, 0)))
    if not want_y:
        return None, final
    cc = cm.astype(f32).reshape(b, nc, SSD_CHUNK, g, n)
    y_off = jnp.einsum('bclgn,cbgrpn,bclgr->bclgrp', cc, states_in, jnp.exp(a_cs))
    a_t = jnp.moveaxis(a_cs, 2, -1)
    seg = a_t[..., :, None] - a_t[..., None, :]
    lower = jnp.tril(jnp.ones((SSD_CHUNK, SSD_CHUNK), dtype=bool))
    decay = jnp.exp(jnp.where(lower, seg, -jnp.inf))
    cb = jnp.einsum('bclgn,bcsgn->bcgls', cc, bc)
    y_diag = jnp.einsum('bcgrls,bcsgrp->bclgrp', cb[:, :, :, None] * decay, xdt)
    return (y_diag + y_off).reshape(b, t, g, r, p), final


def ssd_inputs(u_scan, conv_w, conv_b):
    b, t, _ = u_scan.shape
    gn = SSM_GROUPS * SSM_STATE
    xbc = jax.nn.silu(depthwise_conv(u_scan[..., :SSM_XBC], conv_w, conv_b))
    xs = xbc[..., :SSM_D].reshape(b, t, SSM_GROUPS, SSM_HPG, SSM_HEAD_DIM)
    bm = xbc[..., SSM_D:SSM_D + gn].reshape(b, t, SSM_GROUPS, SSM_STATE)
    cm = xbc[..., SSM_D + gn:].reshape(b, t, SSM_GROUPS, SSM_STATE)
    dt_raw = u_scan[..., SSM_XBC:].astype(jnp.float32).reshape(b, t, SSM_GROUPS, SSM_HPG)
    return xs, bm, cm, dt_raw


def bidir_ssd(inputs, a_log, dt_bias, d_skip, inits, want_y):
    xs, bm, cm, dt_raw = inputs
    ys = []
    finals = []
    for d in range(2):
        dt = jax.nn.softplus(dt_raw + dt_bias[d].astype(jnp.float32).reshape(SSM_GROUPS, SSM_HPG))
        a = -jnp.exp(a_log[d].astype(jnp.float32)).reshape(SSM_GROUPS, SSM_HPG)
        seq = (xs, dt, bm, cm)
        if d == 1:
            seq = tuple(jnp.flip(z, axis=1) for z in seq)
        y, fin = ssd_chunked(seq[0], seq[1], seq[2], seq[3], a, inits[d], want_y)
        finals.append(fin)
        if want_y:
            ys.append(jnp.flip(y, axis=1) if d == 1 else y)
    if not want_y:
        return None, finals
    skip = (d_skip[0] + d_skip[1]).astype(jnp.float32).reshape(SSM_GROUPS, SSM_HPG, 1)
    return ys[0] + ys[1] + skip * xs.astype(jnp.float32), finals


def conformer_conv(u, dw_w, dw_b, ln_g, ln_b):
    v = u[..., :CONF_D] * jax.nn.sigmoid(u[..., CONF_D:])
    v = depthwise_conv(v, dw_w, dw_b)
    return jax.nn.silu(layer_norm(v, ln_g, ln_b))


def ssm_conv_mixer(h_lat, h_ctx, w_in, conv_w, conv_b, a_log, dt_bias, d_skip, norm_g, dw_w, dw_b, ln_g, ln_b, w_out, ctx_out):
    lo, hi = SSM_D, SSM_D + SSM_XBC + SSM_HEADS
    b = h_lat.shape[0]
    u_lat = h_lat @ w_in
    if ctx_out:
        u_ctx = h_ctx @ w_in
        ctx_scan = u_ctx[..., lo:hi]
    else:
        u_ctx = None
        ctx_scan = h_ctx @ w_in[:, lo:hi]
    zero = jnp.zeros((b, SSM_GROUPS, SSM_HPG, SSM_HEAD_DIM, SSM_STATE), jnp.float32)
    y_ctx, ctx_states = bidir_ssd(ssd_inputs(ctx_scan, conv_w, conv_b), a_log, dt_bias, d_skip, (zero, zero), ctx_out)
    y_lat, _ = bidir_ssd(ssd_inputs(u_lat[..., lo:hi], conv_w, conv_b), a_log, dt_bias, d_skip, ctx_states, True)

    def merge(u, y):
        bb, t, _ = u.shape
        gated = y.reshape(bb, t, SSM_D) * jax.nn.silu(u[..., :SSM_D].astype(jnp.float32))
        y_ssm = rms_norm(gated, norm_g).astype(u.dtype)
        y_conv = conformer_conv(u[..., hi:], dw_w, dw_b, ln_g, ln_b)
        return jnp.concatenate([y_ssm, y_conv], axis=-1) @ w_out

    out_ctx = merge(u_ctx, y_ctx) if ctx_out else None
    return merge(u_lat, y_lat), out_ctx


def moe_ffn(h, w_router, b_router, w_gate_up, b_gate_up, w_down, b_down):
    n_tok, d = h.shape
    n_assign = n_tok * MOE_TOP_K
    logits = h.astype(jnp.float32) @ w_router.astype(jnp.float32) + b_router.astype(jnp.float32)
    top_logit, top_e = lax.top_k(logits, MOE_TOP_K)
    top_w = jax.nn.softmax(top_logit, axis=-1).astype(h.dtype)
    flat_e = top_e.reshape(-1)
    flat_tok = jnp.arange(n_assign, dtype=jnp.int32) // MOE_TOP_K
    order = jnp.argsort(flat_e)
    e_sorted = flat_e[order]
    counts = jnp.zeros((N_EXPERTS,), jnp.int32).at[flat_e].add(1)
    padded = (counts + MOE_BLOCK - 1) // MOE_BLOCK * MOE_BLOCK
    start = jnp.cumsum(counts) - counts
    pend = jnp.cumsum(padded)
    pstart = pend - padded
    dest = pstart[e_sorted] + jnp.arange(n_assign, dtype=jnp.int32) - start[e_sorted]
    n_blocks = -(-n_assign // MOE_BLOCK) + N_EXPERTS
    n_rows = n_blocks * MOE_BLOCK
    row_tok = jnp.zeros((n_rows,), jnp.int32).at[dest].set(flat_tok[order])
    row_w = jnp.zeros((n_rows,), h.dtype).at[dest].set(top_w.reshape(-1)[order])
    block_start = jnp.arange(n_blocks, dtype=jnp.int32) * MOE_BLOCK
    block_expert = jnp.minimum(jnp.searchsorted(pend, block_start, side='right'), N_EXPERTS - 1)
    xs = h[row_tok].reshape(n_blocks, MOE_BLOCK, d)

    def expert_block(args):
        xb, e = args
        gu = xb @ w_gate_up[e] + b_gate_up[e]
        gate = jnp.minimum(gu[:, :MOE_FF], SWIGLU_LIMIT)
        up = jnp.clip(gu[:, MOE_FF:], -SWIGLU_LIMIT, SWIGLU_LIMIT)
        act = (up + 1.0) * (gate * jax.nn.sigmoid(SWIGLU_ALPHA * gate))
        return act @ w_down[e] + b_down[e]

    ys = lax.map(expert_block, (xs, block_expert)).reshape(n_rows, d) * row_w[:, None]
    return jnp.zeros_like(h).at[row_tok].add(ys)


def setup_inputs(seed: int = 0) -> dict:
    key = jax.random.key(seed)
    ks = iter(list(jax.random.split(key, 64)))
    f32 = jnp.float32

    def nrm(shape, scale):
        return jax.random.normal(next(ks), shape, f32) * scale

    def gain(shape):
        return 1.0 + 0.05 * jax.random.normal(next(ks), shape, f32)

    D = D_MODEL
    NA, NS = N_ATTN_LAYERS, N_SSM_LAYERS
    a_init = jax.random.uniform(next(ks), (NS, 2, SSM_HEADS), f32, 1.0, 16.0)
    dt_init = jnp.exp(jax.random.uniform(next(ks), (NS, 2, SSM_HEADS), f32, np.log(1e-3), np.log(1e-1)))
    return {
        'x': nrm((BATCH, SEQ, D), 1.0),
        'c': nrm((BATCH, D), 1.0),
        'ctx': nrm((BATCH, CTX_LEN, D), 1.0),
        'c_ctx': nrm((D,), 1.0),
        'w_mod': nrm((DEPTH, D, 6 * D), 0.5 * D ** -0.5),
        'b_mod': nrm((DEPTH, 6 * D), 0.02),
        'norm_g': gain((DEPTH, 2, D)),
        'attn_w_in': nrm((NA, D, ATTN_IN), D ** -0.5),
        'mla_g_cq': gain((NA, MLA_Q_LORA)),
        'mla_w_uq': nrm((NA, MLA_Q_LORA, MLA_HEADS * (MLA_NOPE + MLA_ROPE)), MLA_Q_LORA ** -0.5),
        'mla_g_ckv': gain((NA, MLA_KV_LORA)),
        'mla_w_ukv': nrm((NA, MLA_KV_LORA, MLA_HEADS * (MLA_NOPE + MLA_V)), MLA_KV_LORA ** -0.5),
        'gqa_g_q': gain((NA, GQA_HEAD_DIM)),
        'gqa_g_k': gain((NA, GQA_HEAD_DIM)),
        'attn_w_out': nrm((NA, ATTN_OUT, D), ATTN_OUT ** -0.5),
        'ssm_w_in': nrm((NS, D, SSM_IN), D ** -0.5),
        'ssm_conv_w': nrm((NS, SSM_CONV, SSM_XBC), SSM_CONV ** -0.5),
        'ssm_conv_b': nrm((NS, SSM_XBC), 0.02),
        'ssm_a_log': jnp.log(a_init),
        'ssm_dt_bias': dt_init + jnp.log(-jnp.expm1(-dt_init)),
        'ssm_d': gain((NS, 2, SSM_HEADS)),
        'ssm_norm_g': gain((NS, SSM_D)),
        'conf_dw_w': nrm((NS, CONF_K, CONF_D), CONF_K ** -0.5),
        'conf_dw_b': nrm((NS, CONF_D), 0.02),
        'conf_ln_g': gain((NS, CONF_D)),
        'conf_ln_b': nrm((NS, CONF_D), 0.02),
        'ssm_w_out': nrm((NS, SSM_OUT, D), SSM_OUT ** -0.5),
        'moe_w_router': nrm((DEPTH, D, N_EXPERTS), D ** -0.5),
        'moe_b_router': nrm((DEPTH, N_EXPERTS), 0.01),
        'moe_w_gate_up': nrm((DEPTH, N_EXPERTS, D, 2 * MOE_FF), D ** -0.5),
        'moe_b_gate_up': nrm((DEPTH, N_EXPERTS, 2 * MOE_FF), 0.02),
        'moe_w_down': nrm((DEPTH, N_EXPERTS, MOE_FF, D), MOE_FF ** -0.5),
        'moe_b_down': nrm((DEPTH, N_EXPERTS, D), 0.02),
        'final_g': gain((D,)),
    }


def reference(x, c, ctx, c_ctx, w_mod, b_mod, norm_g, attn_w_in, mla_g_cq, mla_w_uq, mla_g_ckv, mla_w_ukv, gqa_g_q, gqa_g_k, attn_w_out, ssm_w_in, ssm_conv_w, ssm_conv_b, ssm_a_log, ssm_dt_bias, ssm_d, ssm_norm_g, conf_dw_w, conf_dw_b, conf_ln_g, conf_ln_b, ssm_w_out, moe_w_router, moe_b_router, moe_w_gate_up, moe_b_gate_up, moe_w_down, moe_b_down, final_g):
    b, s, d = x.shape
    silu_c = jax.nn.silu(c)
    silu_cc = jax.nn.silu(c_ctx)
    for i in range(DEPTH):
        ctx_out = i < DEPTH - 1
        j = i // 2
        mod = (silu_c @ w_mod[i] + b_mod[i])[:, None, :]
        sh1, sc1, g1, sh2, sc2, g2 = jnp.split(mod, 6, axis=-1)
        n_cm = 6 if ctx_out else 2
        mod_c = silu_cc @ w_mod[i][:, :n_cm * d] + b_mod[i][:n_cm * d]
        cmods = jnp.split(mod_c, n_cm, axis=-1)
        h_lat = modulate(rms_norm(x, norm_g[i, 0]), sh1, sc1)
        h_ctx = modulate(rms_norm(ctx, norm_g[i, 0]), cmods[0], cmods[1])
        if i % 2 == 0:
            o_lat, o_ctx = attention_mixer(h_lat, h_ctx, attn_w_in[j], mla_g_cq[j], mla_w_uq[j], mla_g_ckv[j], mla_w_ukv[j], gqa_g_q[j], gqa_g_k[j], attn_w_out[j], ctx_out)
        else:
            o_lat, o_ctx = ssm_conv_mixer(h_lat, h_ctx, ssm_w_in[j], ssm_conv_w[j], ssm_conv_b[j], ssm_a_log[j], ssm_dt_bias[j], ssm_d[j], ssm_norm_g[j], conf_dw_w[j], conf_dw_b[j], conf_ln_g[j], conf_ln_b[j], ssm_w_out[j], ctx_out)
        x = x + g1 * o_lat
        h_lat = modulate(rms_norm(x, norm_g[i, 1]), sh2, sc2)
        if ctx_out:
            ctx = ctx + cmods[2] * o_ctx
            h_ctx = modulate(rms_norm(ctx, norm_g[i, 1]), cmods[3], cmods[4])
            tokens = jnp.concatenate([h_lat.reshape(-1, d), h_ctx.reshape(-1, d)], axis=0)
            f = moe_ffn(tokens, moe_w_router[i], moe_b_router[i], moe_w_gate_up[i], moe_b_gate_up[i], moe_w_down[i], moe_b_down[i])
            x = x + g2 * f[:b * s].reshape(b, s, d)
            ctx = ctx + cmods[5] * f[b * s:].reshape(ctx.shape)
        else:
            f = moe_ffn(h_lat.reshape(-1, d), moe_w_router[i], moe_b_router[i], moe_w_gate_up[i], moe_b_gate_up[i], moe_w_down[i], moe_b_down[i])
            x = x + g2 * f.reshape(b, s, d)
    return rms_norm(x, final_g)
```

```python
import functools

import jax
import jax.numpy as jnp
from jax import lax
from jax.experimental import pallas as pl
from jax.experimental.pallas import tpu as pltpu

F32 = jnp.float32
BF16 = jnp.bfloat16
HIGHEST = lax.Precision.HIGHEST

D = 1024
EPS = 1e-6
GRID_W = 64
ROPE_THETA = 10000.0
LOG2E = 1.4426950408889634

MLA_HEADS = 8
MLA_NOPE = 64
MLA_ROPE = 32
MLA_Q_LORA = 256
MLA_KV_LORA = 128
MLA_SCALE = (MLA_NOPE + MLA_ROPE) ** -0.5
GQA_HEADS = 8
GQA_HEAD_DIM = 64
GQA_SCALE = GQA_HEAD_DIM ** -0.5
ATTN_IN_AUG = 1280

SSM_HEADS = 16
SSM_HEAD_DIM = 64
SSM_STATE = 128
SSM_D = 1024
SSM_XBC = 1536
SSM_CONV = 5
CONF_K = 31
CHUNK = 128
SSM_IN_AUG = 1024 + 1536 + 128 + 2048

N_EXPERTS = 32
TOP_K = 4
MOE_FF = 1024
SWIGLU_LIMIT = 7.0
SWIGLU_ALPHA = 1.702

TM = 256
TMOE = 256
LANES = 128
HALO = 16
VMEM_LIMIT = 56 * 1024 * 1024


def _cparams(sem):
    return pltpu.CompilerParams(dimension_semantics=sem, vmem_limit_bytes=VMEM_LIMIT)


def _rms(x, g):
    return x * lax.rsqrt(jnp.mean(x * x, axis=-1, keepdims=True) + EPS) * g


def _sigmoid(x):
    return 1.0 / (1.0 + jnp.exp(-x))


def _const_spec(shape):
    n = len(shape)
    return pl.BlockSpec(shape, lambda *_: (0,) * n)


def _mod_kernel(c_ref, w_ref, b_ref, o_ref):
    c = c_ref[...]
    o_ref[0] = jnp.dot(c * _sigmoid(c), w_ref[0], precision=HIGHEST, preferred_element_type=F32) + b_ref[0]


def _modulations(c_rows, w_mod, b_mod):
    depth, _, n = w_mod.shape
    tn = 512
    r = c_rows.shape[0]
    return pl.pallas_call(
        _mod_kernel,
        grid=(depth, n // tn),
        in_specs=[pl.BlockSpec((r, D), lambda l, j: (0, 0)),
                  pl.BlockSpec((1, D, tn), lambda l, j: (l, 0, j)),
                  pl.BlockSpec((1, 1, tn), lambda l, j: (l, 0, j))],
        out_specs=pl.BlockSpec((1, r, tn), lambda l, j: (l, 0, j)),
        out_shape=jax.ShapeDtypeStruct((depth, r, n), F32),
        compiler_params=_cparams(("arbitrary", "arbitrary")),
        name="modulations",
    )(c_rows, w_mod, b_mod.reshape(depth, 1, n))


def _attn_in_kernel(x_ref, mods_ref, ng_ref, win_ref, gcq_ref, wuq_ref, gckv_ref, wkv_ref, gq_ref, gk_ref, bd_ref,
                    tab_ref, qm_ref, km_ref, vm_ref, qg_ref, kg_ref, vgs_ref):
    x = x_ref[0]
    sh = mods_ref[0, 0, 0:1, :]
    sc = mods_ref[0, 0, 1:2, :]
    h = _rms(x, ng_ref[...]) * (1.0 + sc) + sh
    u = jnp.dot(h.astype(BF16), win_ref[...], preferred_element_type=F32)

    tab = tab_ref[...]
    cg, sg = tab[:, 0:128], tab[:, 128:256]
    cm, sm = tab[:, 256:384], tab[:, 384:512]
    lane = lax.broadcasted_iota(jnp.int32, (x.shape[0], LANES), 1)
    even = (lane & 1) == 0
    lo = lane < 64

    def rope(v, c, s):
        partner = jnp.where(even, pltpu.roll(v, LANES - 1, 1), pltpu.roll(v, 1, 1))
        return v * c + partner * s

    cq = _rms(u[:, 0:256], gcq_ref[...])
    qm = jnp.dot(cq.astype(BF16), wuq_ref[...], preferred_element_type=F32)
    for hd in range(MLA_HEADS):
        qm_ref[0, hd] = rope(qm[:, hd * 128:(hd + 1) * 128], cm, sm).astype(BF16)
    ckv = _rms(u[:, 256:384], gckv_ref[...])
    kv = jnp.dot(ckv.astype(BF16), wkv_ref[...], preferred_element_type=F32)
    kr = rope(u[:, 384:512], cm, sm)
    for hd in range(MLA_HEADS):
        km_ref[0, hd] = (kv[:, hd * 128:(hd + 1) * 128] + kr).astype(BF16)
    for p in range(MLA_HEADS // 2):
        vm_ref[0, p] = kv[:, 1024 + p * 128:1024 + (p + 1) * 128].astype(BF16)

    qg = u[:, 512:1024]
    ms = jnp.dot((qg * qg).astype(BF16), bd_ref[...], preferred_element_type=F32)
    qg = qg * lax.rsqrt(ms + EPS) * gq_ref[...]
    zero = jnp.zeros((x.shape[0], LANES), F32)
    for p in range(GQA_HEADS // 2):
        blk = rope(qg[:, p * 128:(p + 1) * 128], cg, sg)
        swp = pltpu.roll(blk, 64, 1)
        if p < 2:
            qg_ref[0, 2 * p] = jnp.where(lo, blk, zero).astype(BF16)
            qg_ref[0, 2 * p + 1] = jnp.where(lo, swp, zero).astype(BF16)
        else:
            qg_ref[0, 2 * p] = jnp.where(lo, zero, swp).astype(BF16)
            qg_ref[0, 2 * p + 1] = jnp.where(lo, zero, blk).astype(BF16)
    kg = u[:, 1024:1152]
    msk = jnp.dot((kg * kg).astype(BF16), bd_ref[0:128, 0:128], preferred_element_type=F32)
    kg_ref[0] = rope(kg * lax.rsqrt(msk + EPS) * gk_ref[...], cg, sg).astype(BF16)
    vg = u[:, 1152:1280]
    vgs_ref[0, 0] = vg.astype(BF16)
    vgs_ref[0, 1] = pltpu.roll(vg, 64, 1).astype(BF16)


def _rope_tables(seq, ctx_len):
    rows = seq // GRID_W
    row = jnp.broadcast_to(jnp.arange(rows, dtype=F32)[:, None], (rows, GRID_W)).reshape(-1)
    col = jnp.broadcast_to(jnp.arange(GRID_W, dtype=F32)[None, :], (rows, GRID_W)).reshape(-1)

    def interleaved(rot_dim):
        n_freq = rot_dim // 4
        inv_freq = ROPE_THETA ** (-jnp.arange(n_freq, dtype=F32) / n_freq)
        ang = jnp.concatenate([row[:, None] * inv_freq, col[:, None] * inv_freq], axis=-1)
        cos = jnp.repeat(jnp.cos(ang), 2, axis=-1)
        sin = jnp.repeat(jnp.sin(ang), 2, axis=-1) * jnp.tile(jnp.array([-1.0, 1.0], F32), rot_dim // 2)
        return cos, sin

    cg, sg = interleaved(GQA_HEAD_DIM)
    cg, sg = jnp.tile(cg, (1, 2)), jnp.tile(sg, (1, 2))
    cm32, sm32 = interleaved(MLA_ROPE)
    ones, zeros = jnp.ones((seq, 64), F32), jnp.zeros((seq, 64), F32)
    cm = jnp.concatenate([ones, cm32, ones[:, :32]], axis=-1)
    sm = jnp.concatenate([zeros, sm32, zeros[:, :32]], axis=-1)
    lat = jnp.concatenate([cg, sg, cm, sm], axis=-1)
    ident = jnp.concatenate([jnp.ones((ctx_len, 128), F32), jnp.zeros((ctx_len, 128), F32)] * 2, axis=-1)
    return jnp.concatenate([ident, lat], axis=0)


def _attn_in(x_all, mods, ng, w_in, g_cq, w_uq, g_ckv, w_ukv, g_q, g_k, tab, nct):
    b, t, _ = x_all.shape
    o1, o2, o3 = MLA_Q_LORA, MLA_Q_LORA + MLA_KV_LORA, MLA_Q_LORA + MLA_KV_LORA + MLA_ROPE
    zc = lambda n: jnp.zeros((D, n), F32)
    w_aug = jnp.concatenate([w_in[:, :o2], zc(64), w_in[:, o2:o3], zc(32), w_in[:, o3:]], axis=1).astype(BF16)
    wuq = jnp.pad(w_uq.reshape(MLA_Q_LORA, MLA_HEADS, 96), ((0, 0), (0, 0), (0, 32))).reshape(MLA_Q_LORA, 1024).astype(BF16)
    wukv = w_ukv.reshape(MLA_KV_LORA, MLA_HEADS, 128)
    wk = jnp.pad(wukv[:, :, :64], ((0, 0), (0, 0), (0, 64))).reshape(MLA_KV_LORA, 1024)
    wkv = jnp.concatenate([wk, wukv[:, :, 64:].reshape(MLA_KV_LORA, 512)], axis=1).astype(BF16)
    gcq = (g_cq * (MLA_SCALE * LOG2E)).reshape(1, -1)
    gq = (jnp.tile(g_q, GQA_HEADS) * (GQA_SCALE * LOG2E)).reshape(1, -1)
    gk = jnp.tile(g_k, 2).reshape(1, -1)
    bd = jnp.kron(jnp.eye(GQA_HEADS, dtype=F32), jnp.full((64, 64), 1.0 / 64, F32)).astype(BF16)
    nt = t // TM
    tok = lambda bb, i: (bb, i, 0)
    hm = lambda bb, i: (bb, 0, i, 0)
    return pl.pallas_call(
        _attn_in_kernel,
        grid=(b, nt),
        in_specs=[pl.BlockSpec((1, TM, D), tok),
                  pl.BlockSpec((1, 1, 6, D), lambda bb, i: (bb, jnp.where(i < nct, 1, 0), 0, 0)),
                  _const_spec((1, D)), _const_spec((D, ATTN_IN_AUG)), _const_spec((1, MLA_Q_LORA)),
                  _const_spec((MLA_Q_LORA, 1024)), _const_spec((1, MLA_KV_LORA)), _const_spec((MLA_KV_LORA, 1536)),
                  _const_spec((1, 512)), _const_spec((1, 128)), _const_spec((512, 512)),
                  pl.BlockSpec((TM, 512), lambda bb, i: (i, 0))],
        out_specs=[pl.BlockSpec((1, 8, TM, 128), hm), pl.BlockSpec((1, 8, TM, 128), hm),
                   pl.BlockSpec((1, 4, TM, 128), hm), pl.BlockSpec((1, 8, TM, 128), hm),
                   pl.BlockSpec((1, TM, 128), tok), pl.BlockSpec((1, 2, TM, 128), hm)],
        out_shape=[jax.ShapeDtypeStruct((b, 8, t, 128), BF16), jax.ShapeDtypeStruct((b, 8, t, 128), BF16),
                   jax.ShapeDtypeStruct((b, 4, t, 128), BF16), jax.ShapeDtypeStruct((b, 8, t, 128), BF16),
                   jax.ShapeDtypeStruct((b, t, 128), BF16), jax.ShapeDtypeStruct((b, 2, t, 128), BF16)],
        compiler_params=_cparams(("arbitrary", "arbitrary")),
        name="attn_in",
    )(x_all, mods, ng.reshape(1, D), w_aug, gcq, wuq, g_ckv.reshape(1, -1), wkv, gq, gk, bd, tab)


def _softmax_pv(q, k, v):
    s = lax.dot_general(q, k, (((1,), (1,)), ((), ())), preferred_element_type=F32)
    m = jnp.max(s, axis=-1, keepdims=True)
    p = jnp.exp2(s - m)
    l = jnp.sum(p, axis=-1, keepdims=True)
    return jnp.dot(p.astype(BF16), v, preferred_element_type=F32) / l


def _mla_kernel(q_ref, k_ref, v_ref, o_ref, *, nct, ctx_len):
    i = pl.program_id(2)
    lo = lax.broadcasted_iota(jnp.int32, (q_ref.shape[2], LANES), 1) < 64

    def run(nk):
        oa = _softmax_pv(q_ref[0, 0], k_ref[0, 0, 0:nk, :], v_ref[0, 0, 0:nk, :])
        ob = _softmax_pv(q_ref[0, 1], k_ref[0, 1, 0:nk, :], v_ref[0, 0, 0:nk, :])
        o_ref[0] = jnp.where(lo, oa, ob).astype(BF16)

    @pl.when(i < nct)
    def _():
        run(ctx_len)

    @pl.when(i >= nct)
    def _():
        run(k_ref.shape[2])


def _mla_attention(qm, km, vm, nct, ctx_len):
    b, _, t, _ = qm.shape
    nt = t // TM
    return pl.pallas_call(
        functools.partial(_mla_kernel, nct=nct, ctx_len=ctx_len),
        grid=(b, 4, nt),
        in_specs=[pl.BlockSpec((1, 2, TM, 128), lambda bb, p, i: (bb, p, i, 0)),
                  pl.BlockSpec((1, 2, t, 128), lambda bb, p, i: (bb, p, 0, 0)),
                  pl.BlockSpec((1, 1, t, 128), lambda bb, p, i: (bb, p, 0, 0))],
        out_specs=pl.BlockSpec((1, TM, 128), lambda bb, p, i: (bb, i, p)),
        out_shape=jax.ShapeDtypeStruct((b, t, 512), BF16),
        compiler_params=_cparams(("arbitrary", "arbitrary", "arbitrary")),
        name="mla_attention",
    )(qm, km, vm)


def _gqa_kernel(q_ref, k_ref, ve_ref, vo_ref, o_ref, *, nct, ctx_len):
    i = pl.program_id(2)
    lo = lax.broadcasted_iota(jnp.int32, (q_ref.shape[2], LANES), 1) < 64

    def run(nk):
        k = k_ref[0, 0:nk, :]
        for pr in range(2):
            oa = _softmax_pv(q_ref[0, 2 * pr], k, ve_ref[0, 0, 0:nk, :])
            ob = _softmax_pv(q_ref[0, 2 * pr + 1], k, vo_ref[0, 0, 0:nk, :])
            o_ref[0, :, pr * 128:(pr + 1) * 128] = jnp.where(lo, oa, ob).astype(BF16)

    @pl.when(i < nct)
    def _():
        run(ctx_len)

    @pl.when(i >= nct)
    def _():
        run(k_ref.shape[1])


def _gqa_attention(qg, kg, vgs, nct, ctx_len):
    b, _, t, _ = qg.shape
    nt = t // TM
    return pl.pallas_call(
        functools.partial(_gqa_kernel, nct=nct, ctx_len=ctx_len),
        grid=(b, 2, nt),
        in_specs=[pl.BlockSpec((1, 4, TM, 128), lambda bb, g, i: (bb, g, i, 0)),
                  pl.BlockSpec((1, t, 128), lambda bb, g, i: (bb, 0, 0)),
                  pl.BlockSpec((1, 1, t, 128), lambda bb, g, i: (bb, g, 0, 0)),
                  pl.BlockSpec((1, 1, t, 128), lambda bb, g, i: (bb, 1 - g, 0, 0))],
        out_specs=pl.BlockSpec((1, TM, 256), lambda bb, g, i: (bb, i, g)),
        out_shape=jax.ShapeDtypeStruct((b, t, 512), BF16),
        compiler_params=_cparams(("arbitrary", "arbitrary", "arbitrary")),
        name="gqa_attention",
    )(qg, kg, vgs, vgs)


def _residual_norm_router(x, o, mods_ref, g2_ref, wr_ref, br_ref, x1_ref, h2_ref, lg_ref):
    g1 = mods_ref[0, 0, 2:3, :]
    sh2 = mods_ref[0, 0, 3:4, :]
    sc2 = mods_ref[0, 0, 4:5, :]
    x1 = x + g1 * o
    h2 = _rms(x1, g2_ref[...]) * (1.0 + sc2) + sh2
    x1_ref[0] = x1
    h2_ref[0] = h2.astype(BF16)
    lg_ref[0] = jnp.dot(h2, wr_ref[...], precision=HIGHEST, preferred_element_type=F32) + br_ref[...]


def _post_attn_kernel(x_ref, om_ref, og_ref, wout_ref, mods_ref, g2_ref, wr_ref, br_ref, x1_ref, h2_ref, lg_ref):
    o = (jnp.dot(om_ref[0], wout_ref[0:512, :], preferred_element_type=F32)
         + jnp.dot(og_ref[0], wout_ref[512:1024, :], preferred_element_type=F32))
    _residual_norm_router(x_ref[0], o, mods_ref, g2_ref, wr_ref, br_ref, x1_ref, h2_ref, lg_ref)


def _post_out_specs(b, t):
    tok = lambda bb, i: (bb, i, 0)
    specs = [pl.BlockSpec((1, TM, D), tok), pl.BlockSpec((1, TM, D), tok), pl.BlockSpec((1, TM, N_EXPERTS), tok)]
    shapes = [jax.ShapeDtypeStruct((b, t, D), F32), jax.ShapeDtypeStruct((b, t, D), BF16),
              jax.ShapeDtypeStruct((b, t, N_EXPERTS), F32)]
    return specs, shapes


def _post_attn(x_all, o_m, o_g, w_out, mods, g2, w_router, b_router, nct):
    b, t, _ = x_all.shape
    tok = lambda bb, i: (bb, i, 0)
    out_specs, out_shape = _post_out_specs(b, t)
    return pl.pallas_call(
        _post_attn_kernel,
        grid=(b, t // TM),
        in_specs=[pl.BlockSpec((1, TM, D), tok), pl.BlockSpec((1, TM, 512), tok), pl.BlockSpec((1, TM, 512), tok),
                  _const_spec((1024, D)),
                  pl.BlockSpec((1, 1, 6, D), lambda bb, i: (bb, jnp.where(i < nct, 1, 0), 0, 0)),
                  _const_spec((1, D)), _const_spec((D, N_EXPERTS)), _const_spec((1, N_EXPERTS))],
        out_specs=out_specs, out_shape=out_shape,
        compiler_params=_cparams(("arbitrary", "arbitrary")),
        name="post_attn",
    )(x_all, o_m, o_g, w_out.astype(BF16), mods, g2.reshape(1, D), w_router, b_router.reshape(1, -1))


def _route(logits):
    n_tok = logits.shape[0]
    n_assign = n_tok * TOP_K
    top_logit, top_e = lax.top_k(logits, TOP_K)
    top_w = jax.nn.softmax(top_logit, axis=-1)
    flat_e = top_e.reshape(-1).astype(jnp.int32)
    order = jnp.argsort(flat_e)
    e_sorted = flat_e[order]
    counts = jnp.zeros((N_EXPERTS,), jnp.int32).at[flat_e].add(1)
    padded = (counts + TMOE - 1) // TMOE * TMOE
    start = jnp.cumsum(counts) - counts
    pend = jnp.cumsum(padded)
    pstart = pend - padded
    dest = pstart[e_sorted] + jnp.arange(n_assign, dtype=jnp.int32) - start[e_sorted]
    n_tiles = -(-n_assign // TMOE) + N_EXPERTS
    row_tok = jnp.zeros((n_tiles * TMOE,), jnp.int32).at[dest].set((order // TOP_K).astype(jnp.int32))
    pos = jnp.zeros((n_assign,), jnp.int32).at[order].set(dest)
    tile_start = jnp.arange(n_tiles, dtype=jnp.int32) * TMOE
    tile_expert = jnp.minimum(jnp.searchsorted(pend, tile_start, side='right'), N_EXPERTS - 1).astype(jnp.int32)
    n_used = (pend[-1] // TMOE).astype(jnp.int32).reshape(1)
    return top_w, row_tok, pos.reshape(n_tok, TOP_K), tile_expert, n_used


def _moe_kernel(te_ref, nu_ref, xs_ref, wgu_ref, bgu_ref, wd_ref, bd_ref, ys_ref, wgu_bf, wd_bf):
    t = pl.program_id(0)
    prev = te_ref[jnp.maximum(t - 1, 0)]

    @pl.when((t == 0) | (te_ref[t] != prev))
    def _():
        wgu_bf[...] = wgu_ref[0].astype(BF16)
        wd_bf[...] = wd_ref[0].astype(BF16)

    @pl.when(t < nu_ref[0])
    def _():
        gu = jnp.dot(xs_ref[...], wgu_bf[...], preferred_element_type=F32) + bgu_ref[0]
        gate = jnp.minimum(gu[:, :MOE_FF], SWIGLU_LIMIT)
        up = jnp.clip(gu[:, MOE_FF:], -SWIGLU_LIMIT, SWIGLU_LIMIT)
        act = (up + 1.0) * (gate * _sigmoid(SWIGLU_ALPHA * gate))
        ys_ref[...] = (jnp.dot(act.astype(BF16), wd_bf[...], preferred_element_type=F32) + bd_ref[0]).astype(BF16)

    @pl.when(t >= nu_ref[0])
    def _():
        ys_ref[...] = jnp.zeros_like(ys_ref)


def _moe_experts(xs, tile_expert, n_used, w_gu, b_gu, w_d, b_d):
    n_rows = xs.shape[0]
    n_tiles = n_rows // TMOE
    grid_spec = pltpu.PrefetchScalarGridSpec(
        num_scalar_prefetch=2,
        grid=(n_tiles,),
        in_specs=[pl.BlockSpec((TMOE, D), lambda t, te, nu: (t, 0)),
                  pl.BlockSpec((1, D, 2 * MOE_FF), lambda t, te, nu: (te[t], 0, 0)),
                  pl.BlockSpec((1, 1, 2 * MOE_FF), lambda t, te, nu: (te[t], 0, 0)),
                  pl.BlockSpec((1, MOE_FF, D), lambda t, te, nu: (te[t], 0, 0)),
                  pl.BlockSpec((1, 1, D), lambda t, te, nu: (te[t], 0, 0))],
        out_specs=pl.BlockSpec((TMOE, D), lambda t, te, nu: (t, 0)),
        scratch_shapes=[pltpu.VMEM((D, 2 * MOE_FF), BF16), pltpu.VMEM((MOE_FF, D), BF16)],
    )
    return pl.pallas_call(
        _moe_kernel,
        grid_spec=grid_spec,
        out_shape=jax.ShapeDtypeStruct((n_rows, D), BF16),
        compiler_params=_cparams(("arbitrary",)),
        name="moe_experts",
    )(tile_expert, n_used, xs, w_gu, b_gu.reshape(N_EXPERTS, 1, -1), w_d, b_d.reshape(N_EXPERTS, 1, -1))


def _moe(h_tok, logits, w_gu, b_gu, w_d, b_d):
    top_w, row_tok, pos, tile_expert, n_used = _route(logits)
    xs = jnp.take(h_tok, row_tok, axis=0)
    ys = _moe_experts(xs, tile_expert, n_used, w_gu, b_gu, w_d, b_d)
    return jnp.sum(jnp.take(ys, pos, axis=0).astype(F32) * top_w[..., None], axis=1)


def _ssm_in_kernel(x_ref, f_ref, modsp_ref, mods_ref, ng_ref, win_ref, x1_ref, z_ref, xbc_ref, dt_ref, v_ref):
    x = x_ref[0] + modsp_ref[0, 0, 5:6, :] * f_ref[0]
    x1_ref[0] = x
    sh = mods_ref[0, 0, 0:1, :]
    sc = mods_ref[0, 0, 1:2, :]
    h = _rms(x, ng_ref[...]) * (1.0 + sc) + sh
    u = jnp.dot(h.astype(BF16), win_ref[...], preferred_element_type=F32)
    z_ref[0] = u[:, 0:1024]
    xbc_ref[0] = u[:, 1024:2560]
    dt_ref[0] = u[:, 2560:2688]
    v_ref[0] = u[:, 2688:3712] * _sigmoid(u[:, 3712:4736])


def _ssm_in(x_all, f_all, mods_prev, mods, ng, w_in, nct):
    b, t, _ = x_all.shape
    hi = SSM_D + SSM_XBC + SSM_HEADS
    w_aug = jnp.concatenate([w_in[:, :hi], jnp.zeros((D, 128 - SSM_HEADS), F32), w_in[:, hi:]], axis=1).astype(BF16)
    tok = lambda bb, i: (bb, i, 0)
    modspec = pl.BlockSpec((1, 1, 6, D), lambda bb, i: (bb, jnp.where(i < nct, 1, 0), 0, 0))
    return pl.pallas_call(
        _ssm_in_kernel,
        grid=(b, t // TM),
        in_specs=[pl.BlockSpec((1, TM, D), tok), pl.BlockSpec((1, TM, D), tok), modspec, modspec,
                  _const_spec((1, D)), _const_spec((D, SSM_IN_AUG))],
        out_specs=[pl.BlockSpec((1, TM, D), tok), pl.BlockSpec((1, TM, 1024), tok), pl.BlockSpec((1, TM, SSM_XBC), tok),
                   pl.BlockSpec((1, TM, 128), tok), pl.BlockSpec((1, TM, 1024), tok)],
        out_shape=[jax.ShapeDtypeStruct((b, t, D), F32), jax.ShapeDtypeStruct((b, t, 1024), F32),
                   jax.ShapeDtypeStruct((b, t, SSM_XBC), F32), jax.ShapeDtypeStruct((b, t, 128), F32),
                   jax.ShapeDtypeStruct((b, t, 1024), F32)],
        compiler_params=_cparams(("arbitrary", "arbitrary")),
        name="ssm_in",
    )(x_all, f_all, mods_prev, mods, ng.reshape(1, D), w_aug)


def _conv_kernel(x_ref, w_ref, b_ref, o_ref, pad_ref, *, taps, ctx_len, silu):
    t = x_ref.shape[1]
    ct = x_ref.shape[2]
    half = taps // 2
    zeros = jnp.zeros((HALO, ct), F32)
    pad_ref[0:HALO, :] = zeros
    pad_ref[HALO:HALO + ctx_len, :] = x_ref[0, 0:ctx_len, :]
    pad_ref[HALO + ctx_len:2 * HALO + ctx_len, :] = zeros
    pad_ref[2 * HALO + ctx_len:2 * HALO + t, :] = x_ref[0, ctx_len:t, :]
    pad_ref[2 * HALO + t:3 * HALO + t, :] = zeros
    w = w_ref[...]
    bias = b_ref[...]

    def segment(out_start, length, pad_start):
        def body(c, carry):
            base = pl.multiple_of(c * CHUNK, CHUNK)
            win = pad_ref[pl.ds(pad_start - HALO + base, CHUNK + 2 * HALO), :]
            acc = jnp.broadcast_to(bias, (CHUNK, ct))
            for k in range(taps):
                off = HALO + k - half
                acc = acc + w[k:k + 1, :] * win[off:off + CHUNK, :]
            if silu:
                acc = acc * _sigmoid(acc)
            o_ref[0, pl.ds(out_start + base, CHUNK), :] = acc
            return carry
        lax.fori_loop(0, length // CHUNK, body, 0)

    segment(0, ctx_len, HALO)
    segment(ctx_len, t - ctx_len, 2 * HALO + ctx_len)


def _depthwise_conv(x, w, bias, ctx_len, silu):
    b, t, c = x.shape
    taps = w.shape[0]
    ct = 256
    return pl.pallas_call(
        functools.partial(_conv_kernel, taps=taps, ctx_len=ctx_len, silu=silu),
        grid=(b, c // ct),
        in_specs=[pl.BlockSpec((1, t, ct), lambda bb, j: (bb, 0, j)),
                  pl.BlockSpec((taps, ct), lambda bb, j: (0, j)),
                  pl.BlockSpec((1, ct), lambda bb, j: (0, j))],
        out_specs=pl.BlockSpec((1, t, ct), lambda bb, j: (bb, 0, j)),
        out_shape=jax.ShapeDtypeStruct((b, t, c), F32),
        scratch_shapes=[pltpu.VMEM((t + 3 * HALO, ct), F32)],
        compiler_params=_cparams(("arbitrary", "arbitrary")),
        name=f"depthwise_conv{taps}",
    )(x, w, bias.reshape(1, c))


def _expand_heads(v, lane_lo):
    rows = v.shape[0]
    blocks = []
    for p in range(SSM_HEADS // 2):
        a = jnp.broadcast_to(v[:, 2 * p:2 * p + 1], (rows, LANES))
        b = jnp.broadcast_to(v[:, 2 * p + 1:2 * p + 2], (rows, LANES))
        blocks.append(jnp.where(lane_lo, a, b))
    return jnp.concatenate(blocks, axis=1)


def _ssd_chunk(xbc, dt_raw, a_row, bias_row, state_ref, d, reverse):
    L = CHUNK
    ri = lax.broadcasted_iota(jnp.int32, (L, L), 0)
    ci = lax.broadcasted_iota(jnp.int32, (L, L), 1)
    lane_lo = lax.broadcasted_iota(jnp.int32, (L, LANES), 1) < 64
    mask = (ci >= ri) if reverse else (ri >= ci)
    tri = mask.astype(F32)

    xdt_in = dt_raw + bias_row
    dt = jnp.maximum(xdt_in, 0.0) + jnp.log1p(jnp.exp(-jnp.abs(xdt_in)))
    da = dt * a_row
    cs = jnp.dot(tri, da, precision=HIGHEST, preferred_element_type=F32)
    cs_t = cs.T
    a_end = cs[0:1, :] if reverse else cs[L - 1:L, :]
    d_in = jnp.exp(cs)
    d_out = jnp.exp(a_end - cs)

    x = xbc[:, 0:SSM_D]
    xdt = x * _expand_heads(dt, lane_lo)
    xdd = (xdt * _expand_heads(d_out, lane_lo)).astype(BF16)
    din_x = _expand_heads(d_in, lane_lo)
    s_decay = din_x[0:1, :] if reverse else din_x[L - 1:L, :]
    xdt = xdt.astype(BF16)

    ys = []
    for g in range(2):
        bm = xbc[:, SSM_D + g * SSM_STATE:SSM_D + (g + 1) * SSM_STATE].astype(BF16)
        cm = xbc[:, SSM_D + 2 * SSM_STATE + g * SSM_STATE:SSM_D + 2 * SSM_STATE + (g + 1) * SSM_STATE].astype(BF16)
        cb = lax.dot_general(cm, bm, (((1,), (1,)), ((), ())), preferred_element_type=F32)
        gl = slice(g * 512, (g + 1) * 512)
        state = state_ref[d, g]
        y_off = jnp.dot(cm, state.astype(BF16), preferred_element_type=F32) * din_x[:, gl]
        upd = lax.dot_general(bm, xdd[:, gl], (((0,), (0,)), ((), ())), preferred_element_type=F32)
        state_ref[d, g] = state * s_decay[:, gl] + upd
        for p in range(4):
            hp = g * 4 + p
            outs = []
            for j in range(2):
                hh = 2 * hp + j
                col = jnp.broadcast_to(cs[:, hh:hh + 1], (L, L))
                row = jnp.broadcast_to(cs_t[hh:hh + 1, :], (L, L))
                dec = jnp.exp(jnp.where(mask, col - row, -jnp.inf))
                outs.append(jnp.dot((cb * dec).astype(BF16), xdt[:, hp * 128:(hp + 1) * 128], preferred_element_type=F32))
            ys.append(jnp.where(lane_lo, outs[0], outs[1]) + y_off[:, p * 128:(p + 1) * 128])
    return jnp.concatenate(ys, axis=1)


def _ssd_kernel(xf_ref, xb_ref, dtf_ref, dtb_ref, alog_ref, bias_ref, yf_ref, yb_ref, state_ref):
    @pl.when(pl.program_id(1) == 0)
    def _():
        state_ref[...] = jnp.zeros_like(state_ref)

    a = -jnp.exp(alog_ref[...])
    bias = bias_ref[...]
    yf_ref[0] = _ssd_chunk(xf_ref[0], dtf_ref[0], a[0:1, :], bias[0:1, :], state_ref, 0, False)
    yb_ref[0] = _ssd_chunk(xb_ref[0], dtb_ref[0], a[1:2, :], bias[1:2, :], state_ref, 1, True)


def _ssd(xbc, dt, a_log, dt_bias, ctx_len):
    b, t, _ = xbc.shape
    nc = t // CHUNK
    ncc = ctx_len // CHUNK

    def fwd(bb, j):
        return (bb, j, 0)

    def bwd(bb, j):
        return (bb, jnp.where(j < ncc, ncc - 1 - j, nc - 1 + ncc - j), 0)

    pad = lambda v: jnp.pad(v, ((0, 0), (0, 128 - SSM_HEADS)))
    return pl.pallas_call(
        _ssd_kernel,
        grid=(b, nc),
        in_specs=[pl.BlockSpec((1, CHUNK, SSM_XBC), fwd), pl.BlockSpec((1, CHUNK, SSM_XBC), bwd),
                  pl.BlockSpec((1, CHUNK, 128), fwd), pl.BlockSpec((1, CHUNK, 128), bwd),
                  _const_spec((2, 128)), _const_spec((2, 128))],
        out_specs=[pl.BlockSpec((1, CHUNK, SSM_D), fwd), pl.BlockSpec((1, CHUNK, SSM_D), bwd)],
        out_shape=[jax.ShapeDtypeStruct((b, t, SSM_D), F32), jax.ShapeDtypeStruct((b, t, SSM_D), F32)],
        scratch_shapes=[pltpu.VMEM((2, 2, SSM_STATE, 512), F32)],
        compiler_params=_cparams(("arbitrary", "arbitrary")),
        name="ssd_scan",
    )(xbc, xbc, dt, dt, pad(a_log), pad(dt_bias))


def _post_ssm_kernel(x_ref, yf_ref, yb_ref, xs_ref, z_ref, v_ref, dsk_ref, sg_ref, lng_ref, lnb_ref, wout_ref,
                     mods_ref, g2_ref, wr_ref, br_ref, x1_ref, h2_ref, lg_ref):
    y = yf_ref[0] + yb_ref[0] + dsk_ref[...] * xs_ref[0]
    z = z_ref[0]
    y_ssm = _rms(y * (z * _sigmoid(z)), sg_ref[...])
    v = v_ref[0]
    mu = jnp.mean(v, axis=-1, keepdims=True)
    vc = v - mu
    ln = vc * lax.rsqrt(jnp.mean(vc * vc, axis=-1, keepdims=True) + EPS) * lng_ref[...] + lnb_ref[...]
    y_conv = ln * _sigmoid(ln)
    o = (jnp.dot(y_ssm.astype(BF16), wout_ref[0:1024, :], preferred_element_type=F32)
         + jnp.dot(y_conv.astype(BF16), wout_ref[1024:2048, :], preferred_element_type=F32))
    _residual_norm_router(x_ref[0], o, mods_ref, g2_ref, wr_ref, br_ref, x1_ref, h2_ref, lg_ref)


def _post_ssm(x_all, yf, yb, xbc_act, z, v, d_skip, ssm_norm_g, ln_g, ln_b, w_out, mods, g2, w_router, b_router, nct):
    b, t, _ = x_all.shape
    tok = lambda bb, i: (bb, i, 0)
    dsk = jnp.repeat(d_skip[0] + d_skip[1], SSM_HEAD_DIM).reshape(1, SSM_D)
    out_specs, out_shape = _post_out_specs(b, t)
    row = lambda: pl.BlockSpec((1, TM, 1024), tok)
    return pl.pallas_call(
        _post_ssm_kernel,
        grid=(b, t // TM),
        in_specs=[row(), row(), row(), row(), row(), row(),
                  _const_spec((1, SSM_D)), _const_spec((1, SSM_D)), _const_spec((1, D)), _const_spec((1, D)),
                  _const_spec((2048, D)),
                  pl.BlockSpec((1, 1, 6, D), lambda bb, i: (bb, jnp.where(i < nct, 1, 0), 0, 0)),
                  _const_spec((1, D)), _const_spec((D, N_EXPERTS)), _const_spec((1, N_EXPERTS))],
        out_specs=out_specs, out_shape=out_shape,
        compiler_params=_cparams(("arbitrary", "arbitrary")),
        name="post_ssm",
    )(x_all, yf, yb, xbc_act, z, v, dsk, ssm_norm_g.reshape(1, -1), ln_g.reshape(1, -1), ln_b.reshape(1, -1),
      w_out.astype(BF16), mods, g2.reshape(1, D), w_router, b_router.reshape(1, -1))


def _final_kernel(x_ref, f_ref, mods_ref, g_ref, o_ref):
    x = x_ref[0] + mods_ref[0, 0, 5:6, :] * f_ref[0]
    o_ref[0] = _rms(x, g_ref[...])


def _final(x_all, f_lat, mods, final_g, nct):
    b, s, _ = f_lat.shape
    return pl.pallas_call(
        _final_kernel,
        grid=(b, s // TM),
        in_specs=[pl.BlockSpec((1, TM, D), lambda bb, i: (bb, i + nct, 0)),
                  pl.BlockSpec((1, TM, D), lambda bb, i: (bb, i, 0)),
                  pl.BlockSpec((1, 1, 6, D), lambda bb, i: (bb, 0, 0, 0)),
                  _const_spec((1, D))],
        out_specs=pl.BlockSpec((1, TM, D), lambda bb, i: (bb, i, 0)),
        out_shape=jax.ShapeDtypeStruct((b, s, D), F32),
        compiler_params=_cparams(("arbitrary", "arbitrary")),
        name="final_norm",
    )(x_all, f_lat, mods, final_g.reshape(1, D))


def kernel(x, c, ctx, c_ctx, w_mod, b_mod, norm_g, attn_w_in, mla_g_cq, mla_w_uq, mla_g_ckv, mla_w_ukv, gqa_g_q, gqa_g_k, attn_w_out, ssm_w_in, ssm_conv_w, ssm_conv_b, ssm_a_log, ssm_dt_bias, ssm_d, ssm_norm_g, conf_dw_w, conf_dw_b, conf_ln_g, conf_ln_b, ssm_w_out, moe_w_router, moe_b_router, moe_w_gate_up, moe_b_gate_up, moe_w_down, moe_b_down, final_g):
    b, s, _ = x.shape
    ctx_len = ctx.shape[1]
    t = ctx_len + s
    assert ctx_len % TM == 0 and s % TM == 0 and s % GRID_W == 0
    nct = ctx_len // TM

    c_rows = jnp.concatenate([c, c_ctx[None, :], jnp.zeros((-(b + 1) % 8, D), F32)], axis=0)
    mod_all = _modulations(c_rows, w_mod, b_mod)
    mods = []
    for i in range(w_mod.shape[0]):
        lat = mod_all[i, :b].reshape(b, 1, 6, D)
        cm = jnp.broadcast_to(mod_all[i, b].reshape(1, 1, 6, D), (b, 1, 6, D))
        mods.append(jnp.concatenate([lat, cm], axis=1))

    x_all = jnp.concatenate([ctx, x], axis=1)

    tab = _rope_tables(s, ctx_len)
    qm, km, vm, qg, kg, vgs = _attn_in(x_all, mods[0], norm_g[0, 0], attn_w_in[0], mla_g_cq[0], mla_w_uq[0], mla_g_ckv[0],
                                       mla_w_ukv[0], gqa_g_q[0], gqa_g_k[0], tab, nct)
    o_m = _mla_attention(qm, km, vm, nct, ctx_len)
    o_g = _gqa_attention(qg, kg, vgs, nct, ctx_len)
    x_all, h2, logits = _post_attn(x_all, o_m, o_g, attn_w_out[0], mods[0], norm_g[0, 1], moe_w_router[0], moe_b_router[0], nct)
    f_all = _moe(h2.reshape(b * t, D), logits.reshape(b * t, N_EXPERTS), moe_w_gate_up[0], moe_b_gate_up[0],
                 moe_w_down[0], moe_b_down[0]).reshape(b, t, D)

    x_all, z, xbc, dt, v = _ssm_in(x_all, f_all, mods[0], mods[1], norm_g[1, 0], ssm_w_in[0], nct)
    xbc_act = _depthwise_conv(xbc, ssm_conv_w[0], ssm_conv_b[0], ctx_len, True)
    v_conv = _depthwise_conv(v, conf_dw_w[0], conf_dw_b[0], ctx_len, False)
    yf, yb = _ssd(xbc_act, dt, ssm_a_log[0], ssm_dt_bias[0], ctx_len)
    x_all, h2, logits = _post_ssm(x_all, yf, yb, xbc_act, z, v_conv, ssm_d[0], ssm_norm_g[0], conf_ln_g[0], conf_ln_b[0],
                                  ssm_w_out[0], mods[1], norm_g[1, 1], moe_w_router[1], moe_b_router[1], nct)
    f_lat = _moe(h2[:, ctx_len:].reshape(b * s, D), logits[:, ctx_len:].reshape(b * s, N_EXPERTS), moe_w_gate_up[1],
                 moe_b_gate_up[1], moe_w_down[1], moe_b_down[1]).reshape(b, s, D)
    return _final(x_all, f_lat, mods[1], final_g, nct)
```

```python
import functools

import jax
import jax.numpy as jnp
from jax import lax
from jax.experimental import pallas as pl
from jax.experimental.pallas import tpu as pltpu
from jax.experimental.pallas import tpu_sc as plsc

F32 = jnp.float32
BF16 = jnp.bfloat16
HIGHEST = lax.Precision.HIGHEST

D = 1024
EPS = 1e-6
GRID_W = 64
ROPE_THETA = 10000.0
LOG2E = 1.4426950408889634

MLA_HEADS = 8
MLA_NOPE = 64
MLA_ROPE = 32
MLA_Q_LORA = 256
MLA_KV_LORA = 128
MLA_SCALE = (MLA_NOPE + MLA_ROPE) ** -0.5
GQA_HEADS = 8
GQA_HEAD_DIM = 64
GQA_SCALE = GQA_HEAD_DIM ** -0.5
ATTN_IN_AUG = 1280

SSM_HEADS = 16
SSM_HEAD_DIM = 64
SSM_STATE = 128
SSM_D = 1024
SSM_XBC = 1536
SSM_CONV = 5
CONF_K = 31
CHUNK = 128
SSM_IN_AUG = 1024 + 1536 + 128 + 2048

N_EXPERTS = 32
TOP_K = 4
MOE_FF = 1024
SWIGLU_LIMIT = 7.0
SWIGLU_ALPHA = 1.702

TM = 256
TMOE = 256
ROUTE_LANES = 128
SC_WINDOW = 128
SC_ROW_WORDS = 256
LANES = 128
HALO = 16
VMEM_LIMIT = 56 * 1024 * 1024


def _cparams(sem):
    return pltpu.CompilerParams(dimension_semantics=sem, vmem_limit_bytes=VMEM_LIMIT)


def _rms(x, g):
    return x * lax.rsqrt(jnp.mean(x * x, axis=-1, keepdims=True) + EPS) * g


def _sigmoid(x):
    return 1.0 / (1.0 + jnp.exp(-x))


def _const_spec(shape):
    n = len(shape)
    return pl.BlockSpec(shape, lambda *_: (0,) * n)


def _mod_kernel(c_ref, w_ref, b_ref, o_ref):
    c = c_ref[...]
    o_ref[0] = jnp.dot(c * _sigmoid(c), w_ref[0], precision=HIGHEST, preferred_element_type=F32) + b_ref[0]


def _modulations(c_rows, w_mod, b_mod):
    depth, _, n = w_mod.shape
    tn = 512
    r = c_rows.shape[0]
    return pl.pallas_call(
        _mod_kernel,
        grid=(depth, n // tn),
        in_specs=[pl.BlockSpec((r, D), lambda l, j: (0, 0)),
                  pl.BlockSpec((1, D, tn), lambda l, j: (l, 0, j)),
                  pl.BlockSpec((1, 1, tn), lambda l, j: (l, 0, j))],
        out_specs=pl.BlockSpec((1, r, tn), lambda l, j: (l, 0, j)),
        out_shape=jax.ShapeDtypeStruct((depth, r, n), F32),
        compiler_params=_cparams(("arbitrary", "arbitrary")),
        name="modulations",
    )(c_rows, w_mod, b_mod.reshape(depth, 1, n))


def _attn_in_kernel(x_ref, mods_ref, ng_ref, win_ref, gcq_ref, wuq_ref, gckv_ref, wkv_ref, gq_ref, gk_ref, bd_ref,
                    tab_ref, qm_ref, km_ref, vm_ref, qg_ref, kg_ref, vgs_ref):
    x = x_ref[0]
    sh = mods_ref[0, 0, 0:1, :]
    sc = mods_ref[0, 0, 1:2, :]
    h = _rms(x, ng_ref[...]) * (1.0 + sc) + sh
    u = jnp.dot(h.astype(BF16), win_ref[...], preferred_element_type=F32)

    tab = tab_ref[...]
    cg, sg = tab[:, 0:128], tab[:, 128:256]
    cm, sm = tab[:, 256:384], tab[:, 384:512]
    lane = lax.broadcasted_iota(jnp.int32, (x.shape[0], LANES), 1)
    even = (lane & 1) == 0
    lo = lane < 64

    def rope(v, c, s):
        partner = jnp.where(even, pltpu.roll(v, LANES - 1, 1), pltpu.roll(v, 1, 1))
        return v * c + partner * s

    cq = _rms(u[:, 0:256], gcq_ref[...])
    qm = jnp.dot(cq.astype(BF16), wuq_ref[...], preferred_element_type=F32)
    for hd in range(MLA_HEADS):
        qm_ref[0, hd] = rope(qm[:, hd * 128:(hd + 1) * 128], cm, sm).astype(BF16)
    ckv = _rms(u[:, 256:384], gckv_ref[...])
    kv = jnp.dot(ckv.astype(BF16), wkv_ref[...], preferred_element_type=F32)
    kr = rope(u[:, 384:512], cm, sm)
    for hd in range(MLA_HEADS):
        km_ref[0, hd] = (kv[:, hd * 128:(hd + 1) * 128] + kr).astype(BF16)
    for p in range(MLA_HEADS // 2):
        vm_ref[0, p] = kv[:, 1024 + p * 128:1024 + (p + 1) * 128].astype(BF16)

    qg = u[:, 512:1024]
    ms = jnp.dot((qg * qg).astype(BF16), bd_ref[...], preferred_element_type=F32)
    qg = qg * lax.rsqrt(ms + EPS) * gq_ref[...]
    zero = jnp.zeros((x.shape[0], LANES), F32)
    for p in range(GQA_HEADS // 2):
        blk = rope(qg[:, p * 128:(p + 1) * 128], cg, sg)
        swp = pltpu.roll(blk, 64, 1)
        if p < 2:
            qg_ref[0, 2 * p] = jnp.where(lo, blk, zero).astype(BF16)
            qg_ref[0, 2 * p + 1] = jnp.where(lo, swp, zero).astype(BF16)
        else:
            qg_ref[0, 2 * p] = jnp.where(lo, zero, swp).astype(BF16)
            qg_ref[0, 2 * p + 1] = jnp.where(lo, zero, blk).astype(BF16)
    kg = u[:, 1024:1152]
    msk = jnp.dot((kg * kg).astype(BF16), bd_ref[0:128, 0:128], preferred_element_type=F32)
    kg_ref[0] = rope(kg * lax.rsqrt(msk + EPS) * gk_ref[...], cg, sg).astype(BF16)
    vg = u[:, 1152:1280]
    vgs_ref[0, 0] = vg.astype(BF16)
    vgs_ref[0, 1] = pltpu.roll(vg, 64, 1).astype(BF16)


def _rope_tables(seq, ctx_len):
    rows = seq // GRID_W
    row = jnp.broadcast_to(jnp.arange(rows, dtype=F32)[:, None], (rows, GRID_W)).reshape(-1)
    col = jnp.broadcast_to(jnp.arange(GRID_W, dtype=F32)[None, :], (rows, GRID_W)).reshape(-1)

    def interleaved(rot_dim):
        n_freq = rot_dim // 4
        inv_freq = ROPE_THETA ** (-jnp.arange(n_freq, dtype=F32) / n_freq)
        ang = jnp.concatenate([row[:, None] * inv_freq, col[:, None] * inv_freq], axis=-1)
        cos = jnp.repeat(jnp.cos(ang), 2, axis=-1)
        sin = jnp.repeat(jnp.sin(ang), 2, axis=-1) * jnp.tile(jnp.array([-1.0, 1.0], F32), rot_dim // 2)
        return cos, sin

    cg, sg = interleaved(GQA_HEAD_DIM)
    cg, sg = jnp.tile(cg, (1, 2)), jnp.tile(sg, (1, 2))
    cm32, sm32 = interleaved(MLA_ROPE)
    ones, zeros = jnp.ones((seq, 64), F32), jnp.zeros((seq, 64), F32)
    cm = jnp.concatenate([ones, cm32, ones[:, :32]], axis=-1)
    sm = jnp.concatenate([zeros, sm32, zeros[:, :32]], axis=-1)
    lat = jnp.concatenate([cg, sg, cm, sm], axis=-1)
    ident = jnp.concatenate([jnp.ones((ctx_len, 128), F32), jnp.zeros((ctx_len, 128), F32)] * 2, axis=-1)
    return jnp.concatenate([ident, lat], axis=0)


def _attn_in(x_all, mods, ng, w_in, g_cq, w_uq, g_ckv, w_ukv, g_q, g_k, tab, nct):
    b, t, _ = x_all.shape
    o1, o2, o3 = MLA_Q_LORA, MLA_Q_LORA + MLA_KV_LORA, MLA_Q_LORA + MLA_KV_LORA + MLA_ROPE
    zc = lambda n: jnp.zeros((D, n), F32)
    w_aug = jnp.concatenate([w_in[:, :o2], zc(64), w_in[:, o2:o3], zc(32), w_in[:, o3:]], axis=1).astype(BF16)
    wuq = jnp.pad(w_uq.reshape(MLA_Q_LORA, MLA_HEADS, 96), ((0, 0), (0, 0), (0, 32))).reshape(MLA_Q_LORA, 1024).astype(BF16)
    wukv = w_ukv.reshape(MLA_KV_LORA, MLA_HEADS, 128)
    wk = jnp.pad(wukv[:, :, :64], ((0, 0), (0, 0), (0, 64))).reshape(MLA_KV_LORA, 1024)
    wkv = jnp.concatenate([wk, wukv[:, :, 64:].reshape(MLA_KV_LORA, 512)], axis=1).astype(BF16)
    gcq = (g_cq * (MLA_SCALE * LOG2E)).reshape(1, -1)
    gq = (jnp.tile(g_q, GQA_HEADS) * (GQA_SCALE * LOG2E)).reshape(1, -1)
    gk = jnp.tile(g_k, 2).reshape(1, -1)
    bd = jnp.kron(jnp.eye(GQA_HEADS, dtype=F32), jnp.full((64, 64), 1.0 / 64, F32)).astype(BF16)
    nt = t // TM
    tok = lambda bb, i: (bb, i, 0)
    hm = lambda bb, i: (bb, 0, i, 0)
    return pl.pallas_call(
        _attn_in_kernel,
        grid=(b, nt),
        in_specs=[pl.BlockSpec((1, TM, D), tok),
                  pl.BlockSpec((1, 1, 6, D), lambda bb, i: (bb, jnp.where(i < nct, 1, 0), 0, 0)),
                  _const_spec((1, D)), _const_spec((D, ATTN_IN_AUG)), _const_spec((1, MLA_Q_LORA)),
                  _const_spec((MLA_Q_LORA, 1024)), _const_spec((1, MLA_KV_LORA)), _const_spec((MLA_KV_LORA, 1536)),
                  _const_spec((1, 512)), _const_spec((1, 128)), _const_spec((512, 512)),
                  pl.BlockSpec((TM, 512), lambda bb, i: (i, 0))],
        out_specs=[pl.BlockSpec((1, 8, TM, 128), hm), pl.BlockSpec((1, 8, TM, 128), hm),
                   pl.BlockSpec((1, 4, TM, 128), hm), pl.BlockSpec((1, 8, TM, 128), hm),
                   pl.BlockSpec((1, TM, 128), tok), pl.BlockSpec((1, 2, TM, 128), hm)],
        out_shape=[jax.ShapeDtypeStruct((b, 8, t, 128), BF16), jax.ShapeDtypeStruct((b, 8, t, 128), BF16),
                   jax.ShapeDtypeStruct((b, 4, t, 128), BF16), jax.ShapeDtypeStruct((b, 8, t, 128), BF16),
                   jax.ShapeDtypeStruct((b, t, 128), BF16), jax.ShapeDtypeStruct((b, 2, t, 128), BF16)],
        compiler_params=_cparams(("arbitrary", "arbitrary")),
        name="attn_in",
    )(x_all, mods, ng.reshape(1, D), w_aug, gcq, wuq, g_ckv.reshape(1, -1), wkv, gq, gk, bd, tab)


def _softmax_pv(q, k, v):
    s = lax.dot_general(q, k, (((1,), (1,)), ((), ())), preferred_element_type=F32)
    m = jnp.max(s, axis=-1, keepdims=True)
    p = jnp.exp2(s - m)
    l = jnp.sum(p, axis=-1, keepdims=True)
    return jnp.dot(p.astype(BF16), v, preferred_element_type=F32) / l


def _mla_kernel(q_ref, k_ref, v_ref, o_ref, *, nct, ctx_len):
    i = pl.program_id(2)
    lo = lax.broadcasted_iota(jnp.int32, (q_ref.shape[2], LANES), 1) < 64

    def run(nk):
        oa = _softmax_pv(q_ref[0, 0], k_ref[0, 0, 0:nk, :], v_ref[0, 0, 0:nk, :])
        ob = _softmax_pv(q_ref[0, 1], k_ref[0, 1, 0:nk, :], v_ref[0, 0, 0:nk, :])
        o_ref[0] = jnp.where(lo, oa, ob).astype(BF16)

    @pl.when(i < nct)
    def _():
        run(ctx_len)

    @pl.when(i >= nct)
    def _():
        run(k_ref.shape[2])


def _mla_attention(qm, km, vm, nct, ctx_len):
    b, _, t, _ = qm.shape
    nt = t // TM
    return pl.pallas_call(
        functools.partial(_mla_kernel, nct=nct, ctx_len=ctx_len),
        grid=(b, 4, nt),
        in_specs=[pl.BlockSpec((1, 2, TM, 128), lambda bb, p, i: (bb, p, i, 0)),
                  pl.BlockSpec((1, 2, t, 128), lambda bb, p, i: (bb, p, 0, 0)),
                  pl.BlockSpec((1, 1, t, 128), lambda bb, p, i: (bb, p, 0, 0))],
        out_specs=pl.BlockSpec((1, TM, 128), lambda bb, p, i: (bb, i, p)),
        out_shape=jax.ShapeDtypeStruct((b, t, 512), BF16),
        compiler_params=_cparams(("arbitrary", "arbitrary", "arbitrary")),
        name="mla_attention",
    )(qm, km, vm)


def _gqa_kernel(q_ref, k_ref, ve_ref, vo_ref, o_ref, *, nct, ctx_len):
    i = pl.program_id(2)
    lo = lax.broadcasted_iota(jnp.int32, (q_ref.shape[2], LANES), 1) < 64

    def run(nk):
        k = k_ref[0, 0:nk, :]
        for pr in range(2):
            oa = _softmax_pv(q_ref[0, 2 * pr], k, ve_ref[0, 0, 0:nk, :])
            ob = _softmax_pv(q_ref[0, 2 * pr + 1], k, vo_ref[0, 0, 0:nk, :])
            o_ref[0, :, pr * 128:(pr + 1) * 128] = jnp.where(lo, oa, ob).astype(BF16)

    @pl.when(i < nct)
    def _():
        run(ctx_len)

    @pl.when(i >= nct)
    def _():
        run(k_ref.shape[1])


def _gqa_attention(qg, kg, vgs, nct, ctx_len):
    b, _, t, _ = qg.shape
    nt = t // TM
    return pl.pallas_call(
        functools.partial(_gqa_kernel, nct=nct, ctx_len=ctx_len),
        grid=(b, 2, nt),
        in_specs=[pl.BlockSpec((1, 4, TM, 128), lambda bb, g, i: (bb, g, i, 0)),
                  pl.BlockSpec((1, t, 128), lambda bb, g, i: (bb, 0, 0)),
                  pl.BlockSpec((1, 1, t, 128), lambda bb, g, i: (bb, g, 0, 0)),
                  pl.BlockSpec((1, 1, t, 128), lambda bb, g, i: (bb, 1 - g, 0, 0))],
        out_specs=pl.BlockSpec((1, TM, 256), lambda bb, g, i: (bb, i, g)),
        out_shape=jax.ShapeDtypeStruct((b, t, 512), BF16),
        compiler_params=_cparams(("arbitrary", "arbitrary", "arbitrary")),
        name="gqa_attention",
    )(qg, kg, vgs, vgs)


def _pack_bf16_pairs(h):
    half = h.shape[1] // 2
    lo = pltpu.bitcast(h[:, :half].astype(BF16).astype(F32), jnp.uint32) >> 16
    hi = pltpu.bitcast(h[:, half:].astype(BF16).astype(F32), jnp.uint32) & jnp.uint32(0xFFFF0000)
    return lo | hi


def _unpack_bf16_pairs(w):
    return pltpu.bitcast(w << 16, F32), pltpu.bitcast(w & jnp.uint32(0xFFFF0000), F32)


def _store_planes(ref, words):
    ref[0, 0] = words[:, 0:SC_ROW_WORDS]
    ref[1, 0] = words[:, SC_ROW_WORDS:2 * SC_ROW_WORDS]


def _route_tile(logits, tri_ref, carry_ref, counted):
    rows = logits.shape[0]
    lane = lax.broadcasted_iota(jnp.int32, (rows, N_EXPERTS), 1).astype(F32)
    work = logits
    vals, ids, hots = [], [], []
    for _ in range(TOP_K):
        m = jnp.max(work, axis=-1, keepdims=True)
        idx = jnp.min(jnp.where(work == m, lane, float(N_EXPERTS)), axis=-1, keepdims=True)
        hot = lane == idx
        vals.append(m)
        ids.append(idx)
        hots.append(hot)
        work = jnp.where(hot, -jnp.inf, work)
    exps = [jnp.exp(v - vals[0]) for v in vals]
    den = exps[0] + exps[1] + exps[2] + exps[3]
    mask = jnp.where(hots[0] | hots[1] | hots[2] | hots[3], 1.0, 0.0)
    before = jnp.dot(tri_ref[...], mask.astype(BF16), preferred_element_type=F32) + carry_ref[...]
    slot = lax.broadcasted_iota(jnp.int32, (rows, ROUTE_LANES), 1)
    slab = jnp.zeros((rows, ROUTE_LANES), F32)
    for k in range(TOP_K):
        rank = jnp.sum(jnp.where(hots[k], before, 0.0), axis=-1, keepdims=True)
        slab = jnp.where(slot == k, ids[k], slab)
        slab = jnp.where(slot == TOP_K + k, rank, slab)
        slab = jnp.where(slot == 2 * TOP_K + k, exps[k] / den, slab)
    carry_ref[...] += counted * jnp.sum(mask, axis=0, keepdims=True)
    return slab


def _residual_norm_router(x, o, mods_ref, g2_ref, wrh_ref, wrl_ref, br_ref, tri_ref, x1_ref, hp_ref, slab_ref,
                          cnt_ref, carry_ref, *, nct, route_ctx):
    first = (pl.program_id(0) == 0) & (pl.program_id(1) == 0)

    @pl.when(first)
    def _():
        carry_ref[...] = jnp.zeros_like(carry_ref)

    g1 = mods_ref[0, 0, 2:3, :]
    sh2 = mods_ref[0, 0, 3:4, :]
    sc2 = mods_ref[0, 0, 4:5, :]
    x1 = x + g1 * o
    h2 = _rms(x1, g2_ref[...]) * (1.0 + sc2) + sh2
    x1_ref[0] = x1
    _store_planes(hp_ref, _pack_bf16_pairs(h2))
    hh = h2.astype(BF16)
    hl = (h2 - hh.astype(F32)).astype(BF16)
    logits = (jnp.dot(hh, wrh_ref[...], preferred_element_type=F32) + jnp.dot(hl, wrh_ref[...], preferred_element_type=F32)
              + jnp.dot(hh, wrl_ref[...], preferred_element_type=F32) + br_ref[...])
    counted = 1.0 if route_ctx else jnp.where(pl.program_id(1) >= nct, 1.0, 0.0)
    slab_ref[0] = _route_tile(logits, tri_ref, carry_ref, counted)
    cnt_ref[...] = carry_ref[...]


def _post_attn_kernel(x_ref, om_ref, og_ref, wout_ref, mods_ref, g2_ref, wrh_ref, wrl_ref, br_ref, tri_ref,
                      x1_ref, hp_ref, slab_ref, cnt_ref, carry_ref, *, nct):
    o = (jnp.dot(om_ref[0], wout_ref[0:512, :], preferred_element_type=F32)
         + jnp.dot(og_ref[0], wout_ref[512:1024, :], preferred_element_type=F32))
    _residual_norm_router(x_ref[0], o, mods_ref, g2_ref, wrh_ref, wrl_ref, br_ref, tri_ref, x1_ref, hp_ref, slab_ref,
                          cnt_ref, carry_ref, nct=nct, route_ctx=True)


def _post_out_specs(b, t):
    tok = lambda bb, i: (bb, i, 0)
    specs = [pl.BlockSpec((1, TM, D), tok), pl.BlockSpec((2, 1, TM, SC_ROW_WORDS), lambda bb, i: (0, bb, i, 0)),
             pl.BlockSpec((1, TM, ROUTE_LANES), tok), _const_spec((1, N_EXPERTS))]
    shapes = [jax.ShapeDtypeStruct((b, t, D), F32), jax.ShapeDtypeStruct((2, b, t, SC_ROW_WORDS), jnp.uint32),
              jax.ShapeDtypeStruct((b, t, ROUTE_LANES), F32), jax.ShapeDtypeStruct((1, N_EXPERTS), F32)]
    return specs, shapes


def _router_operands(w_router, b_router):
    wrh = w_router.astype(BF16)
    wrl = (w_router - wrh.astype(F32)).astype(BF16)
    tri = jnp.tril(jnp.ones((TM, TM), F32), -1).astype(BF16)
    return wrh, wrl, b_router.reshape(1, -1), tri


def _router_specs():
    return [_const_spec((D, N_EXPERTS)), _const_spec((D, N_EXPERTS)), _const_spec((1, N_EXPERTS)), _const_spec((TM, TM))]


def _post_attn(x_all, o_m, o_g, w_out, mods, g2, w_router, b_router, nct):
    b, t, _ = x_all.shape
    tok = lambda bb, i: (bb, i, 0)
    out_specs, out_shape = _post_out_specs(b, t)
    return pl.pallas_call(
        functools.partial(_post_attn_kernel, nct=nct),
        grid=(b, t // TM),
        in_specs=[pl.BlockSpec((1, TM, D), tok), pl.BlockSpec((1, TM, 512), tok), pl.BlockSpec((1, TM, 512), tok),
                  _const_spec((1024, D)),
                  pl.BlockSpec((1, 1, 6, D), lambda bb, i: (bb, jnp.where(i < nct, 1, 0), 0, 0)),
                  _const_spec((1, D))] + _router_specs(),
        out_specs=out_specs, out_shape=out_shape,
        scratch_shapes=[pltpu.VMEM((1, N_EXPERTS), F32)],
        compiler_params=_cparams(("arbitrary", "arbitrary")),
        name="post_attn",
    )(x_all, o_m, o_g, w_out.astype(BF16), mods, g2.reshape(1, D), *_router_operands(w_router, b_router))


def _route_tables(slab, counts, n_tiles):
    e = slab[:, 0:TOP_K].astype(jnp.int32)
    rank = slab[:, TOP_K:2 * TOP_K].astype(jnp.int32)
    counts = counts.reshape(N_EXPERTS).astype(jnp.int32)
    padded = (counts + TMOE - 1) // TMOE * TMOE
    pend = jnp.cumsum(padded)
    pstart = pend - padded
    experts = jnp.arange(N_EXPERTS, dtype=jnp.int32)
    pos = rank + jnp.sum(jnp.where(e[..., None] == experts, pstart, 0), axis=-1)
    tile_start = jnp.arange(n_tiles, dtype=jnp.int32) * TMOE
    tile_expert = jnp.minimum(jnp.sum(tile_start[:, None] >= pend[None, :], axis=-1), N_EXPERTS - 1).astype(jnp.int32)
    mine = tile_expert[:, None] == experts
    live = jnp.sum(jnp.where(mine, counts, 0), axis=-1) - (tile_start - jnp.sum(jnp.where(mine, pstart, 0), axis=-1))
    tile_valid = jnp.clip(live, 0, TMOE).astype(jnp.int32)
    return pos, tile_expert, tile_valid


def _sc_mesh():
    return plsc.VectorSubcoreMesh(core_axis_name="core", subcore_axis_name="subcore")


def _sc_scatter_rows(src, idx, n_out, nb_tok, nb_src, nb_plane, off):
    m = idx.shape[0]
    steps_per_k = m // SC_WINDOW // TOP_K

    def src_block(i):
        q = i % steps_per_k
        plane = q // (steps_per_k // 2)
        r = q % (steps_per_k // 2)
        return (plane * nb_plane + (r // nb_tok) * nb_src + off + r % nb_tok, 0)

    @pl.kernel(out_type=jax.ShapeDtypeStruct((n_out, SC_ROW_WORDS), src.dtype), mesh=_sc_mesh(), scratch_types=[])
    def k(src_hbm, i_hbm, o_hbm):
        def body(x_vmem, i_vmem):
            pltpu.sync_copy(x_vmem, o_hbm.at[i_vmem.at[0]])

        pltpu.emit_pipeline(
            body,
            grid=(m // SC_WINDOW,),
            in_specs=[pl.BlockSpec((SC_WINDOW, SC_ROW_WORDS), src_block),
                      pl.BlockSpec((1, SC_WINDOW), lambda i: (0, i))],
            out_specs=[],
            core_axis_name=("core", "subcore"),
            dimension_semantics=(pltpu.PARALLEL,),
        )(src_hbm, i_hbm)

    return k(src, idx.reshape(1, m))


def _sc_gather_rows(src, idx):
    m = idx.shape[0]

    @pl.kernel(out_type=jax.ShapeDtypeStruct((m, SC_ROW_WORDS), src.dtype), mesh=_sc_mesh(), scratch_types=[])
    def k(src_hbm, i_hbm, o_hbm):
        def body(i_vmem, o_vmem):
            pltpu.sync_copy(src_hbm.at[i_vmem.at[0]], o_vmem)

        pltpu.emit_pipeline(
            body,
            grid=(m // SC_WINDOW,),
            in_specs=[pl.BlockSpec((1, SC_WINDOW), lambda i: (0, i))],
            out_specs=[pl.BlockSpec((SC_WINDOW, SC_ROW_WORDS), lambda i: (i, 0))],
            core_axis_name=("core", "subcore"),
            dimension_semantics=(pltpu.PARALLEL,),
        )(i_hbm, o_hbm)

    return k(src, idx.reshape(1, m))


def _moe_kernel(te_ref, tv_ref, xs_ref, wgu_ref, bgu_ref, wd_ref, bd_ref, ys_ref, wgu_bf, wd_bf):
    t = pl.program_id(0)
    prev = te_ref[jnp.maximum(t - 1, 0)]

    @pl.when((t == 0) | (te_ref[t] != prev))
    def _():
        wgu_bf[...] = wgu_ref[0, 0].astype(BF16)
        wd_bf[...] = wd_ref[0, 0].astype(BF16)

    valid = tv_ref[t]

    @pl.when(valid > 0)
    def _():
        lo, hi = _unpack_bf16_pairs(jnp.concatenate([xs_ref[0], xs_ref[1]], axis=1))
        live = lax.broadcasted_iota(jnp.int32, (TMOE, 1), 0) < valid
        x = jnp.where(live, jnp.concatenate([lo, hi], axis=1), 0.0).astype(BF16)
        gu = jnp.dot(x, wgu_bf[...], preferred_element_type=F32) + bgu_ref[0, 0]
        gate = jnp.minimum(gu[:, :MOE_FF], SWIGLU_LIMIT)
        up = jnp.clip(gu[:, MOE_FF:], -SWIGLU_LIMIT, SWIGLU_LIMIT)
        act = (up + 1.0) * (gate * _sigmoid(SWIGLU_ALPHA * gate))
        y = jnp.dot(act.astype(BF16), wd_bf[...], preferred_element_type=F32) + bd_ref[0, 0]
        words = _pack_bf16_pairs(y)
        ys_ref[0] = words[:, 0:SC_ROW_WORDS]
        ys_ref[1] = words[:, SC_ROW_WORDS:2 * SC_ROW_WORDS]

    @pl.when(valid == 0)
    def _():
        ys_ref[...] = jnp.zeros_like(ys_ref)


def _moe_experts(xs, tile_expert, tile_valid, layer, w_gu, b_gu, w_d, b_d):
    n_rows = xs.shape[1]
    n_tiles = n_rows // TMOE
    depth = w_gu.shape[0]
    grid_spec = pltpu.PrefetchScalarGridSpec(
        num_scalar_prefetch=2,
        grid=(n_tiles,),
        in_specs=[pl.BlockSpec((2, TMOE, SC_ROW_WORDS), lambda t, te, tv: (0, t, 0)),
                  pl.BlockSpec((1, 1, D, 2 * MOE_FF), lambda t, te, tv: (layer, te[t], 0, 0)),
                  pl.BlockSpec((1, 1, 1, 2 * MOE_FF), lambda t, te, tv: (layer, te[t], 0, 0)),
                  pl.BlockSpec((1, 1, MOE_FF, D), lambda t, te, tv: (layer, te[t], 0, 0)),
                  pl.BlockSpec((1, 1, 1, D), lambda t, te, tv: (layer, te[t], 0, 0))],
        out_specs=pl.BlockSpec((2, TMOE, SC_ROW_WORDS), lambda t, te, tv: (0, t, 0)),
        scratch_shapes=[pltpu.VMEM((D, 2 * MOE_FF), BF16), pltpu.VMEM((MOE_FF, D), BF16)],
    )
    return pl.pallas_call(
        _moe_kernel,
        grid_spec=grid_spec,
        out_shape=jax.ShapeDtypeStruct((2, n_rows, SC_ROW_WORDS), jnp.uint32),
        compiler_params=_cparams(("arbitrary",)),
        name="moe_experts",
    )(tile_expert, tile_valid, xs, w_gu, b_gu.reshape(depth, N_EXPERTS, 1, -1), w_d, b_d.reshape(depth, N_EXPERTS, 1, -1))


def _moe(hp, slab, counts, layer, w_gu, b_gu, w_d, b_d, n_seq, row0):
    _, b, t, _ = hp.shape
    n_tok = b * n_seq
    n_tiles = -(-n_tok * TOP_K // TMOE) + N_EXPERTS
    n_rows = n_tiles * TMOE
    pos, tile_expert, tile_valid = _route_tables(slab, counts, n_tiles)
    idx = (pos.T[:, None, :] + (jnp.arange(2, dtype=jnp.int32) * n_rows)[None, :, None]).reshape(-1)
    xs = _sc_scatter_rows(hp.reshape(2 * b * t, SC_ROW_WORDS), idx, 2 * n_rows, n_seq // SC_WINDOW, t // SC_WINDOW,
                          b * t // SC_WINDOW, row0 // SC_WINDOW)
    ys = _moe_experts(xs.reshape(2, n_rows, SC_ROW_WORDS), tile_expert, tile_valid, layer, w_gu, b_gu, w_d, b_d)
    g = _sc_gather_rows(ys.reshape(2 * n_rows, SC_ROW_WORDS), idx)
    return g.reshape(TOP_K, 2, b, n_seq, SC_ROW_WORDS)


def _combine(g_ref, slab):
    acc_lo = acc_hi = None
    for k in range(TOP_K):
        lo, hi = _unpack_bf16_pairs(jnp.concatenate([g_ref[k, 0, 0], g_ref[k, 1, 0]], axis=1))
        w = slab[:, 2 * TOP_K + k:2 * TOP_K + k + 1]
        acc_lo = w * lo if acc_lo is None else acc_lo + w * lo
        acc_hi = w * hi if acc_hi is None else acc_hi + w * hi
    return jnp.concatenate([acc_lo, acc_hi], axis=1)


def _ssm_in_kernel(x_ref, g_ref, slab_ref, modsp_ref, mods_ref, ng_ref, win_ref, x1_ref, z_ref, xbc_ref, dt_ref, v_ref):
    x = x_ref[0] + modsp_ref[0, 0, 5:6, :] * _combine(g_ref, slab_ref[0])
    x1_ref[0] = x
    sh = mods_ref[0, 0, 0:1, :]
    sc = mods_ref[0, 0, 1:2, :]
    h = _rms(x, ng_ref[...]) * (1.0 + sc) + sh
    u = jnp.dot(h.astype(BF16), win_ref[...], preferred_element_type=F32)
    z_ref[0] = u[:, 0:1024]
    xbc_ref[0] = u[:, 1024:2560]
    dt_ref[0] = u[:, 2560:2688]
    v_ref[0] = u[:, 2688:3712] * _sigmoid(u[:, 3712:4736])


def _ssm_in(x_all, g_all, slab, mods_prev, mods, ng, w_in, nct):
    b, t, _ = x_all.shape
    hi = SSM_D + SSM_XBC + SSM_HEADS
    w_aug = jnp.concatenate([w_in[:, :hi], jnp.zeros((D, 128 - SSM_HEADS), F32), w_in[:, hi:]], axis=1).astype(BF16)
    tok = lambda bb, i: (bb, i, 0)
    modspec = pl.BlockSpec((1, 1, 6, D), lambda bb, i: (bb, jnp.where(i < nct, 1, 0), 0, 0))
    return pl.pallas_call(
        _ssm_in_kernel,
        grid=(b, t // TM),
        in_specs=[pl.BlockSpec((1, TM, D), tok),
                  pl.BlockSpec((TOP_K, 2, 1, TM, SC_ROW_WORDS), lambda bb, i: (0, 0, bb, i, 0)),
                  pl.BlockSpec((1, TM, ROUTE_LANES), tok), modspec, modspec,
                  _const_spec((1, D)), _const_spec((D, SSM_IN_AUG))],
        out_specs=[pl.BlockSpec((1, TM, D), tok), pl.BlockSpec((1, TM, 1024), tok), pl.BlockSpec((1, TM, SSM_XBC), tok),
                   pl.BlockSpec((1, TM, 128), tok), pl.BlockSpec((1, TM, 1024), tok)],
        out_shape=[jax.ShapeDtypeStruct((b, t, D), F32), jax.ShapeDtypeStruct((b, t, 1024), F32),
                   jax.ShapeDtypeStruct((b, t, SSM_XBC), F32), jax.ShapeDtypeStruct((b, t, 128), F32),
                   jax.ShapeDtypeStruct((b, t, 1024), F32)],
        compiler_params=_cparams(("arbitrary", "arbitrary")),
        name="ssm_in",
    )(x_all, g_all, slab, mods_prev, mods, ng.reshape(1, D), w_aug)


def _conv_kernel(x_ref, w_ref, b_ref, o_ref, pad_ref, *, taps, ctx_len, silu):
    t = x_ref.shape[1]
    ct = x_ref.shape[2]
    half = taps // 2
    zeros = jnp.zeros((HALO, ct), F32)
    pad_ref[0:HALO, :] = zeros
    pad_ref[HALO:HALO + ctx_len, :] = x_ref[0, 0:ctx_len, :]
    pad_ref[HALO + ctx_len:2 * HALO + ctx_len, :] = zeros
    pad_ref[2 * HALO + ctx_len:2 * HALO + t, :] = x_ref[0, ctx_len:t, :]
    pad_ref[2 * HALO + t:3 * HALO + t, :] = zeros
    w = w_ref[...]
    bias = b_ref[...]

    def segment(out_start, length, pad_start):
        def body(c, carry):
            base = pl.multiple_of(c * CHUNK, CHUNK)
            win = pad_ref[pl.ds(pad_start - HALO + base, CHUNK + 2 * HALO), :]
            acc = jnp.broadcast_to(bias, (CHUNK, ct))
            for k in range(taps):
                off = HALO + k - half
                acc = acc + w[k:k + 1, :] * win[off:off + CHUNK, :]
            if silu:
                acc = acc * _sigmoid(acc)
            o_ref[0, pl.ds(out_start + base, CHUNK), :] = acc
            return carry
        lax.fori_loop(0, length // CHUNK, body, 0)

    segment(0, ctx_len, HALO)
    segment(ctx_len, t - ctx_len, 2 * HALO + ctx_len)


def _depthwise_conv(x, w, bias, ctx_len, silu):
    b, t, c = x.shape
    taps = w.shape[0]
    ct = 256
    return pl.pallas_call(
        functools.partial(_conv_kernel, taps=taps, ctx_len=ctx_len, silu=silu),
        grid=(b, c // ct),
        in_specs=[pl.BlockSpec((1, t, ct), lambda bb, j: (bb, 0, j)),
                  pl.BlockSpec((taps, ct), lambda bb, j: (0, j)),
                  pl.BlockSpec((1, ct), lambda bb, j: (0, j))],
        out_specs=pl.BlockSpec((1, t, ct), lambda bb, j: (bb, 0, j)),
        out_shape=jax.ShapeDtypeStruct((b, t, c), F32),
        scratch_shapes=[pltpu.VMEM((t + 3 * HALO, ct), F32)],
        compiler_params=_cparams(("arbitrary", "arbitrary")),
        name=f"depthwise_conv{taps}",
    )(x, w, bias.reshape(1, c))


def _expand_heads(v, lane_lo):
    rows = v.shape[0]
    blocks = []
    for p in range(SSM_HEADS // 2):
        a = jnp.broadcast_to(v[:, 2 * p:2 * p + 1], (rows, LANES))
        b = jnp.broadcast_to(v[:, 2 * p + 1:2 * p + 2], (rows, LANES))
        blocks.append(jnp.where(lane_lo, a, b))
    return jnp.concatenate(blocks, axis=1)


def _ssd_chunk(xbc, dt_raw, a_row, bias_row, state_ref, d, reverse):
    L = CHUNK
    ri = lax.broadcasted_iota(jnp.int32, (L, L), 0)
    ci = lax.broadcasted_iota(jnp.int32, (L, L), 1)
    lane_lo = lax.broadcasted_iota(jnp.int32, (L, LANES), 1) < 64
    mask = (ci >= ri) if reverse else (ri >= ci)
    tri = mask.astype(F32)

    xdt_in = dt_raw + bias_row
    dt = jnp.maximum(xdt_in, 0.0) + jnp.log1p(jnp.exp(-jnp.abs(xdt_in)))
    da = dt * a_row
    cs = jnp.dot(tri, da, precision=HIGHEST, preferred_element_type=F32)
    cs_t = cs.T
    a_end = cs[0:1, :] if reverse else cs[L - 1:L, :]
    d_in = jnp.exp(cs)
    d_out = jnp.exp(a_end - cs)

    x = xbc[:, 0:SSM_D]
    xdt = x * _expand_heads(dt, lane_lo)
    xdd = (xdt * _expand_heads(d_out, lane_lo)).astype(BF16)
    din_x = _expand_heads(d_in, lane_lo)
    s_decay = din_x[0:1, :] if reverse else din_x[L - 1:L, :]
    xdt = xdt.astype(BF16)

    ys = []
    for g in range(2):
        bm = xbc[:, SSM_D + g * SSM_STATE:SSM_D + (g + 1) * SSM_STATE].astype(BF16)
        cm = xbc[:, SSM_D + 2 * SSM_STATE + g * SSM_STATE:SSM_D + 2 * SSM_STATE + (g + 1) * SSM_STATE].astype(BF16)
        cb = lax.dot_general(cm, bm, (((1,), (1,)), ((), ())), preferred_element_type=F32)
        gl = slice(g * 512, (g + 1) * 512)
        state = state_ref[d, g]
        y_off = jnp.dot(cm, state.astype(BF16), preferred_element_type=F32) * din_x[:, gl]
        upd = lax.dot_general(bm, xdd[:, gl], (((0,), (0,)), ((), ())), preferred_element_type=F32)
        state_ref[d, g] = state * s_decay[:, gl] + upd
        for p in range(4):
            hp = g * 4 + p
            outs = []
            for j in range(2):
                hh = 2 * hp + j
                col = jnp.broadcast_to(cs[:, hh:hh + 1], (L, L))
                row = jnp.broadcast_to(cs_t[hh:hh + 1, :], (L, L))
                dec = jnp.exp(jnp.where(mask, col - row, -jnp.inf))
                outs.append(jnp.dot((cb * dec).astype(BF16), xdt[:, hp * 128:(hp + 1) * 128], preferred_element_type=F32))
            ys.append(jnp.where(lane_lo, outs[0], outs[1]) + y_off[:, p * 128:(p + 1) * 128])
    return jnp.concatenate(ys, axis=1)


def _ssd_kernel(xf_ref, xb_ref, dtf_ref, dtb_ref, alog_ref, bias_ref, yf_ref, yb_ref, state_ref):
    @pl.when(pl.program_id(1) == 0)
    def _():
        state_ref[...] = jnp.zeros_like(state_ref)

    a = -jnp.exp(alog_ref[...])
    bias = bias_ref[...]
    yf_ref[0] = _ssd_chunk(xf_ref[0], dtf_ref[0], a[0:1, :], bias[0:1, :], state_ref, 0, False)
    yb_ref[0] = _ssd_chunk(xb_ref[0], dtb_ref[0], a[1:2, :], bias[1:2, :], state_ref, 1, True)


def _ssd(xbc, dt, a_log, dt_bias, ctx_len):
    b, t, _ = xbc.shape
    nc = t // CHUNK
    ncc = ctx_len // CHUNK

    def fwd(bb, j):
        return (bb, j, 0)

    def bwd(bb, j):
        return (bb, jnp.where(j < ncc, ncc - 1 - j, nc - 1 + ncc - j), 0)

    pad = lambda v: jnp.pad(v, ((0, 0), (0, 128 - SSM_HEADS)))
    return pl.pallas_call(
        _ssd_kernel,
        grid=(b, nc),
        in_specs=[pl.BlockSpec((1, CHUNK, SSM_XBC), fwd), pl.BlockSpec((1, CHUNK, SSM_XBC), bwd),
                  pl.BlockSpec((1, CHUNK, 128), fwd), pl.BlockSpec((1, CHUNK, 128), bwd),
                  _const_spec((2, 128)), _const_spec((2, 128))],
        out_specs=[pl.BlockSpec((1, CHUNK, SSM_D), fwd), pl.BlockSpec((1, CHUNK, SSM_D), bwd)],
        out_shape=[jax.ShapeDtypeStruct((b, t, SSM_D), F32), jax.ShapeDtypeStruct((b, t, SSM_D), F32)],
        scratch_shapes=[pltpu.VMEM((2, 2, SSM_STATE, 512), F32)],
        compiler_params=_cparams(("arbitrary", "arbitrary")),
        name="ssd_scan",
    )(xbc, xbc, dt, dt, pad(a_log), pad(dt_bias))


def _post_ssm_kernel(x_ref, yf_ref, yb_ref, xs_ref, z_ref, v_ref, dsk_ref, sg_ref, lng_ref, lnb_ref, wout_ref,
                     mods_ref, g2_ref, wrh_ref, wrl_ref, br_ref, tri_ref, x1_ref, hp_ref, slab_ref, cnt_ref, carry_ref,
                     *, nct):
    y = yf_ref[0] + yb_ref[0] + dsk_ref[...] * xs_ref[0]
    z = z_ref[0]
    y_ssm = _rms(y * (z * _sigmoid(z)), sg_ref[...])
    v = v_ref[0]
    mu = jnp.mean(v, axis=-1, keepdims=True)
    vc = v - mu
    ln = vc * lax.rsqrt(jnp.mean(vc * vc, axis=-1, keepdims=True) + EPS) * lng_ref[...] + lnb_ref[...]
    y_conv = ln * _sigmoid(ln)
    o = (jnp.dot(y_ssm.astype(BF16), wout_ref[0:1024, :], preferred_element_type=F32)
         + jnp.dot(y_conv.astype(BF16), wout_ref[1024:2048, :], preferred_element_type=F32))
    _residual_norm_router(x_ref[0], o, mods_ref, g2_ref, wrh_ref, wrl_ref, br_ref, tri_ref, x1_ref, hp_ref, slab_ref,
                          cnt_ref, carry_ref, nct=nct, route_ctx=False)


def _post_ssm(x_all, yf, yb, xbc_act, z, v, d_skip, ssm_norm_g, ln_g, ln_b, w_out, mods, g2, w_router, b_router, nct):
    b, t, _ = x_all.shape
    tok = lambda bb, i: (bb, i, 0)
    dsk = jnp.repeat(d_skip[0] + d_skip[1], SSM_HEAD_DIM).reshape(1, SSM_D)
    out_specs, out_shape = _post_out_specs(b, t)
    row = lambda: pl.BlockSpec((1, TM, 1024), tok)
    return pl.pallas_call(
        functools.partial(_post_ssm_kernel, nct=nct),
        grid=(b, t // TM),
        in_specs=[row(), row(), row(), row(), row(), row(),
                  _const_spec((1, SSM_D)), _const_spec((1, SSM_D)), _const_spec((1, D)), _const_spec((1, D)),
                  _const_spec((2048, D)),
                  pl.BlockSpec((1, 1, 6, D), lambda bb, i: (bb, jnp.where(i < nct, 1, 0), 0, 0)),
                  _const_spec((1, D))] + _router_specs(),
        out_specs=out_specs, out_shape=out_shape,
        scratch_shapes=[pltpu.VMEM((1, N_EXPERTS), F32)],
        compiler_params=_cparams(("arbitrary", "arbitrary")),
        name="post_ssm",
    )(x_all, yf, yb, xbc_act, z, v, dsk, ssm_norm_g.reshape(1, -1), ln_g.reshape(1, -1), ln_b.reshape(1, -1),
      w_out.astype(BF16), mods, g2.reshape(1, D), *_router_operands(w_router, b_router))


def _final_kernel(x_ref, g_ref, slab_ref, mods_ref, fg_ref, o_ref):
    x = x_ref[0] + mods_ref[0, 0, 5:6, :] * _combine(g_ref, slab_ref[0])
    o_ref[0] = _rms(x, fg_ref[...])


def _final(x_all, g_lat, slab, mods, final_g, nct):
    b, s = g_lat.shape[2], g_lat.shape[3]
    return pl.pallas_call(
        _final_kernel,
        grid=(b, s // TM),
        in_specs=[pl.BlockSpec((1, TM, D), lambda bb, i: (bb, i + nct, 0)),
                  pl.BlockSpec((TOP_K, 2, 1, TM, SC_ROW_WORDS), lambda bb, i: (0, 0, bb, i, 0)),
                  pl.BlockSpec((1, TM, ROUTE_LANES), lambda bb, i: (bb, i + nct, 0)),
                  pl.BlockSpec((1, 1, 6, D), lambda bb, i: (bb, 0, 0, 0)),
                  _const_spec((1, D))],
        out_specs=pl.BlockSpec((1, TM, D), lambda bb, i: (bb, i, 0)),
        out_shape=jax.ShapeDtypeStruct((b, s, D), F32),
        compiler_params=_cparams(("arbitrary", "arbitrary")),
        name="final_norm",
    )(x_all, g_lat, slab, mods, final_g.reshape(1, D))


def kernel(x, c, ctx, c_ctx, w_mod, b_mod, norm_g, attn_w_in, mla_g_cq, mla_w_uq, mla_g_ckv, mla_w_ukv, gqa_g_q, gqa_g_k, attn_w_out, ssm_w_in, ssm_conv_w, ssm_conv_b, ssm_a_log, ssm_dt_bias, ssm_d, ssm_norm_g, conf_dw_w, conf_dw_b, conf_ln_g, conf_ln_b, ssm_w_out, moe_w_router, moe_b_router, moe_w_gate_up, moe_b_gate_up, moe_w_down, moe_b_down, final_g):
    b, s, _ = x.shape
    ctx_len = ctx.shape[1]
    t = ctx_len + s
    assert ctx_len % TM == 0 and s % TM == 0 and s % GRID_W == 0
    nct = ctx_len // TM

    c_rows = jnp.concatenate([c, c_ctx[None, :], jnp.zeros((-(b + 1) % 8, D), F32)], axis=0)
    mod_all = _modulations(c_rows, w_mod, b_mod)
    mods = []
    for i in range(w_mod.shape[0]):
        lat = mod_all[i, :b].reshape(b, 1, 6, D)
        cm = jnp.broadcast_to(mod_all[i, b].reshape(1, 1, 6, D), (b, 1, 6, D))
        mods.append(jnp.concatenate([lat, cm], axis=1))

    x_all = jnp.concatenate([ctx, x], axis=1)

    tab = _rope_tables(s, ctx_len)
    qm, km, vm, qg, kg, vgs = _attn_in(x_all, mods[0], norm_g[0, 0], attn_w_in[0], mla_g_cq[0], mla_w_uq[0], mla_g_ckv[0],
                                       mla_w_ukv[0], gqa_g_q[0], gqa_g_k[0], tab, nct)
    o_m = _mla_attention(qm, km, vm, nct, ctx_len)
    o_g = _gqa_attention(qg, kg, vgs, nct, ctx_len)
    x_all, hp, slab0, counts = _post_attn(x_all, o_m, o_g, attn_w_out[0], mods[0], norm_g[0, 1], moe_w_router[0],
                                          moe_b_router[0], nct)
    g_all = _moe(hp, slab0.reshape(b * t, ROUTE_LANES), counts, 0, moe_w_gate_up, moe_b_gate_up, moe_w_down, moe_b_down, t, 0)

    x_all, z, xbc, dt, v = _ssm_in(x_all, g_all, slab0, mods[0], mods[1], norm_g[1, 0], ssm_w_in[0], nct)
    xbc_act = _depthwise_conv(xbc, ssm_conv_w[0], ssm_conv_b[0], ctx_len, True)
    v_conv = _depthwise_conv(v, conf_dw_w[0], conf_dw_b[0], ctx_len, False)
    yf, yb = _ssd(xbc_act, dt, ssm_a_log[0], ssm_dt_bias[0], ctx_len)
    x_all, hp, slab1, counts = _post_ssm(x_all, yf, yb, xbc_act, z, v_conv, ssm_d[0], ssm_norm_g[0], conf_ln_g[0],
                                         conf_ln_b[0], ssm_w_out[0], mods[1], norm_g[1, 1], moe_w_router[1],
                                         moe_b_router[1], nct)
    g_lat = _moe(hp, slab1[:, ctx_len:].reshape(b * s, ROUTE_LANES), counts, 1, moe_w_gate_up, moe_b_gate_up, moe_w_down,
                 moe_b_down, s, ctx_len)
    return _final(x_all, g_lat, slab1, mods[1], final_g, nct)
```

```python
import functools

import jax
import jax.numpy as jnp
from jax import lax
from jax.experimental import pallas as pl
from jax.experimental.pallas import tpu as pltpu
from jax.experimental.pallas import tpu_sc as plsc

F32 = jnp.float32
BF16 = jnp.bfloat16
HIGHEST = lax.Precision.HIGHEST

D = 1024
EPS = 1e-6
GRID_W = 64
ROPE_THETA = 10000.0
LOG2E = 1.4426950408889634

MLA_HEADS = 8
MLA_NOPE = 64
MLA_ROPE = 32
MLA_Q_LORA = 256
MLA_KV_LORA = 128
MLA_SCALE = (MLA_NOPE + MLA_ROPE) ** -0.5
GQA_HEADS = 8
GQA_HEAD_DIM = 64
GQA_SCALE = GQA_HEAD_DIM ** -0.5
ATTN_IN_AUG = 1280

SSM_HEADS = 16
SSM_HEAD_DIM = 64
SSM_STATE = 128
SSM_D = 1024
SSM_XBC = 1536
SSM_CONV = 5
CONF_K = 31
CHUNK = 128
SSM_IN_AUG = 1024 + 1536 + 128 + 2048

N_EXPERTS = 32
TOP_K = 4
MOE_FF = 1024
SWIGLU_LIMIT = 7.0
SWIGLU_ALPHA = 1.702

TM = 256
TMOE = 512
ROUTE_LANES = 128
SC_WINDOW = 128
SC_ROW_WORDS = 256
LANES = 128
HALO = 16
VMEM_LIMIT = 56 * 1024 * 1024


def _cparams(sem):
    return pltpu.CompilerParams(dimension_semantics=sem, vmem_limit_bytes=VMEM_LIMIT)


def _rms(x, g):
    return x * lax.rsqrt(jnp.mean(x * x, axis=-1, keepdims=True) + EPS) * g


def _sigmoid(x):
    return 1.0 / (1.0 + jnp.exp(-x))


def _const_spec(shape):
    n = len(shape)
    return pl.BlockSpec(shape, lambda *_: (0,) * n)


def _mod_kernel(c_ref, w_ref, b_ref, o_ref):
    c = c_ref[...]
    o_ref[0] = jnp.dot(c * _sigmoid(c), w_ref[0], precision=HIGHEST, preferred_element_type=F32) + b_ref[0]


def _modulations(c_rows, w_mod, b_mod):
    depth, _, n = w_mod.shape
    tn = 512
    r = c_rows.shape[0]
    return pl.pallas_call(
        _mod_kernel,
        grid=(depth, n // tn),
        in_specs=[pl.BlockSpec((r, D), lambda l, j: (0, 0)),
                  pl.BlockSpec((1, D, tn), lambda l, j: (l, 0, j)),
                  pl.BlockSpec((1, 1, tn), lambda l, j: (l, 0, j))],
        out_specs=pl.BlockSpec((1, r, tn), lambda l, j: (l, 0, j)),
        out_shape=jax.ShapeDtypeStruct((depth, r, n), F32),
        compiler_params=_cparams(("arbitrary", "arbitrary")),
        name="modulations",
    )(c_rows, w_mod, b_mod.reshape(depth, 1, n))


def _attn_in_kernel(x_ref, mods_ref, ng_ref, win_ref, gcq_ref, wuq_ref, gckv_ref, wkv_ref, gq_ref, gk_ref, bd_ref,
                    tab_ref, qm_ref, km_ref, vm_ref, qg_ref, kg_ref, vgs_ref):
    x = x_ref[0]
    sh = mods_ref[0, 0, 0:1, :]
    sc = mods_ref[0, 0, 1:2, :]
    h = _rms(x, ng_ref[...]) * (1.0 + sc) + sh
    u = jnp.dot(h.astype(BF16), win_ref[...], preferred_element_type=F32)

    tab = tab_ref[...]
    cg, sg = tab[:, 0:128], tab[:, 128:256]
    cm, sm = tab[:, 256:384], tab[:, 384:512]
    lane = lax.broadcasted_iota(jnp.int32, (x.shape[0], LANES), 1)
    even = (lane & 1) == 0
    lo = lane < 64

    def rope(v, c, s):
        partner = jnp.where(even, pltpu.roll(v, LANES - 1, 1), pltpu.roll(v, 1, 1))
        return v * c + partner * s

    cq = _rms(u[:, 0:256], gcq_ref[...])
    qm = jnp.dot(cq.astype(BF16), wuq_ref[...], preferred_element_type=F32)
    for hd in range(MLA_HEADS):
        qm_ref[0, hd] = rope(qm[:, hd * 128:(hd + 1) * 128], cm, sm).astype(BF16)
    ckv = _rms(u[:, 256:384], gckv_ref[...])
    kv = jnp.dot(ckv.astype(BF16), wkv_ref[...], preferred_element_type=F32)
    kr = rope(u[:, 384:512], cm, sm)
    for hd in range(MLA_HEADS):
        km_ref[0, hd] = (kv[:, hd * 128:(hd + 1) * 128] + kr).astype(BF16)
    for p in range(MLA_HEADS // 2):
        vm_ref[0, p] = kv[:, 1024 + p * 128:1024 + (p + 1) * 128].astype(BF16)

    qg = u[:, 512:1024]
    ms = jnp.dot((qg * qg).astype(BF16), bd_ref[...], preferred_element_type=F32)
    qg = qg * lax.rsqrt(ms + EPS) * gq_ref[...]
    zero = jnp.zeros((x.shape[0], LANES), F32)
    for p in range(GQA_HEADS // 2):
        blk = rope(qg[:, p * 128:(p + 1) * 128], cg, sg)
        swp = pltpu.roll(blk, 64, 1)
        if p < 2:
            qg_ref[0, 2 * p] = jnp.where(lo, blk, zero).astype(BF16)
            qg_ref[0, 2 * p + 1] = jnp.where(lo, swp, zero).astype(BF16)
        else:
            qg_ref[0, 2 * p] = jnp.where(lo, zero, swp).astype(BF16)
            qg_ref[0, 2 * p + 1] = jnp.where(lo, zero, blk).astype(BF16)
    kg = u[:, 1024:1152]
    msk = jnp.dot((kg * kg).astype(BF16), bd_ref[0:128, 0:128], preferred_element_type=F32)
    kg_ref[0] = rope(kg * lax.rsqrt(msk + EPS) * gk_ref[...], cg, sg).astype(BF16)
    vg = u[:, 1152:1280]
    vgs_ref[0, 0] = vg.astype(BF16)
    vgs_ref[0, 1] = pltpu.roll(vg, 64, 1).astype(BF16)


def _rope_tables(seq, ctx_len):
    rows = seq // GRID_W
    row = jnp.broadcast_to(jnp.arange(rows, dtype=F32)[:, None], (rows, GRID_W)).reshape(-1)
    col = jnp.broadcast_to(jnp.arange(GRID_W, dtype=F32)[None, :], (rows, GRID_W)).reshape(-1)

    def interleaved(rot_dim):
        n_freq = rot_dim // 4
        inv_freq = ROPE_THETA ** (-jnp.arange(n_freq, dtype=F32) / n_freq)
        ang = jnp.concatenate([row[:, None] * inv_freq, col[:, None] * inv_freq], axis=-1)
        cos = jnp.repeat(jnp.cos(ang), 2, axis=-1)
        sin = jnp.repeat(jnp.sin(ang), 2, axis=-1) * jnp.tile(jnp.array([-1.0, 1.0], F32), rot_dim // 2)
        return cos, sin

    cg, sg = interleaved(GQA_HEAD_DIM)
    cg, sg = jnp.tile(cg, (1, 2)), jnp.tile(sg, (1, 2))
    cm32, sm32 = interleaved(MLA_ROPE)
    ones, zeros = jnp.ones((seq, 64), F32), jnp.zeros((seq, 64), F32)
    cm = jnp.concatenate([ones, cm32, ones[:, :32]], axis=-1)
    sm = jnp.concatenate([zeros, sm32, zeros[:, :32]], axis=-1)
    lat = jnp.concatenate([cg, sg, cm, sm], axis=-1)
    ident = jnp.concatenate([jnp.ones((ctx_len, 128), F32), jnp.zeros((ctx_len, 128), F32)] * 2, axis=-1)
    return jnp.concatenate([ident, lat], axis=0)


def _attn_in(x_all, mods, ng, w_in, g_cq, w_uq, g_ckv, w_ukv, g_q, g_k, tab, nct):
    b, t, _ = x_all.shape
    o1, o2, o3 = MLA_Q_LORA, MLA_Q_LORA + MLA_KV_LORA, MLA_Q_LORA + MLA_KV_LORA + MLA_ROPE
    zc = lambda n: jnp.zeros((D, n), F32)
    w_aug = jnp.concatenate([w_in[:, :o2], zc(64), w_in[:, o2:o3], zc(32), w_in[:, o3:]], axis=1).astype(BF16)
    wuq = jnp.pad(w_uq.reshape(MLA_Q_LORA, MLA_HEADS, 96), ((0, 0), (0, 0), (0, 32))).reshape(MLA_Q_LORA, 1024).astype(BF16)
    wukv = w_ukv.reshape(MLA_KV_LORA, MLA_HEADS, 128)
    wk = jnp.pad(wukv[:, :, :64], ((0, 0), (0, 0), (0, 64))).reshape(MLA_KV_LORA, 1024)
    wkv = jnp.concatenate([wk, wukv[:, :, 64:].reshape(MLA_KV_LORA, 512)], axis=1).astype(BF16)
    gcq = (g_cq * (MLA_SCALE * LOG2E)).reshape(1, -1)
    gq = (jnp.tile(g_q, GQA_HEADS) * (GQA_SCALE * LOG2E)).reshape(1, -1)
    gk = jnp.tile(g_k, 2).reshape(1, -1)
    bd = jnp.kron(jnp.eye(GQA_HEADS, dtype=F32), jnp.full((64, 64), 1.0 / 64, F32)).astype(BF16)
    nt = t // TM
    tok = lambda bb, i: (bb, i, 0)
    hm = lambda bb, i: (bb, 0, i, 0)
    return pl.pallas_call(
        _attn_in_kernel,
        grid=(b, nt),
        in_specs=[pl.BlockSpec((1, TM, D), tok),
                  pl.BlockSpec((1, 1, 6, D), lambda bb, i: (bb, jnp.where(i < nct, 1, 0), 0, 0)),
                  _const_spec((1, D)), _const_spec((D, ATTN_IN_AUG)), _const_spec((1, MLA_Q_LORA)),
                  _const_spec((MLA_Q_LORA, 1024)), _const_spec((1, MLA_KV_LORA)), _const_spec((MLA_KV_LORA, 1536)),
                  _const_spec((1, 512)), _const_spec((1, 128)), _const_spec((512, 512)),
                  pl.BlockSpec((TM, 512), lambda bb, i: (i, 0))],
        out_specs=[pl.BlockSpec((1, 8, TM, 128), hm), pl.BlockSpec((1, 8, TM, 128), hm),
                   pl.BlockSpec((1, 4, TM, 128), hm), pl.BlockSpec((1, 8, TM, 128), hm),
                   pl.BlockSpec((1, TM, 128), tok), pl.BlockSpec((1, 2, TM, 128), hm)],
        out_shape=[jax.ShapeDtypeStruct((b, 8, t, 128), BF16), jax.ShapeDtypeStruct((b, 8, t, 128), BF16),
                   jax.ShapeDtypeStruct((b, 4, t, 128), BF16), jax.ShapeDtypeStruct((b, 8, t, 128), BF16),
                   jax.ShapeDtypeStruct((b, t, 128), BF16), jax.ShapeDtypeStruct((b, 2, t, 128), BF16)],
        compiler_params=_cparams(("arbitrary", "arbitrary")),
        name="attn_in",
    )(x_all, mods, ng.reshape(1, D), w_aug, gcq, wuq, g_ckv.reshape(1, -1), wkv, gq, gk, bd, tab)


def _softmax_pv(q, k, v):
    s = lax.dot_general(q, k, (((1,), (1,)), ((), ())), preferred_element_type=F32)
    m = jnp.max(s, axis=-1, keepdims=True)
    p = jnp.exp2(s - m)
    l = jnp.sum(p, axis=-1, keepdims=True)
    return jnp.dot(p.astype(BF16), v, preferred_element_type=F32) / l


def _mla_kernel(q_ref, k_ref, v_ref, o_ref, *, nct, ctx_len):
    i = pl.program_id(1)
    lo = lax.broadcasted_iota(jnp.int32, (q_ref.shape[2], LANES), 1) < 64

    def run(nk):
        for pr in range(MLA_HEADS // 2):
            v = v_ref[0, pr, 0:nk, :]
            oa = _softmax_pv(q_ref[0, 2 * pr], k_ref[0, 2 * pr, 0:nk, :], v)
            ob = _softmax_pv(q_ref[0, 2 * pr + 1], k_ref[0, 2 * pr + 1, 0:nk, :], v)
            o_ref[0, :, pr * 128:(pr + 1) * 128] = jnp.where(lo, oa, ob).astype(BF16)

    @pl.when(i < nct)
    def _():
        run(ctx_len)

    @pl.when(i >= nct)
    def _():
        run(k_ref.shape[2])


def _mla_attention(qm, km, vm, nct, ctx_len):
    b, _, t, _ = qm.shape
    nt = t // TM
    return pl.pallas_call(
        functools.partial(_mla_kernel, nct=nct, ctx_len=ctx_len),
        grid=(b, nt),
        in_specs=[pl.BlockSpec((1, 8, TM, 128), lambda bb, i: (bb, 0, i, 0)),
                  pl.BlockSpec((1, 8, t, 128), lambda bb, i: (bb, 0, 0, 0)),
                  pl.BlockSpec((1, 4, t, 128), lambda bb, i: (bb, 0, 0, 0))],
        out_specs=pl.BlockSpec((1, TM, 512), lambda bb, i: (bb, i, 0)),
        out_shape=jax.ShapeDtypeStruct((b, t, 512), BF16),
        compiler_params=_cparams(("arbitrary", "arbitrary")),
        name="mla_attention",
    )(qm, km, vm)


def _gqa_kernel(q_ref, k_ref, v_ref, o_ref, *, nct, ctx_len):
    i = pl.program_id(1)
    lo = lax.broadcasted_iota(jnp.int32, (q_ref.shape[2], LANES), 1) < 64

    def run(nk):
        k = k_ref[0, 0:nk, :]
        for g in range(2):
            ve = v_ref[0, g, 0:nk, :]
            vo = v_ref[0, 1 - g, 0:nk, :]
            for pr in range(2):
                hd = 4 * g + 2 * pr
                oa = _softmax_pv(q_ref[0, hd], k, ve)
                ob = _softmax_pv(q_ref[0, hd + 1], k, vo)
                o_ref[0, :, (hd // 2) * 128:(hd // 2 + 1) * 128] = jnp.where(lo, oa, ob).astype(BF16)

    @pl.when(i < nct)
    def _():
        run(ctx_len)

    @pl.when(i >= nct)
    def _():
        run(k_ref.shape[1])


def _gqa_attention(qg, kg, vgs, nct, ctx_len):
    b, _, t, _ = qg.shape
    nt = t // TM
    return pl.pallas_call(
        functools.partial(_gqa_kernel, nct=nct, ctx_len=ctx_len),
        grid=(b, nt),
        in_specs=[pl.BlockSpec((1, 8, TM, 128), lambda bb, i: (bb, 0, i, 0)),
                  pl.BlockSpec((1, t, 128), lambda bb, i: (bb, 0, 0)),
                  pl.BlockSpec((1, 2, t, 128), lambda bb, i: (bb, 0, 0, 0))],
        out_specs=pl.BlockSpec((1, TM, 512), lambda bb, i: (bb, i, 0)),
        out_shape=jax.ShapeDtypeStruct((b, t, 512), BF16),
        compiler_params=_cparams(("arbitrary", "arbitrary")),
        name="gqa_attention",
    )(qg, kg, vgs)


def _pack_bf16_pairs(h):
    half = h.shape[1] // 2
    lo = pltpu.bitcast(h[:, :half].astype(BF16).astype(F32), jnp.uint32) >> 16
    hi = pltpu.bitcast(h[:, half:].astype(BF16).astype(F32), jnp.uint32) & jnp.uint32(0xFFFF0000)
    return lo | hi


def _unpack_bf16_pairs(w):
    return pltpu.bitcast(w << 16, F32), pltpu.bitcast(w & jnp.uint32(0xFFFF0000), F32)


def _store_planes(ref, words):
    ref[0, 0] = words[:, 0:SC_ROW_WORDS]
    ref[1, 0] = words[:, SC_ROW_WORDS:2 * SC_ROW_WORDS]


def _route_tile(lt, rt_ref, carry_ref, counted):
    n_e, rows = lt.shape
    eidx = lax.broadcasted_iota(jnp.int32, (n_e, rows), 0).astype(F32)
    work = lt
    vals, ids, hots = [], [], []
    for _ in range(TOP_K):
        m = jnp.max(work, axis=0, keepdims=True)
        idx = jnp.min(jnp.where(work == m, eidx, float(N_EXPERTS)), axis=0, keepdims=True)
        hot = eidx == idx
        vals.append(m)
        ids.append(idx)
        hots.append(hot)
        work = jnp.where(hot, -jnp.inf, work)
    exps = [jnp.exp(v - vals[0]) for v in vals]
    den = exps[0] + exps[1] + exps[2] + exps[3]
    mask = jnp.where(hots[0] | hots[1] | hots[2] | hots[3], 1.0, 0.0).astype(BF16)
    cum = jnp.dot(mask, rt_ref[...], preferred_element_type=F32)
    carry = carry_ref[...]
    before = cum[:, 0:rows] + jnp.concatenate([carry] * (rows // LANES), axis=1)
    ranks = [jnp.sum(jnp.where(h, before, 0.0), axis=0, keepdims=True) for h in hots]
    carry_ref[...] = carry + counted * cum[:, rows:rows + LANES]
    slab_t = jnp.concatenate(ids + ranks + [e / den for e in exps] + [jnp.zeros((ROUTE_LANES - 3 * TOP_K, rows), F32)],
                             axis=0)
    return slab_t.T


def _residual_norm_router(x, o, mods_ref, g2_ref, wrh_ref, wrl_ref, br_ref, rt_ref, x1_ref, hp_ref, slab_ref,
                          cnt_ref, carry_ref, *, nct, route_ctx):
    first = (pl.program_id(0) == 0) & (pl.program_id(1) == 0)

    @pl.when(first)
    def _():
        carry_ref[...] = jnp.zeros_like(carry_ref)

    g1 = mods_ref[0, 0, 2:3, :]
    sh2 = mods_ref[0, 0, 3:4, :]
    sc2 = mods_ref[0, 0, 4:5, :]
    x1 = x + g1 * o
    h2 = _rms(x1, g2_ref[...]) * (1.0 + sc2) + sh2
    x1_ref[0] = x1
    _store_planes(hp_ref, _pack_bf16_pairs(h2))
    hh = h2.astype(BF16)
    hl = (h2 - hh.astype(F32)).astype(BF16)
    logits = (jnp.dot(hh, wrh_ref[...], preferred_element_type=F32) + jnp.dot(hl, wrh_ref[...], preferred_element_type=F32)
              + jnp.dot(hh, wrl_ref[...], preferred_element_type=F32) + br_ref[...])
    counted = 1.0 if route_ctx else jnp.where(pl.program_id(1) >= nct, 1.0, 0.0)
    slab_ref[0] = _route_tile(logits.T[0:N_EXPERTS, :], rt_ref, carry_ref, counted)
    cnt_ref[...] = carry_ref[...]


def _post_attn_kernel(x_ref, om_ref, og_ref, wout_ref, mods_ref, g2_ref, wrh_ref, wrl_ref, br_ref, rt_ref,
                      x1_ref, hp_ref, slab_ref, cnt_ref, carry_ref, *, nct):
    o = (jnp.dot(om_ref[0], wout_ref[0:512, :], preferred_element_type=F32)
         + jnp.dot(og_ref[0], wout_ref[512:1024, :], preferred_element_type=F32))
    _residual_norm_router(x_ref[0], o, mods_ref, g2_ref, wrh_ref, wrl_ref, br_ref, rt_ref, x1_ref, hp_ref, slab_ref,
                          cnt_ref, carry_ref, nct=nct, route_ctx=True)


def _post_out_specs(b, t):
    tok = lambda bb, i: (bb, i, 0)
    specs = [pl.BlockSpec((1, TM, D), tok), pl.BlockSpec((2, 1, TM, SC_ROW_WORDS), lambda bb, i: (0, bb, i, 0)),
             pl.BlockSpec((1, TM, ROUTE_LANES), tok), _const_spec((N_EXPERTS, LANES))]
    shapes = [jax.ShapeDtypeStruct((b, t, D), F32), jax.ShapeDtypeStruct((2, b, t, SC_ROW_WORDS), jnp.uint32),
              jax.ShapeDtypeStruct((b, t, ROUTE_LANES), F32), jax.ShapeDtypeStruct((N_EXPERTS, LANES), F32)]
    return specs, shapes


def _router_operands(w_router, b_router):
    wp = jnp.pad(w_router, ((0, 0), (0, LANES - N_EXPERTS)))
    wrh = wp.astype(BF16)
    wrl = (wp - wrh.astype(F32)).astype(BF16)
    rt = jnp.concatenate([jnp.triu(jnp.ones((TM, TM), F32), 1), jnp.ones((TM, LANES), F32)], axis=1).astype(BF16)
    return wrh, wrl, jnp.pad(b_router, (0, LANES - N_EXPERTS)).reshape(1, LANES), rt


def _router_specs():
    return [_const_spec((D, LANES)), _const_spec((D, LANES)), _const_spec((1, LANES)), _const_spec((TM, TM + LANES))]


def _post_attn(x_all, o_m, o_g, w_out, mods, g2, w_router, b_router, nct):
    b, t, _ = x_all.shape
    tok = lambda bb, i: (bb, i, 0)
    out_specs, out_shape = _post_out_specs(b, t)
    return pl.pallas_call(
        functools.partial(_post_attn_kernel, nct=nct),
        grid=(b, t // TM),
        in_specs=[pl.BlockSpec((1, TM, D), tok), pl.BlockSpec((1, TM, 512), tok), pl.BlockSpec((1, TM, 512), tok),
                  _const_spec((1024, D)),
                  pl.BlockSpec((1, 1, 6, D), lambda bb, i: (bb, jnp.where(i < nct, 1, 0), 0, 0)),
                  _const_spec((1, D))] + _router_specs(),
        out_specs=out_specs, out_shape=out_shape,
        scratch_shapes=[pltpu.VMEM((N_EXPERTS, LANES), F32)],
        compiler_params=_cparams(("arbitrary", "arbitrary")),
        name="post_attn",
    )(x_all, o_m, o_g, w_out.astype(BF16), mods, g2.reshape(1, D), *_router_operands(w_router, b_router))


def _route_tables(slab, counts, n_tiles):
    e = slab[:, 0:TOP_K].astype(jnp.int32)
    rank = slab[:, TOP_K:2 * TOP_K].astype(jnp.int32)
    counts = counts[:, 0].astype(jnp.int32)
    padded = (counts + TMOE - 1) // TMOE * TMOE
    pend = jnp.cumsum(padded)
    pstart = pend - padded
    experts = jnp.arange(N_EXPERTS, dtype=jnp.int32)
    pos = rank + jnp.sum(jnp.where(e[..., None] == experts, pstart, 0), axis=-1)
    tile_start = jnp.arange(n_tiles, dtype=jnp.int32) * TMOE
    tile_expert = jnp.minimum(jnp.sum(tile_start[:, None] >= pend[None, :], axis=-1), N_EXPERTS - 1).astype(jnp.int32)
    mine = tile_expert[:, None] == experts
    live = jnp.sum(jnp.where(mine, counts, 0), axis=-1) - (tile_start - jnp.sum(jnp.where(mine, pstart, 0), axis=-1))
    tile_valid = jnp.clip(live, 0, TMOE).astype(jnp.int32)
    return pos, tile_expert, tile_valid


def _sc_mesh():
    return plsc.VectorSubcoreMesh(core_axis_name="core", subcore_axis_name="subcore")


def _sc_scatter_rows(src, idx, n_out, nb_tok, nb_src, nb_plane, off):
    m = idx.shape[0]
    steps_per_k = m // SC_WINDOW // TOP_K

    def src_block(i):
        q = i % steps_per_k
        plane = q // (steps_per_k // 2)
        r = q % (steps_per_k // 2)
        return (plane * nb_plane + (r // nb_tok) * nb_src + off + r % nb_tok, 0)

    @pl.kernel(out_type=jax.ShapeDtypeStruct((n_out, SC_ROW_WORDS), src.dtype), mesh=_sc_mesh(), scratch_types=[])
    def k(src_hbm, i_hbm, o_hbm):
        def body(x_vmem, i_vmem):
            pltpu.sync_copy(x_vmem, o_hbm.at[i_vmem.at[0]])

        pltpu.emit_pipeline(
            body,
            grid=(m // SC_WINDOW,),
            in_specs=[pl.BlockSpec((SC_WINDOW, SC_ROW_WORDS), src_block),
                      pl.BlockSpec((1, SC_WINDOW), lambda i: (0, i))],
            out_specs=[],
            core_axis_name=("core", "subcore"),
            dimension_semantics=(pltpu.PARALLEL,),
        )(src_hbm, i_hbm)

    return k(src, idx.reshape(1, m))


def _sc_gather_rows(src, idx):
    m = idx.shape[0]

    @pl.kernel(out_type=jax.ShapeDtypeStruct((m, SC_ROW_WORDS), src.dtype), mesh=_sc_mesh(), scratch_types=[])
    def k(src_hbm, i_hbm, o_hbm):
        def body(i_vmem, o_vmem):
            pltpu.sync_copy(src_hbm.at[i_vmem.at[0]], o_vmem)

        pltpu.emit_pipeline(
            body,
            grid=(m // SC_WINDOW,),
            in_specs=[pl.BlockSpec((1, SC_WINDOW), lambda i: (0, i))],
            out_specs=[pl.BlockSpec((SC_WINDOW, SC_ROW_WORDS), lambda i: (i, 0))],
            core_axis_name=("core", "subcore"),
            dimension_semantics=(pltpu.PARALLEL,),
        )(i_hbm, o_hbm)

    return k(src, idx.reshape(1, m))


def _moe_kernel(te_ref, tv_ref, xs_ref, wgu_ref, bgu_ref, wd_ref, bd_ref, ys_ref, wgu_bf, wd_bf):
    t = pl.program_id(0)
    prev = te_ref[jnp.maximum(t - 1, 0)]

    @pl.when((t == 0) | (te_ref[t] != prev))
    def _():
        wgu_bf[...] = wgu_ref[0, 0].astype(BF16)
        wd_bf[...] = wd_ref[0, 0].astype(BF16)

    valid = tv_ref[t]
    half = TMOE // 2

    def experts(rows):
        lo, hi = _unpack_bf16_pairs(jnp.concatenate([xs_ref[0, 0:rows, :], xs_ref[1, 0:rows, :]], axis=1))
        live = lax.broadcasted_iota(jnp.int32, (rows, 1), 0) < valid
        x = jnp.where(live, jnp.concatenate([lo, hi], axis=1), 0.0).astype(BF16)
        gu = jnp.dot(x, wgu_bf[...], preferred_element_type=F32) + bgu_ref[0, 0]
        gate = jnp.minimum(gu[:, :MOE_FF], SWIGLU_LIMIT)
        up = jnp.clip(gu[:, MOE_FF:], -SWIGLU_LIMIT, SWIGLU_LIMIT)
        act = (up + 1.0) * (gate * _sigmoid(SWIGLU_ALPHA * gate))
        y = jnp.dot(act.astype(BF16), wd_bf[...], preferred_element_type=F32) + bd_ref[0, 0]
        words = _pack_bf16_pairs(y)
        ys_ref[0, 0:rows, :] = words[:, 0:SC_ROW_WORDS]
        ys_ref[1, 0:rows, :] = words[:, SC_ROW_WORDS:2 * SC_ROW_WORDS]

    @pl.when(valid > half)
    def _():
        experts(TMOE)

    @pl.when((valid > 0) & (valid <= half))
    def _():
        experts(half)
        ys_ref[:, half:TMOE, :] = jnp.zeros((2, half, SC_ROW_WORDS), jnp.uint32)

    @pl.when(valid == 0)
    def _():
        ys_ref[...] = jnp.zeros_like(ys_ref)


def _moe_experts(xs, tile_expert, tile_valid, layer, w_gu, b_gu, w_d, b_d):
    n_rows = xs.shape[1]
    n_tiles = n_rows // TMOE
    depth = w_gu.shape[0]
    grid_spec = pltpu.PrefetchScalarGridSpec(
        num_scalar_prefetch=2,
        grid=(n_tiles,),
        in_specs=[pl.BlockSpec((2, TMOE, SC_ROW_WORDS), lambda t, te, tv: (0, t, 0)),
                  pl.BlockSpec((1, 1, D, 2 * MOE_FF), lambda t, te, tv: (layer, te[t], 0, 0)),
                  pl.BlockSpec((1, 1, 1, 2 * MOE_FF), lambda t, te, tv: (layer, te[t], 0, 0)),
                  pl.BlockSpec((1, 1, MOE_FF, D), lambda t, te, tv: (layer, te[t], 0, 0)),
                  pl.BlockSpec((1, 1, 1, D), lambda t, te, tv: (layer, te[t], 0, 0))],
        out_specs=pl.BlockSpec((2, TMOE, SC_ROW_WORDS), lambda t, te, tv: (0, t, 0)),
        scratch_shapes=[pltpu.VMEM((D, 2 * MOE_FF), BF16), pltpu.VMEM((MOE_FF, D), BF16)],
    )
    return pl.pallas_call(
        _moe_kernel,
        grid_spec=grid_spec,
        out_shape=jax.ShapeDtypeStruct((2, n_rows, SC_ROW_WORDS), jnp.uint32),
        compiler_params=_cparams(("arbitrary",)),
        name="moe_experts",
    )(tile_expert, tile_valid, xs, w_gu, b_gu.reshape(depth, N_EXPERTS, 1, -1), w_d, b_d.reshape(depth, N_EXPERTS, 1, -1))


def _moe(hp, slab, counts, layer, w_gu, b_gu, w_d, b_d, n_seq, row0):
    _, b, t, _ = hp.shape
    n_tok = b * n_seq
    n_tiles = -(-n_tok * TOP_K // TMOE) + N_EXPERTS
    n_rows = n_tiles * TMOE
    pos, tile_expert, tile_valid = _route_tables(slab, counts, n_tiles)
    idx = (pos.T[:, None, :] + (jnp.arange(2, dtype=jnp.int32) * n_rows)[None, :, None]).reshape(-1)
    xs = _sc_scatter_rows(hp.reshape(2 * b * t, SC_ROW_WORDS), idx, 2 * n_rows, n_seq // SC_WINDOW, t // SC_WINDOW,
                          b * t // SC_WINDOW, row0 // SC_WINDOW)
    ys = _moe_experts(xs.reshape(2, n_rows, SC_ROW_WORDS), tile_expert, tile_valid, layer, w_gu, b_gu, w_d, b_d)
    g = _sc_gather_rows(ys.reshape(2 * n_rows, SC_ROW_WORDS), idx)
    return g.reshape(TOP_K, 2, b, n_seq, SC_ROW_WORDS)


def _combine(g_ref, slab):
    acc_lo = acc_hi = None
    for k in range(TOP_K):
        lo, hi = _unpack_bf16_pairs(jnp.concatenate([g_ref[k, 0, 0], g_ref[k, 1, 0]], axis=1))
        w = slab[:, 2 * TOP_K + k:2 * TOP_K + k + 1]
        acc_lo = w * lo if acc_lo is None else acc_lo + w * lo
        acc_hi = w * hi if acc_hi is None else acc_hi + w * hi
    return jnp.concatenate([acc_lo, acc_hi], axis=1)


def _ssm_in_kernel(x_ref, g_ref, slab_ref, modsp_ref, mods_ref, ng_ref, win_ref, x1_ref, z_ref, xbc_ref, dt_ref, v_ref):
    x = x_ref[0] + modsp_ref[0, 0, 5:6, :] * _combine(g_ref, slab_ref[0])
    x1_ref[0] = x
    sh = mods_ref[0, 0, 0:1, :]
    sc = mods_ref[0, 0, 1:2, :]
    h = _rms(x, ng_ref[...]) * (1.0 + sc) + sh
    u = jnp.dot(h.astype(BF16), win_ref[...], preferred_element_type=F32)
    z_ref[0] = u[:, 0:1024]
    xbc_ref[0] = u[:, 1024:2560]
    dt_ref[0] = u[:, 2560:2688]
    v_ref[0] = u[:, 2688:3712] * _sigmoid(u[:, 3712:4736])


def _ssm_in(x_all, g_all, slab, mods_prev, mods, ng, w_in, nct):
    b, t, _ = x_all.shape
    hi = SSM_D + SSM_XBC + SSM_HEADS
    w_aug = jnp.concatenate([w_in[:, :hi], jnp.zeros((D, 128 - SSM_HEADS), F32), w_in[:, hi:]], axis=1).astype(BF16)
    tok = lambda bb, i: (bb, i, 0)
    modspec = pl.BlockSpec((1, 1, 6, D), lambda bb, i: (bb, jnp.where(i < nct, 1, 0), 0, 0))
    return pl.pallas_call(
        _ssm_in_kernel,
        grid=(b, t // TM),
        in_specs=[pl.BlockSpec((1, TM, D), tok),
                  pl.BlockSpec((TOP_K, 2, 1, TM, SC_ROW_WORDS), lambda bb, i: (0, 0, bb, i, 0)),
                  pl.BlockSpec((1, TM, ROUTE_LANES), tok), modspec, modspec,
                  _const_spec((1, D)), _const_spec((D, SSM_IN_AUG))],
        out_specs=[pl.BlockSpec((1, TM, D), tok), pl.BlockSpec((1, TM, 1024), tok), pl.BlockSpec((1, TM, SSM_XBC), tok),
                   pl.BlockSpec((1, TM, 128), tok), pl.BlockSpec((1, TM, 1024), tok)],
        out_shape=[jax.ShapeDtypeStruct((b, t, D), F32), jax.ShapeDtypeStruct((b, t, 1024), F32),
                   jax.ShapeDtypeStruct((b, t, SSM_XBC), F32), jax.ShapeDtypeStruct((b, t, 128), F32),
                   jax.ShapeDtypeStruct((b, t, 1024), F32)],
        compiler_params=_cparams(("arbitrary", "arbitrary")),
        name="ssm_in",
    )(x_all, g_all, slab, mods_prev, mods, ng.reshape(1, D), w_aug)


def _conv_kernel(x_ref, w_ref, b_ref, o_ref, pad_ref, *, taps, ctx_len, silu):
    t = x_ref.shape[1]
    ct = x_ref.shape[2]
    half = taps // 2
    zeros = jnp.zeros((HALO, ct), F32)
    pad_ref[0:HALO, :] = zeros
    pad_ref[HALO:HALO + ctx_len, :] = x_ref[0, 0:ctx_len, :]
    pad_ref[HALO + ctx_len:2 * HALO + ctx_len, :] = zeros
    pad_ref[2 * HALO + ctx_len:2 * HALO + t, :] = x_ref[0, ctx_len:t, :]
    pad_ref[2 * HALO + t:3 * HALO + t, :] = zeros
    w = w_ref[...]
    bias = b_ref[...]
    rows = CHUNK + 2 * HALO

    def segment(out_start, length, pad_start):
        def body(c, carry):
            base = pl.multiple_of(c * CHUNK, CHUNK)
            win = pad_ref[pl.ds(pad_start - HALO + base, rows), :]
            acc = jnp.broadcast_to(bias, (CHUNK, ct))
            for r in range(8):
                ks = [k for k in range(taps) if (HALO + k - half) % 8 == r]
                if not ks:
                    continue
                rolled = win if r == 0 else pltpu.roll(win, rows - r, 0)
                for k in ks:
                    off = HALO + k - half - r
                    acc = acc + w[k:k + 1, :] * rolled[off:off + CHUNK, :]
            if silu:
                acc = acc * _sigmoid(acc)
            o_ref[0, pl.ds(out_start + base, CHUNK), :] = acc
            return carry
        lax.fori_loop(0, length // CHUNK, body, 0)

    segment(0, ctx_len, HALO)
    segment(ctx_len, t - ctx_len, 2 * HALO + ctx_len)


def _depthwise_conv(x, w, bias, ctx_len, silu):
    b, t, c = x.shape
    taps = w.shape[0]
    ct = 256
    return pl.pallas_call(
        functools.partial(_conv_kernel, taps=taps, ctx_len=ctx_len, silu=silu),
        grid=(b, c // ct),
        in_specs=[pl.BlockSpec((1, t, ct), lambda bb, j: (bb, 0, j)),
                  pl.BlockSpec((taps, ct), lambda bb, j: (0, j)),
                  pl.BlockSpec((1, ct), lambda bb, j: (0, j))],
        out_specs=pl.BlockSpec((1, t, ct), lambda bb, j: (bb, 0, j)),
        out_shape=jax.ShapeDtypeStruct((b, t, c), F32),
        scratch_shapes=[pltpu.VMEM((t + 3 * HALO, ct), F32)],
        compiler_params=_cparams(("arbitrary", "arbitrary")),
        name=f"depthwise_conv{taps}",
    )(x, w, bias.reshape(1, c))


def _ssd_chunk(xbc, dt_raw, a_row, bias_row, state_ref, d, reverse):
    L = CHUNK
    ri = lax.broadcasted_iota(jnp.int32, (L, L), 0)
    ci = lax.broadcasted_iota(jnp.int32, (L, L), 1)
    lane_lo = lax.broadcasted_iota(jnp.int32, (L, LANES), 1) < 64
    mask = (ci >= ri) if reverse else (ri >= ci)
    tri = mask.astype(F32)

    xdt_in = dt_raw + bias_row
    dt = jnp.maximum(xdt_in, 0.0) + jnp.log1p(jnp.exp(-jnp.abs(xdt_in)))
    da = dt * a_row
    cs = jnp.dot(tri, da, precision=HIGHEST, preferred_element_type=F32)
    cs_t = cs.T
    end = 0 if reverse else L - 1
    a_end = cs[end:end + 1, :]

    def lanes(v, hh):
        return jnp.broadcast_to(v[:, hh:hh + 1], (v.shape[0], LANES))

    ys = []
    for g in range(2):
        bm = xbc[:, SSM_D + g * SSM_STATE:SSM_D + (g + 1) * SSM_STATE].astype(BF16)
        cm = xbc[:, SSM_D + 2 * SSM_STATE + g * SSM_STATE:SSM_D + 2 * SSM_STATE + (g + 1) * SSM_STATE].astype(BF16)
        cb = lax.dot_general(cm, bm, (((1,), (1,)), ((), ())), preferred_element_type=F32)
        state = state_ref[d, g]
        y_off = jnp.dot(cm, state.astype(BF16), preferred_element_type=F32)
        xdd_blocks, sdec_blocks = [], []
        for p in range(4):
            hp = g * 4 + p
            gmats, e_cols, d_outs = [], [], []
            for j in range(2):
                hh = 2 * hp + j
                col = lanes(cs, hh)
                row = jnp.broadcast_to(cs_t[hh:hh + 1, :], (L, L))
                dec = jnp.exp(jnp.where(mask, col - row, -jnp.inf))
                gmats.append((cb * dec).astype(BF16))
                e_cols.append(jnp.exp(col))
                d_outs.append(jnp.exp(lanes(a_end, hh) - col))
            din = jnp.where(lane_lo, e_cols[0], e_cols[1])
            dout = jnp.where(lane_lo, d_outs[0], d_outs[1])
            dtp = jnp.where(lane_lo, lanes(dt, 2 * hp), lanes(dt, 2 * hp + 1))
            xdt = xbc[:, hp * 128:(hp + 1) * 128] * dtp
            xdd_blocks.append((xdt * dout).astype(BF16))
            sdec_blocks.append(din[end:end + 1, :])
            xdt = xdt.astype(BF16)
            ya = jnp.dot(gmats[0], xdt, preferred_element_type=F32)
            yb = jnp.dot(gmats[1], xdt, preferred_element_type=F32)
            ys.append(jnp.where(lane_lo, ya, yb) + din * y_off[:, p * 128:(p + 1) * 128])
        upd = lax.dot_general(bm, jnp.concatenate(xdd_blocks, axis=1), (((0,), (0,)), ((), ())),
                              preferred_element_type=F32)
        state_ref[d, g] = state * jnp.concatenate(sdec_blocks, axis=1) + upd
    return jnp.concatenate(ys, axis=1)


def _ssd_kernel(xf_ref, xb_ref, dtf_ref, dtb_ref, alog_ref, bias_ref, yf_ref, yb_ref, state_ref):
    @pl.when(pl.program_id(1) == 0)
    def _():
        state_ref[...] = jnp.zeros_like(state_ref)

    a = -jnp.exp(alog_ref[...])
    bias = bias_ref[...]
    yf_ref[0] = _ssd_chunk(xf_ref[0], dtf_ref[0], a[0:1, :], bias[0:1, :], state_ref, 0, False)
    yb_ref[0] = _ssd_chunk(xb_ref[0], dtb_ref[0], a[1:2, :], bias[1:2, :], state_ref, 1, True)


def _ssd(xbc, dt, a_log, dt_bias, ctx_len):
    b, t, _ = xbc.shape
    nc = t // CHUNK
    ncc = ctx_len // CHUNK

    def fwd(bb, j):
        return (bb, j, 0)

    def bwd(bb, j):
        return (bb, jnp.where(j < ncc, ncc - 1 - j, nc - 1 + ncc - j), 0)

    pad = lambda v: jnp.pad(v, ((0, 0), (0, 128 - SSM_HEADS)))
    return pl.pallas_call(
        _ssd_kernel,
        grid=(b, nc),
        in_specs=[pl.BlockSpec((1, CHUNK, SSM_XBC), fwd), pl.BlockSpec((1, CHUNK, SSM_XBC), bwd),
                  pl.BlockSpec((1, CHUNK, 128), fwd), pl.BlockSpec((1, CHUNK, 128), bwd),
                  _const_spec((2, 128)), _const_spec((2, 128))],
        out_specs=[pl.BlockSpec((1, CHUNK, SSM_D), fwd), pl.BlockSpec((1, CHUNK, SSM_D), bwd)],
        out_shape=[jax.ShapeDtypeStruct((b, t, SSM_D), F32), jax.ShapeDtypeStruct((b, t, SSM_D), F32)],
        scratch_shapes=[pltpu.VMEM((2, 2, SSM_STATE, 512), F32)],
        compiler_params=_cparams(("arbitrary", "arbitrary")),
        name="ssd_scan",
    )(xbc, xbc, dt, dt, pad(a_log), pad(dt_bias))


def _post_ssm_kernel(x_ref, yf_ref, yb_ref, xs_ref, z_ref, v_ref, dsk_ref, sg_ref, lng_ref, lnb_ref, wout_ref,
                     mods_ref, g2_ref, wrh_ref, wrl_ref, br_ref, rt_ref, x1_ref, hp_ref, slab_ref, cnt_ref, carry_ref,
                     *, nct):
    y = yf_ref[0] + yb_ref[0] + dsk_ref[...] * xs_ref[0]
    z = z_ref[0]
    y_ssm = _rms(y * (z * _sigmoid(z)), sg_ref[...])
    v = v_ref[0]
    mu = jnp.mean(v, axis=-1, keepdims=True)
    vc = v - mu
    ln = vc * lax.rsqrt(jnp.mean(vc * vc, axis=-1, keepdims=True) + EPS) * lng_ref[...] + lnb_ref[...]
    y_conv = ln * _sigmoid(ln)
    o = (jnp.dot(y_ssm.astype(BF16), wout_ref[0:1024, :], preferred_element_type=F32)
         + jnp.dot(y_conv.astype(BF16), wout_ref[1024:2048, :], preferred_element_type=F32))
    _residual_norm_router(x_ref[0], o, mods_ref, g2_ref, wrh_ref, wrl_ref, br_ref, rt_ref, x1_ref, hp_ref, slab_ref,
                          cnt_ref, carry_ref, nct=nct, route_ctx=False)


def _post_ssm(x_all, yf, yb, xbc_act, z, v, d_skip, ssm_norm_g, ln_g, ln_b, w_out, mods, g2, w_router, b_router, nct):
    b, t, _ = x_all.shape
    tok = lambda bb, i: (bb, i, 0)
    dsk = jnp.repeat(d_skip[0] + d_skip[1], SSM_HEAD_DIM).reshape(1, SSM_D)
    out_specs, out_shape = _post_out_specs(b, t)
    row = lambda: pl.BlockSpec((1, TM, 1024), tok)
    return pl.pallas_call(
        functools.partial(_post_ssm_kernel, nct=nct),
        grid=(b, t // TM),
        in_specs=[row(), row(), row(), row(), row(), row(),
                  _const_spec((1, SSM_D)), _const_spec((1, SSM_D)), _const_spec((1, D)), _const_spec((1, D)),
                  _const_spec((2048, D)),
                  pl.BlockSpec((1, 1, 6, D), lambda bb, i: (bb, jnp.where(i < nct, 1, 0), 0, 0)),
                  _const_spec((1, D))] + _router_specs(),
        out_specs=out_specs, out_shape=out_shape,
        scratch_shapes=[pltpu.VMEM((N_EXPERTS, LANES), F32)],
        compiler_params=_cparams(("arbitrary", "arbitrary")),
        name="post_ssm",
    )(x_all, yf, yb, xbc_act, z, v, dsk, ssm_norm_g.reshape(1, -1), ln_g.reshape(1, -1), ln_b.reshape(1, -1),
      w_out.astype(BF16), mods, g2.reshape(1, D), *_router_operands(w_router, b_router))


def _final_kernel(x_ref, g_ref, slab_ref, mods_ref, fg_ref, o_ref):
    x = x_ref[0] + mods_ref[0, 0, 5:6, :] * _combine(g_ref, slab_ref[0])
    o_ref[0] = _rms(x, fg_ref[...])


def _final(x_all, g_lat, slab, mods, final_g, nct):
    b, s = g_lat.shape[2], g_lat.shape[3]
    return pl.pallas_call(
        _final_kernel,
        grid=(b, s // TM),
        in_specs=[pl.BlockSpec((1, TM, D), lambda bb, i: (bb, i + nct, 0)),
                  pl.BlockSpec((TOP_K, 2, 1, TM, SC_ROW_WORDS), lambda bb, i: (0, 0, bb, i, 0)),
                  pl.BlockSpec((1, TM, ROUTE_LANES), lambda bb, i: (bb, i + nct, 0)),
                  pl.BlockSpec((1, 1, 6, D), lambda bb, i: (bb, 0, 0, 0)),
                  _const_spec((1, D))],
        out_specs=pl.BlockSpec((1, TM, D), lambda bb, i: (bb, i, 0)),
        out_shape=jax.ShapeDtypeStruct((b, s, D), F32),
        compiler_params=_cparams(("arbitrary", "arbitrary")),
        name="final_norm",
    )(x_all, g_lat, slab, mods, final_g.reshape(1, D))


def kernel(x, c, ctx, c_ctx, w_mod, b_mod, norm_g, attn_w_in, mla_g_cq, mla_w_uq, mla_g_ckv, mla_w_ukv, gqa_g_q, gqa_g_k, attn_w_out, ssm_w_in, ssm_conv_w, ssm_conv_b, ssm_a_log, ssm_dt_bias, ssm_d, ssm_norm_g, conf_dw_w, conf_dw_b, conf_ln_g, conf_ln_b, ssm_w_out, moe_w_router, moe_b_router, moe_w_gate_up, moe_b_gate_up, moe_w_down, moe_b_down, final_g):
    b, s, _ = x.shape
    ctx_len = ctx.shape[1]
    t = ctx_len + s
    assert ctx_len % TM == 0 and s % TM == 0 and s % GRID_W == 0
    nct = ctx_len // TM

    c_rows = jnp.concatenate([c, c_ctx[None, :], jnp.zeros((-(b + 1) % 8, D), F32)], axis=0)
    mod_all = _modulations(c_rows, w_mod, b_mod)
    mods = []
    for i in range(w_mod.shape[0]):
        lat = mod_all[i, :b].reshape(b, 1, 6, D)
        cm = jnp.broadcast_to(mod_all[i, b].reshape(1, 1, 6, D), (b, 1, 6, D))
        mods.append(jnp.concatenate([lat, cm], axis=1))

    x_all = jnp.concatenate([ctx, x], axis=1)

    tab = _rope_tables(s, ctx_len)
    qm, km, vm, qg, kg, vgs = _attn_in(x_all, mods[0], norm_g[0, 0], attn_w_in[0], mla_g_cq[0], mla_w_uq[0], mla_g_ckv[0],
                                       mla_w_ukv[0], gqa_g_q[0], gqa_g_k[0], tab, nct)
    o_m = _mla_attention(qm, km, vm, nct, ctx_len)
    o_g = _gqa_attention(qg, kg, vgs, nct, ctx_len)
    x_all, hp, slab0, counts = _post_attn(x_all, o_m, o_g, attn_w_out[0], mods[0], norm_g[0, 1], moe_w_router[0],
                                          moe_b_router[0], nct)
    g_all = _moe(hp, slab0.reshape(b * t, ROUTE_LANES), counts, 0, moe_w_gate_up, moe_b_gate_up, moe_w_down, moe_b_down, t, 0)

    x_all, z, xbc, dt, v = _ssm_in(x_all, g_all, slab0, mods[0], mods[1], norm_g[1, 0], ssm_w_in[0], nct)
    xbc_act = _depthwise_conv(xbc, ssm_conv_w[0], ssm_conv_b[0], ctx_len, True)
    v_conv = _depthwise_conv(v, conf_dw_w[0], conf_dw_b[0], ctx_len, False)
    yf, yb = _ssd(xbc_act, dt, ssm_a_log[0], ssm_dt_bias[0], ctx_len)
    x_all, hp, slab1, counts = _post_ssm(x_all, yf, yb, xbc_act, z, v_conv, ssm_d[0], ssm_norm_g[0], conf_ln_g[0],
                                         conf_ln_b[0], ssm_w_out[0], mods[1], norm_g[1, 1], moe_w_router[1],
                                         moe_b_router[1], nct)
    g_lat = _moe(hp, slab1[:, ctx_len:].reshape(b * s, ROUTE_LANES), counts, 1, moe_w_gate_up, moe_b_gate_up, moe_w_down,
                 moe_b_down, s, ctx_len)
    return _final(x_all, g_lat, slab1, mods[1], final_g, nct)
```

```python
import functools

import jax
import jax.numpy as jnp
from jax import lax
from jax.experimental import pallas as pl
from jax.experimental.pallas import tpu as pltpu
from jax.experimental.pallas import tpu_sc as plsc

F32 = jnp.float32
BF16 = jnp.bfloat16
HIGHEST = lax.Precision.HIGHEST

D = 1024
EPS = 1e-6
GRID_W = 64
ROPE_THETA = 10000.0
LOG2E = 1.4426950408889634

MLA_HEADS = 8
MLA_NOPE = 64
MLA_ROPE = 32
MLA_Q_LORA = 256
MLA_KV_LORA = 128
MLA_SCALE = (MLA_NOPE + MLA_ROPE) ** -0.5
GQA_HEADS = 8
GQA_HEAD_DIM = 64
GQA_SCALE = GQA_HEAD_DIM ** -0.5
ATTN_IN_AUG = 1280

SSM_HEADS = 16
SSM_HEAD_DIM = 64
SSM_STATE = 128
SSM_D = 1024
SSM_XBC = 1536
SSM_CONV = 5
CONF_K = 31
CHUNK = 128
SSM_IN_AUG = 1024 + 1536 + 128 + 2048

N_EXPERTS = 32
TOP_K = 4
MOE_FF = 1024
SWIGLU_LIMIT = 7.0
SWIGLU_ALPHA = 1.702

TM = 256
TMOE = 512
ROUTE_LANES = 128
SC_WINDOW = 128
SC_ROW_WORDS = 256
LANES = 128
HALO = 16
VMEM_LIMIT = 56 * 1024 * 1024


def _cparams(sem):
    return pltpu.CompilerParams(dimension_semantics=sem, vmem_limit_bytes=VMEM_LIMIT)


def _rms(x, g):
    return x * lax.rsqrt(jnp.mean(x * x, axis=-1, keepdims=True) + EPS) * g


def _sigmoid(x):
    return 1.0 / (1.0 + jnp.exp(-x))


def _const_spec(shape):
    n = len(shape)
    return pl.BlockSpec(shape, lambda *_: (0,) * n)


def _mod_kernel(c_ref, w_ref, b_ref, o_ref):
    c = c_ref[...]
    o_ref[0] = jnp.dot(c * _sigmoid(c), w_ref[0], precision=HIGHEST, preferred_element_type=F32) + b_ref[0]


def _modulations(c_rows, w_mod, b_mod):
    depth, _, n = w_mod.shape
    tn = 512
    r = c_rows.shape[0]
    return pl.pallas_call(
        _mod_kernel,
        grid=(depth, n // tn),
        in_specs=[pl.BlockSpec((r, D), lambda l, j: (0, 0)),
                  pl.BlockSpec((1, D, tn), lambda l, j: (l, 0, j)),
                  pl.BlockSpec((1, 1, tn), lambda l, j: (l, 0, j))],
        out_specs=pl.BlockSpec((1, r, tn), lambda l, j: (l, 0, j)),
        out_shape=jax.ShapeDtypeStruct((depth, r, n), F32),
        compiler_params=_cparams(("arbitrary", "arbitrary")),
        name="modulations",
    )(c_rows, w_mod, b_mod.reshape(depth, 1, n))


def _stream_tile(ctx_ref, x_ref, nct):
    return jnp.where(pl.program_id(1) < nct, ctx_ref[0], x_ref[0])


def _stream_specs(nct):
    return [pl.BlockSpec((1, TM, D), lambda bb, i: (bb, jnp.minimum(i, nct - 1), 0)),
            pl.BlockSpec((1, TM, D), lambda bb, i: (bb, jnp.maximum(i - nct, 0), 0))]


def _attn_in_kernel(ctx_ref, x_ref, mods_ref, ng_ref, win_ref, gcq_ref, wuq_ref, gckv_ref, wkv_ref, gq_ref, gk_ref, bd_ref,
                    tab_ref, qm_ref, km_ref, vm_ref, qg_ref, kg_ref, vgs_ref, *, nct):
    x = _stream_tile(ctx_ref, x_ref, nct)
    sh = mods_ref[0, 0, 0:1, :]
    sc = mods_ref[0, 0, 1:2, :]
    h = _rms(x, ng_ref[...]) * (1.0 + sc) + sh
    u = jnp.dot(h.astype(BF16), win_ref[...], preferred_element_type=F32)

    tab = tab_ref[...]
    cg, sg = tab[:, 0:128], tab[:, 128:256]
    cm, sm = tab[:, 256:384], tab[:, 384:512]
    lane = lax.broadcasted_iota(jnp.int32, (x.shape[0], LANES), 1)
    even = (lane & 1) == 0
    lo = lane < 64

    def rope(v, c, s):
        partner = jnp.where(even, pltpu.roll(v, LANES - 1, 1), pltpu.roll(v, 1, 1))
        return v * c + partner * s

    cq = _rms(u[:, 0:256], gcq_ref[...])
    qm = jnp.dot(cq.astype(BF16), wuq_ref[...], preferred_element_type=F32)
    for hd in range(MLA_HEADS):
        qm_ref[0, hd] = rope(qm[:, hd * 128:(hd + 1) * 128], cm, sm).astype(BF16)
    ckv = _rms(u[:, 256:384], gckv_ref[...])
    kv = jnp.dot(ckv.astype(BF16), wkv_ref[...], preferred_element_type=F32)
    kr = rope(u[:, 384:512], cm, sm)
    for hd in range(MLA_HEADS):
        km_ref[0, hd] = (kv[:, hd * 128:(hd + 1) * 128] + kr).astype(BF16)
    one = jnp.ones((x.shape[0], LANES), F32)
    for p in range(MLA_HEADS // 2):
        blk = kv[:, 1024 + p * 128:1024 + (p + 1) * 128]
        vm_ref[0, 2 * p] = jnp.where(lo, blk, one).astype(BF16)
        vm_ref[0, 2 * p + 1] = jnp.where(lo, one, blk).astype(BF16)

    qg = u[:, 512:1024]
    ms = jnp.dot((qg * qg).astype(BF16), bd_ref[...], preferred_element_type=F32)
    qg = qg * lax.rsqrt(ms + EPS) * gq_ref[...]
    zero = jnp.zeros((x.shape[0], LANES), F32)
    for p in range(GQA_HEADS // 2):
        blk = rope(qg[:, p * 128:(p + 1) * 128], cg, sg)
        swp = pltpu.roll(blk, 64, 1)
        if p < 2:
            qg_ref[0, 2 * p] = jnp.where(lo, blk, zero).astype(BF16)
            qg_ref[0, 2 * p + 1] = jnp.where(lo, swp, zero).astype(BF16)
        else:
            qg_ref[0, 2 * p] = jnp.where(lo, zero, swp).astype(BF16)
            qg_ref[0, 2 * p + 1] = jnp.where(lo, zero, blk).astype(BF16)
    kg = u[:, 1024:1152]
    msk = jnp.dot((kg * kg).astype(BF16), bd_ref[0:128, 0:128], preferred_element_type=F32)
    kg_ref[0] = rope(kg * lax.rsqrt(msk + EPS) * gk_ref[...], cg, sg).astype(BF16)
    vg = u[:, 1152:1280]
    vsw = pltpu.roll(vg, 64, 1)
    vgs_ref[0, 0] = jnp.where(lo, vg, one).astype(BF16)
    vgs_ref[0, 1] = jnp.where(lo, one, vsw).astype(BF16)
    vgs_ref[0, 2] = jnp.where(lo, vsw, one).astype(BF16)
    vgs_ref[0, 3] = jnp.where(lo, one, vg).astype(BF16)


def _rope_tables(seq, ctx_len):
    rows = seq // GRID_W
    row = jnp.broadcast_to(jnp.arange(rows, dtype=F32)[:, None], (rows, GRID_W)).reshape(-1)
    col = jnp.broadcast_to(jnp.arange(GRID_W, dtype=F32)[None, :], (rows, GRID_W)).reshape(-1)

    def interleaved(rot_dim):
        n_freq = rot_dim // 4
        inv_freq = ROPE_THETA ** (-jnp.arange(n_freq, dtype=F32) / n_freq)
        ang = jnp.concatenate([row[:, None] * inv_freq, col[:, None] * inv_freq], axis=-1)
        cos = jnp.repeat(jnp.cos(ang), 2, axis=-1)
        sin = jnp.repeat(jnp.sin(ang), 2, axis=-1) * jnp.tile(jnp.array([-1.0, 1.0], F32), rot_dim // 2)
        return cos, sin

    cg, sg = interleaved(GQA_HEAD_DIM)
    cg, sg = jnp.tile(cg, (1, 2)), jnp.tile(sg, (1, 2))
    cm32, sm32 = interleaved(MLA_ROPE)
    ones, zeros = jnp.ones((seq, 64), F32), jnp.zeros((seq, 64), F32)
    cm = jnp.concatenate([ones, cm32, ones[:, :32]], axis=-1)
    sm = jnp.concatenate([zeros, sm32, zeros[:, :32]], axis=-1)
    lat = jnp.concatenate([cg, sg, cm, sm], axis=-1)
    ident = jnp.concatenate([jnp.ones((ctx_len, 128), F32), jnp.zeros((ctx_len, 128), F32)] * 2, axis=-1)
    return jnp.concatenate([ident, lat], axis=0)


def _attn_in(ctx, x, mods, ng, w_in, g_cq, w_uq, g_ckv, w_ukv, g_q, g_k, tab, nct):
    b = x.shape[0]
    t = ctx.shape[1] + x.shape[1]
    o1, o2, o3 = MLA_Q_LORA, MLA_Q_LORA + MLA_KV_LORA, MLA_Q_LORA + MLA_KV_LORA + MLA_ROPE
    zc = lambda n: jnp.zeros((D, n), F32)
    w_aug = jnp.concatenate([w_in[:, :o2], zc(64), w_in[:, o2:o3], zc(32), w_in[:, o3:]], axis=1).astype(BF16)
    wuq = jnp.pad(w_uq.reshape(MLA_Q_LORA, MLA_HEADS, 96), ((0, 0), (0, 0), (0, 32))).reshape(MLA_Q_LORA, 1024).astype(BF16)
    wukv = w_ukv.reshape(MLA_KV_LORA, MLA_HEADS, 128)
    wk = jnp.pad(wukv[:, :, :64], ((0, 0), (0, 0), (0, 64))).reshape(MLA_KV_LORA, 1024)
    wkv = jnp.concatenate([wk, wukv[:, :, 64:].reshape(MLA_KV_LORA, 512)], axis=1).astype(BF16)
    gcq = (g_cq * (MLA_SCALE * LOG2E)).reshape(1, -1)
    gq = (jnp.tile(g_q, GQA_HEADS) * (GQA_SCALE * LOG2E)).reshape(1, -1)
    gk = jnp.tile(g_k, 2).reshape(1, -1)
    bd = jnp.kron(jnp.eye(GQA_HEADS, dtype=F32), jnp.full((64, 64), 1.0 / 64, F32)).astype(BF16)
    nt = t // TM
    tok = lambda bb, i: (bb, i, 0)
    hm = lambda bb, i: (bb, 0, i, 0)
    return pl.pallas_call(
        functools.partial(_attn_in_kernel, nct=nct),
        grid=(b, nt),
        in_specs=_stream_specs(nct) + [
                  pl.BlockSpec((1, 1, 6, D), lambda bb, i: (bb, jnp.where(i < nct, 1, 0), 0, 0)),
                  _const_spec((1, D)), _const_spec((D, ATTN_IN_AUG)), _const_spec((1, MLA_Q_LORA)),
                  _const_spec((MLA_Q_LORA, 1024)), _const_spec((1, MLA_KV_LORA)), _const_spec((MLA_KV_LORA, 1536)),
                  _const_spec((1, 512)), _const_spec((1, 128)), _const_spec((512, 512)),
                  pl.BlockSpec((TM, 512), lambda bb, i: (i, 0))],
        out_specs=[pl.BlockSpec((1, 8, TM, 128), hm), pl.BlockSpec((1, 8, TM, 128), hm),
                   pl.BlockSpec((1, 8, TM, 128), hm), pl.BlockSpec((1, 8, TM, 128), hm),
                   pl.BlockSpec((1, TM, 128), tok), pl.BlockSpec((1, 4, TM, 128), hm)],
        out_shape=[jax.ShapeDtypeStruct((b, 8, t, 128), BF16), jax.ShapeDtypeStruct((b, 8, t, 128), BF16),
                   jax.ShapeDtypeStruct((b, 8, t, 128), BF16), jax.ShapeDtypeStruct((b, 8, t, 128), BF16),
                   jax.ShapeDtypeStruct((b, t, 128), BF16), jax.ShapeDtypeStruct((b, 4, t, 128), BF16)],
        compiler_params=_cparams(("arbitrary", "arbitrary")),
        name="attn_in",
    )(ctx, x, mods, ng.reshape(1, D), w_aug, gcq, wuq, g_ckv.reshape(1, -1), wkv, gq, gk, bd, tab)


def _softmax_pv(q, k, v, sum_lane):
    s = lax.dot_general(q, k, (((1,), (1,)), ((), ())), preferred_element_type=F32)
    m = jnp.max(s, axis=-1, keepdims=True)
    o = jnp.dot(jnp.exp2(s - m).astype(BF16), v, preferred_element_type=F32)
    return o / o[:, sum_lane:sum_lane + 1]


def _mla_kernel(q_ref, k_ref, v_ref, o_ref, *, nct, ctx_len):
    i = pl.program_id(1)
    lo = lax.broadcasted_iota(jnp.int32, (q_ref.shape[2], LANES), 1) < 64

    def run(nk):
        for pr in range(MLA_HEADS // 2):
            oa = _softmax_pv(q_ref[0, 2 * pr], k_ref[0, 2 * pr, 0:nk, :], v_ref[0, 2 * pr, 0:nk, :], 64)
            ob = _softmax_pv(q_ref[0, 2 * pr + 1], k_ref[0, 2 * pr + 1, 0:nk, :], v_ref[0, 2 * pr + 1, 0:nk, :], 0)
            o_ref[0, :, pr * 128:(pr + 1) * 128] = jnp.where(lo, oa, ob).astype(BF16)

    @pl.when(i < nct)
    def _():
        run(ctx_len)

    @pl.when(i >= nct)
    def _():
        run(k_ref.shape[2])


def _mla_attention(qm, km, vm, nct, ctx_len):
    b, _, t, _ = qm.shape
    nt = t // TM
    return pl.pallas_call(
        functools.partial(_mla_kernel, nct=nct, ctx_len=ctx_len),
        grid=(b, nt),
        in_specs=[pl.BlockSpec((1, 8, TM, 128), lambda bb, i: (bb, 0, i, 0)),
                  pl.BlockSpec((1, 8, t, 128), lambda bb, i: (bb, 0, 0, 0)),
                  pl.BlockSpec((1, 8, t, 128), lambda bb, i: (bb, 0, 0, 0))],
        out_specs=pl.BlockSpec((1, TM, 512), lambda bb, i: (bb, i, 0)),
        out_shape=jax.ShapeDtypeStruct((b, t, 512), BF16),
        compiler_params=_cparams(("arbitrary", "arbitrary")),
        name="mla_attention",
    )(qm, km, vm)


def _gqa_kernel(q_ref, k_ref, v_ref, o_ref, *, nct, ctx_len):
    i = pl.program_id(1)
    lo = lax.broadcasted_iota(jnp.int32, (q_ref.shape[2], LANES), 1) < 64

    def run(nk):
        k = k_ref[0, 0:nk, :]
        for g in range(2):
            ve = v_ref[0, 2 * g, 0:nk, :]
            vo = v_ref[0, 2 * g + 1, 0:nk, :]
            for pr in range(2):
                hd = 4 * g + 2 * pr
                oa = _softmax_pv(q_ref[0, hd], k, ve, 64)
                ob = _softmax_pv(q_ref[0, hd + 1], k, vo, 0)
                o_ref[0, :, (hd // 2) * 128:(hd // 2 + 1) * 128] = jnp.where(lo, oa, ob).astype(BF16)

    @pl.when(i < nct)
    def _():
        run(ctx_len)

    @pl.when(i >= nct)
    def _():
        run(k_ref.shape[1])


def _gqa_attention(qg, kg, vgs, nct, ctx_len):
    b, _, t, _ = qg.shape
    nt = t // TM
    return pl.pallas_call(
        functools.partial(_gqa_kernel, nct=nct, ctx_len=ctx_len),
        grid=(b, nt),
        in_specs=[pl.BlockSpec((1, 8, TM, 128), lambda bb, i: (bb, 0, i, 0)),
                  pl.BlockSpec((1, t, 128), lambda bb, i: (bb, 0, 0)),
                  pl.BlockSpec((1, 4, t, 128), lambda bb, i: (bb, 0, 0, 0))],
        out_specs=pl.BlockSpec((1, TM, 512), lambda bb, i: (bb, i, 0)),
        out_shape=jax.ShapeDtypeStruct((b, t, 512), BF16),
        compiler_params=_cparams(("arbitrary", "arbitrary")),
        name="gqa_attention",
    )(qg, kg, vgs)


def _pack_bf16_pairs(h):
    half = h.shape[1] // 2
    lo = pltpu.bitcast(h[:, :half].astype(BF16).astype(F32), jnp.uint32) >> 16
    hi = pltpu.bitcast(h[:, half:].astype(BF16).astype(F32), jnp.uint32) & jnp.uint32(0xFFFF0000)
    return lo | hi


def _unpack_bf16_pairs(w):
    return pltpu.bitcast(w << 16, F32), pltpu.bitcast(w & jnp.uint32(0xFFFF0000), F32)


def _store_planes(ref, words):
    ref[0, 0] = words[:, 0:SC_ROW_WORDS]
    ref[1, 0] = words[:, SC_ROW_WORDS:2 * SC_ROW_WORDS]


def _route_tile(lt, rt_ref, carry_ref, counted):
    n_e, rows = lt.shape
    eidx = lax.broadcasted_iota(jnp.int32, (n_e, rows), 0).astype(F32)
    work = lt
    vals, ids, hots = [], [], []
    for _ in range(TOP_K):
        m = jnp.max(work, axis=0, keepdims=True)
        idx = jnp.min(jnp.where(work == m, eidx, float(N_EXPERTS)), axis=0, keepdims=True)
        hot = eidx == idx
        vals.append(m)
        ids.append(idx)
        hots.append(hot)
        work = jnp.where(hot, -jnp.inf, work)
    exps = [jnp.exp(v - vals[0]) for v in vals]
    den = exps[0] + exps[1] + exps[2] + exps[3]
    mask = jnp.where(hots[0] | hots[1] | hots[2] | hots[3], 1.0, 0.0).astype(BF16)
    cum = jnp.dot(mask, rt_ref[...], preferred_element_type=F32)
    carry = carry_ref[...]
    before = cum[:, 0:rows] + jnp.concatenate([carry] * (rows // LANES), axis=1)
    ranks = [jnp.sum(jnp.where(h, before, 0.0), axis=0, keepdims=True) for h in hots]
    carry_ref[...] = carry + counted * cum[:, rows:rows + LANES]
    slab_t = jnp.concatenate(ids + ranks + [e / den for e in exps] + [jnp.zeros((ROUTE_LANES - 3 * TOP_K, rows), F32)],
                             axis=0)
    return slab_t.T


def _residual_norm_router(x, o, mods_ref, g2_ref, wrh_ref, wrl_ref, br_ref, rt_ref, x1_ref, hp_ref, slab_ref,
                          cnt_ref, carry_ref, *, nct, route_ctx):
    first = (pl.program_id(0) == 0) & (pl.program_id(1) == 0)

    @pl.when(first)
    def _():
        carry_ref[...] = jnp.zeros_like(carry_ref)

    g1 = mods_ref[0, 0, 2:3, :]
    sh2 = mods_ref[0, 0, 3:4, :]
    sc2 = mods_ref[0, 0, 4:5, :]
    x1 = x + g1 * o
    h2 = _rms(x1, g2_ref[...]) * (1.0 + sc2) + sh2
    x1_ref[0] = x1
    _store_planes(hp_ref, _pack_bf16_pairs(h2))
    hh = h2.astype(BF16)
    hl = (h2 - hh.astype(F32)).astype(BF16)
    logits = (jnp.dot(hh, wrh_ref[...], preferred_element_type=F32) + jnp.dot(hl, wrh_ref[...], preferred_element_type=F32)
              + jnp.dot(hh, wrl_ref[...], preferred_element_type=F32) + br_ref[...])
    counted = 1.0 if route_ctx else jnp.where(pl.program_id(1) >= nct, 1.0, 0.0)
    slab_ref[0] = _route_tile(logits.T[0:N_EXPERTS, :], rt_ref, carry_ref, counted)
    cnt_ref[...] = carry_ref[...]


def _post_attn_kernel(ctx_ref, x_ref, om_ref, og_ref, wout_ref, mods_ref, g2_ref, wrh_ref, wrl_ref, br_ref, rt_ref,
                      x1_ref, hp_ref, slab_ref, cnt_ref, carry_ref, *, nct):
    o = (jnp.dot(om_ref[0], wout_ref[0:512, :], preferred_element_type=F32)
         + jnp.dot(og_ref[0], wout_ref[512:1024, :], preferred_element_type=F32))
    _residual_norm_router(_stream_tile(ctx_ref, x_ref, nct), o, mods_ref, g2_ref, wrh_ref, wrl_ref, br_ref, rt_ref,
                          x1_ref, hp_ref, slab_ref, cnt_ref, carry_ref, nct=nct, route_ctx=True)


def _post_out_specs(b, t):
    tok = lambda bb, i: (bb, i, 0)
    specs = [pl.BlockSpec((1, TM, D), tok), pl.BlockSpec((2, 1, TM, SC_ROW_WORDS), lambda bb, i: (0, bb, i, 0)),
             pl.BlockSpec((1, TM, ROUTE_LANES), tok), _const_spec((N_EXPERTS, LANES))]
    shapes = [jax.ShapeDtypeStruct((b, t, D), F32), jax.ShapeDtypeStruct((2, b, t, SC_ROW_WORDS), jnp.uint32),
              jax.ShapeDtypeStruct((b, t, ROUTE_LANES), F32), jax.ShapeDtypeStruct((N_EXPERTS, LANES), F32)]
    return specs, shapes


def _router_operands(w_router, b_router):
    wp = jnp.pad(w_router, ((0, 0), (0, LANES - N_EXPERTS)))
    wrh = wp.astype(BF16)
    wrl = (wp - wrh.astype(F32)).astype(BF16)
    rt = jnp.concatenate([jnp.triu(jnp.ones((TM, TM), F32), 1), jnp.ones((TM, LANES), F32)], axis=1).astype(BF16)
    return wrh, wrl, jnp.pad(b_router, (0, LANES - N_EXPERTS)).reshape(1, LANES), rt


def _router_specs():
    return [_const_spec((D, LANES)), _const_spec((D, LANES)), _const_spec((1, LANES)), _const_spec((TM, TM + LANES))]


def _post_attn(ctx, x, o_m, o_g, w_out, mods, g2, w_router, b_router, nct):
    b, t, _ = o_m.shape
    tok = lambda bb, i: (bb, i, 0)
    out_specs, out_shape = _post_out_specs(b, t)
    return pl.pallas_call(
        functools.partial(_post_attn_kernel, nct=nct),
        grid=(b, t // TM),
        in_specs=_stream_specs(nct) + [
                  pl.BlockSpec((1, TM, 512), tok), pl.BlockSpec((1, TM, 512), tok),
                  _const_spec((1024, D)),
                  pl.BlockSpec((1, 1, 6, D), lambda bb, i: (bb, jnp.where(i < nct, 1, 0), 0, 0)),
                  _const_spec((1, D))] + _router_specs(),
        out_specs=out_specs, out_shape=out_shape,
        scratch_shapes=[pltpu.VMEM((N_EXPERTS, LANES), F32)],
        compiler_params=_cparams(("arbitrary", "arbitrary")),
        name="post_attn",
    )(ctx, x, o_m, o_g, w_out.astype(BF16), mods, g2.reshape(1, D), *_router_operands(w_router, b_router))


def _route_tables(slab, counts, n_tiles):
    e = slab[:, 0:TOP_K].astype(jnp.int32)
    rank = slab[:, TOP_K:2 * TOP_K].astype(jnp.int32)
    counts = counts[:, 0].astype(jnp.int32)
    padded = (counts + TMOE - 1) // TMOE * TMOE
    pend = jnp.cumsum(padded)
    pstart = pend - padded
    experts = jnp.arange(N_EXPERTS, dtype=jnp.int32)
    first_live = pend - counts
    pos = rank + jnp.sum(jnp.where(e[..., None] == experts, first_live, 0), axis=-1)
    tile_start = jnp.arange(n_tiles, dtype=jnp.int32) * TMOE
    tile_expert = jnp.minimum(jnp.sum(tile_start[:, None] >= pend[None, :], axis=-1), N_EXPERTS - 1).astype(jnp.int32)
    mine = tile_expert[:, None] == experts
    live = tile_start + TMOE - jnp.sum(jnp.where(mine, first_live, 0), axis=-1)
    tile_valid = jnp.where(tile_start < pend[-1], jnp.clip(live, 0, TMOE), 0).astype(jnp.int32)
    return pos, tile_expert, tile_valid


def _sc_mesh():
    return plsc.VectorSubcoreMesh(core_axis_name="core", subcore_axis_name="subcore")


def _sc_scatter_rows(src, idx, n_out, nb_tok, nb_src, nb_plane, off):
    m = idx.shape[0]
    steps_per_k = m // SC_WINDOW // TOP_K

    def src_block(i):
        q = i % steps_per_k
        plane = q // (steps_per_k // 2)
        r = q % (steps_per_k // 2)
        return (plane * nb_plane + (r // nb_tok) * nb_src + off + r % nb_tok, 0)

    @pl.kernel(out_type=jax.ShapeDtypeStruct((n_out, SC_ROW_WORDS), src.dtype), mesh=_sc_mesh(), scratch_types=[])
    def k(src_hbm, i_hbm, o_hbm):
        def body(x_vmem, i_vmem):
            pltpu.sync_copy(x_vmem, o_hbm.at[i_vmem.at[0]])

        pltpu.emit_pipeline(
            body,
            grid=(m // SC_WINDOW,),
            in_specs=[pl.BlockSpec((SC_WINDOW, SC_ROW_WORDS), src_block),
                      pl.BlockSpec((1, SC_WINDOW), lambda i: (0, i))],
            out_specs=[],
            core_axis_name=("core", "subcore"),
            dimension_semantics=(pltpu.PARALLEL,),
        )(src_hbm, i_hbm)

    return k(src, idx.reshape(1, m))


def _sc_gather_rows(src, idx):
    m = idx.shape[0]

    @pl.kernel(out_type=jax.ShapeDtypeStruct((m, SC_ROW_WORDS), src.dtype), mesh=_sc_mesh(), scratch_types=[])
    def k(src_hbm, i_hbm, o_hbm):
        def body(i_vmem, o_vmem):
            pltpu.sync_copy(src_hbm.at[i_vmem.at[0]], o_vmem)

        pltpu.emit_pipeline(
            body,
            grid=(m // SC_WINDOW,),
            in_specs=[pl.BlockSpec((1, SC_WINDOW), lambda i: (0, i))],
            out_specs=[pl.BlockSpec((SC_WINDOW, SC_ROW_WORDS), lambda i: (i, 0))],
            core_axis_name=("core", "subcore"),
            dimension_semantics=(pltpu.PARALLEL,),
        )(i_hbm, o_hbm)

    return k(src, idx.reshape(1, m))


def _moe_kernel(te_ref, tv_ref, xs_ref, wgu_ref, bgu_ref, wd_ref, bd_ref, ys_ref, wgu_bf, wd_bf):
    t = pl.program_id(0)
    prev = te_ref[jnp.maximum(t - 1, 0)]

    @pl.when((t == 0) | (te_ref[t] != prev))
    def _():
        wgu_bf[...] = wgu_ref[0, 0].astype(BF16)
        wd_bf[...] = wd_ref[0, 0].astype(BF16)

    valid = tv_ref[t]
    half = TMOE // 2

    def experts(rows):
        r0 = TMOE - rows
        lo, hi = _unpack_bf16_pairs(jnp.concatenate([xs_ref[0, r0:TMOE, :], xs_ref[1, r0:TMOE, :]], axis=1))
        live = lax.broadcasted_iota(jnp.int32, (rows, 1), 0) >= rows - valid
        x = jnp.where(live, jnp.concatenate([lo, hi], axis=1), 0.0).astype(BF16)
        gu = jnp.dot(x, wgu_bf[...], preferred_element_type=F32) + bgu_ref[0, 0]
        gate = jnp.minimum(gu[:, :MOE_FF], SWIGLU_LIMIT)
        up = jnp.clip(gu[:, MOE_FF:], -SWIGLU_LIMIT, SWIGLU_LIMIT)
        act = (up + 1.0) * (gate * _sigmoid(SWIGLU_ALPHA * gate))
        y = jnp.dot(act.astype(BF16), wd_bf[...], preferred_element_type=F32) + bd_ref[0, 0]
        words = _pack_bf16_pairs(y)
        ys_ref[0, r0:TMOE, :] = words[:, 0:SC_ROW_WORDS]
        ys_ref[1, r0:TMOE, :] = words[:, SC_ROW_WORDS:2 * SC_ROW_WORDS]

    @pl.when(valid > half)
    def _():
        experts(TMOE)

    @pl.when((valid > 0) & (valid <= half))
    def _():
        experts(half)
        ys_ref[:, 0:half, :] = jnp.zeros((2, half, SC_ROW_WORDS), jnp.uint32)

    @pl.when(valid == 0)
    def _():
        ys_ref[...] = jnp.zeros_like(ys_ref)


def _moe_experts(xs, tile_expert, tile_valid, layer, w_gu, b_gu, w_d, b_d):
    n_rows = xs.shape[1]
    n_tiles = n_rows // TMOE
    depth = w_gu.shape[0]
    grid_spec = pltpu.PrefetchScalarGridSpec(
        num_scalar_prefetch=2,
        grid=(n_tiles,),
        in_specs=[pl.BlockSpec((2, TMOE, SC_ROW_WORDS), lambda t, te, tv: (0, t, 0)),
                  pl.BlockSpec((1, 1, D, 2 * MOE_FF), lambda t, te, tv: (layer, te[t], 0, 0)),
                  pl.BlockSpec((1, 1, 1, 2 * MOE_FF), lambda t, te, tv: (layer, te[t], 0, 0)),
                  pl.BlockSpec((1, 1, MOE_FF, D), lambda t, te, tv: (layer, te[t], 0, 0)),
                  pl.BlockSpec((1, 1, 1, D), lambda t, te, tv: (layer, te[t], 0, 0))],
        out_specs=pl.BlockSpec((2, TMOE, SC_ROW_WORDS), lambda t, te, tv: (0, t, 0)),
        scratch_shapes=[pltpu.VMEM((D, 2 * MOE_FF), BF16), pltpu.VMEM((MOE_FF, D), BF16)],
    )
    return pl.pallas_call(
        _moe_kernel,
        grid_spec=grid_spec,
        out_shape=jax.ShapeDtypeStruct((2, n_rows, SC_ROW_WORDS), jnp.uint32),
        compiler_params=_cparams(("arbitrary",)),
        name="moe_experts",
    )(tile_expert, tile_valid, xs, w_gu, b_gu.reshape(depth, N_EXPERTS, 1, -1), w_d, b_d.reshape(depth, N_EXPERTS, 1, -1))


def _moe(hp, slab, counts, layer, w_gu, b_gu, w_d, b_d, n_seq, row0):
    _, b, t, _ = hp.shape
    n_tok = b * n_seq
    n_tiles = -(-n_tok * TOP_K // TMOE) + N_EXPERTS
    n_rows = n_tiles * TMOE
    pos, tile_expert, tile_valid = _route_tables(slab, counts, n_tiles)
    idx = (pos.T[:, None, :] + (jnp.arange(2, dtype=jnp.int32) * n_rows)[None, :, None]).reshape(-1)
    xs = _sc_scatter_rows(hp.reshape(2 * b * t, SC_ROW_WORDS), idx, 2 * n_rows, n_seq // SC_WINDOW, t // SC_WINDOW,
                          b * t // SC_WINDOW, row0 // SC_WINDOW)
    ys = _moe_experts(xs.reshape(2, n_rows, SC_ROW_WORDS), tile_expert, tile_valid, layer, w_gu, b_gu, w_d, b_d)
    g = _sc_gather_rows(ys.reshape(2 * n_rows, SC_ROW_WORDS), idx)
    return g.reshape(TOP_K, 2, b, n_seq, SC_ROW_WORDS)


def _combine(g_ref, slab):
    acc_lo = acc_hi = None
    for k in range(TOP_K):
        lo, hi = _unpack_bf16_pairs(jnp.concatenate([g_ref[k, 0, 0], g_ref[k, 1, 0]], axis=1))
        w = slab[:, 2 * TOP_K + k:2 * TOP_K + k + 1]
        acc_lo = w * lo if acc_lo is None else acc_lo + w * lo
        acc_hi = w * hi if acc_hi is None else acc_hi + w * hi
    return jnp.concatenate([acc_lo, acc_hi], axis=1)


def _ssm_in_kernel(x_ref, g_ref, slab_ref, modsp_ref, mods_ref, ng_ref, win_ref, x1_ref, z_ref, xbc_ref, dt_ref, v_ref):
    x = x_ref[0] + modsp_ref[0, 0, 5:6, :] * _combine(g_ref, slab_ref[0])
    x1_ref[0] = x
    sh = mods_ref[0, 0, 0:1, :]
    sc = mods_ref[0, 0, 1:2, :]
    h = _rms(x, ng_ref[...]) * (1.0 + sc) + sh
    u = jnp.dot(h.astype(BF16), win_ref[...], preferred_element_type=F32)
    z_ref[0] = u[:, 0:1024]
    xbc_ref[0] = u[:, 1024:2560]
    dt_ref[0] = u[:, 2560:2688]
    v_ref[0] = u[:, 2688:3712] * _sigmoid(u[:, 3712:4736])


def _ssm_in(x_all, g_all, slab, mods_prev, mods, ng, w_in, nct):
    b, t, _ = x_all.shape
    hi = SSM_D + SSM_XBC + SSM_HEADS
    w_aug = jnp.concatenate([w_in[:, :hi], jnp.zeros((D, 128 - SSM_HEADS), F32), w_in[:, hi:]], axis=1).astype(BF16)
    tok = lambda bb, i: (bb, i, 0)
    modspec = pl.BlockSpec((1, 1, 6, D), lambda bb, i: (bb, jnp.where(i < nct, 1, 0), 0, 0))
    return pl.pallas_call(
        _ssm_in_kernel,
        grid=(b, t // TM),
        in_specs=[pl.BlockSpec((1, TM, D), tok),
                  pl.BlockSpec((TOP_K, 2, 1, TM, SC_ROW_WORDS), lambda bb, i: (0, 0, bb, i, 0)),
                  pl.BlockSpec((1, TM, ROUTE_LANES), tok), modspec, modspec,
                  _const_spec((1, D)), _const_spec((D, SSM_IN_AUG))],
        out_specs=[pl.BlockSpec((1, TM, D), tok), pl.BlockSpec((1, TM, 1024), tok), pl.BlockSpec((1, TM, SSM_XBC), tok),
                   pl.BlockSpec((1, TM, 128), tok), pl.BlockSpec((1, TM, 1024), tok)],
        out_shape=[jax.ShapeDtypeStruct((b, t, D), F32), jax.ShapeDtypeStruct((b, t, 1024), F32),
                   jax.ShapeDtypeStruct((b, t, SSM_XBC), F32), jax.ShapeDtypeStruct((b, t, 128), F32),
                   jax.ShapeDtypeStruct((b, t, 1024), F32)],
        compiler_params=_cparams(("arbitrary", "arbitrary")),
        name="ssm_in",
    )(x_all, g_all, slab, mods_prev, mods, ng.reshape(1, D), w_aug)


def _conv_kernel(x_ref, w_ref, b_ref, o_ref, pad_ref, *, taps, ctx_len, silu):
    t = x_ref.shape[1]
    ct = x_ref.shape[2]
    half = taps // 2
    zeros = jnp.zeros((HALO, ct), F32)
    pad_ref[0:HALO, :] = zeros
    pad_ref[HALO:HALO + ctx_len, :] = x_ref[0, 0:ctx_len, :]
    pad_ref[HALO + ctx_len:2 * HALO + ctx_len, :] = zeros
    pad_ref[2 * HALO + ctx_len:2 * HALO + t, :] = x_ref[0, ctx_len:t, :]
    pad_ref[2 * HALO + t:3 * HALO + t, :] = zeros
    w = w_ref[...]
    bias = b_ref[...]
    rows = CHUNK + 2 * HALO

    def segment(out_start, length, pad_start):
        def body(c, carry):
            base = pl.multiple_of(c * CHUNK, CHUNK)
            win = pad_ref[pl.ds(pad_start - HALO + base, rows), :]
            acc = jnp.broadcast_to(bias, (CHUNK, ct))
            for r in range(8):
                ks = [k for k in range(taps) if (HALO + k - half) % 8 == r]
                if not ks:
                    continue
                rolled = win if r == 0 else pltpu.roll(win, rows - r, 0)
                for k in ks:
                    off = HALO + k - half - r
                    acc = acc + w[k:k + 1, :] * rolled[off:off + CHUNK, :]
            if silu:
                acc = acc * _sigmoid(acc)
            o_ref[0, pl.ds(out_start + base, CHUNK), :] = acc
            return carry
        lax.fori_loop(0, length // CHUNK, body, 0)

    segment(0, ctx_len, HALO)
    segment(ctx_len, t - ctx_len, 2 * HALO + ctx_len)


def _depthwise_conv(x, w, bias, ctx_len, silu):
    b, t, c = x.shape
    taps = w.shape[0]
    ct = 256
    return pl.pallas_call(
        functools.partial(_conv_kernel, taps=taps, ctx_len=ctx_len, silu=silu),
        grid=(b, c // ct),
        in_specs=[pl.BlockSpec((1, t, ct), lambda bb, j: (bb, 0, j)),
                  pl.BlockSpec((taps, ct), lambda bb, j: (0, j)),
                  pl.BlockSpec((1, ct), lambda bb, j: (0, j))],
        out_specs=pl.BlockSpec((1, t, ct), lambda bb, j: (bb, 0, j)),
        out_shape=jax.ShapeDtypeStruct((b, t, c), F32),
        scratch_shapes=[pltpu.VMEM((t + 3 * HALO, ct), F32)],
        compiler_params=_cparams(("arbitrary", "arbitrary")),
        name=f"depthwise_conv{taps}",
    )(x, w, bias.reshape(1, c))


def _ssd_chunk(xbc, dt_raw, a_row, bias_row, state_ref, d, reverse):
    L = CHUNK
    ri = lax.broadcasted_iota(jnp.int32, (L, L), 0)
    ci = lax.broadcasted_iota(jnp.int32, (L, L), 1)
    lane_lo = lax.broadcasted_iota(jnp.int32, (L, LANES), 1) < 64
    mask = (ci >= ri) if reverse else (ri >= ci)
    tri = mask.astype(F32)

    xdt_in = dt_raw + bias_row
    dt = jnp.maximum(xdt_in, 0.0) + jnp.log1p(jnp.exp(-jnp.abs(xdt_in)))
    da = dt * a_row
    cs = jnp.dot(tri, da, precision=HIGHEST, preferred_element_type=F32)
    cs_t = cs.T
    end = 0 if reverse else L - 1
    a_end = cs[end:end + 1, :]

    def lanes(v, hh):
        return jnp.broadcast_to(v[:, hh:hh + 1], (v.shape[0], LANES))

    ys = []
    for g in range(2):
        bm = xbc[:, SSM_D + g * SSM_STATE:SSM_D + (g + 1) * SSM_STATE].astype(BF16)
        cm = xbc[:, SSM_D + 2 * SSM_STATE + g * SSM_STATE:SSM_D + 2 * SSM_STATE + (g + 1) * SSM_STATE].astype(BF16)
        cb = lax.dot_general(cm, bm, (((1,), (1,)), ((), ())), preferred_element_type=F32)
        state = state_ref[d, g]
        y_off = jnp.dot(cm, state.astype(BF16), preferred_element_type=F32)
        xdd_blocks, sdec_blocks = [], []
        for p in range(4):
            hp = g * 4 + p
            gmats, e_cols, d_outs = [], [], []
            for j in range(2):
                hh = 2 * hp + j
                col = lanes(cs, hh)
                row = jnp.broadcast_to(cs_t[hh:hh + 1, :], (L, L))
                dec = jnp.exp(jnp.where(mask, col - row, -jnp.inf))
                gmats.append((cb * dec).astype(BF16))
                e_cols.append(jnp.exp(col))
                d_outs.append(jnp.exp(lanes(a_end, hh) - col))
            din = jnp.where(lane_lo, e_cols[0], e_cols[1])
            dout = jnp.where(lane_lo, d_outs[0], d_outs[1])
            dtp = jnp.where(lane_lo, lanes(dt, 2 * hp), lanes(dt, 2 * hp + 1))
            xdt = xbc[:, hp * 128:(hp + 1) * 128] * dtp
            xdd_blocks.append((xdt * dout).astype(BF16))
            sdec_blocks.append(din[end:end + 1, :])
            xdt = xdt.astype(BF16)
            ya = jnp.dot(gmats[0], xdt, preferred_element_type=F32)
            yb = jnp.dot(gmats[1], xdt, preferred_element_type=F32)
            ys.append(jnp.where(lane_lo, ya, yb) + din * y_off[:, p * 128:(p + 1) * 128])
        upd = lax.dot_general(bm, jnp.concatenate(xdd_blocks, axis=1), (((0,), (0,)), ((), ())),
                              preferred_element_type=F32)
        state_ref[d, g] = state * jnp.concatenate(sdec_blocks, axis=1) + upd
    return jnp.concatenate(ys, axis=1)


def _ssd_kernel(xf_ref, xb_ref, dtf_ref, dtb_ref, alog_ref, bias_ref, yf_ref, yb_ref, state_ref):
    @pl.when(pl.program_id(1) == 0)
    def _():
        state_ref[...] = jnp.zeros_like(state_ref)

    a = -jnp.exp(alog_ref[...])
    bias = bias_ref[...]
    yf_ref[0] = _ssd_chunk(xf_ref[0], dtf_ref[0], a[0:1, :], bias[0:1, :], state_ref, 0, False)
    yb_ref[0] = _ssd_chunk(xb_ref[0], dtb_ref[0], a[1:2, :], bias[1:2, :], state_ref, 1, True)


def _ssd(xbc, dt, a_log, dt_bias, ctx_len):
    b, t, _ = xbc.shape
    nc = t // CHUNK
    ncc = ctx_len // CHUNK

    def fwd(bb, j):
        return (bb, j, 0)

    def bwd(bb, j):
        return (bb, jnp.where(j < ncc, ncc - 1 - j, nc - 1 + ncc - j), 0)

    pad = lambda v: jnp.pad(v, ((0, 0), (0, 128 - SSM_HEADS)))
    return pl.pallas_call(
        _ssd_kernel,
        grid=(b, nc),
        in_specs=[pl.BlockSpec((1, CHUNK, SSM_XBC), fwd), pl.BlockSpec((1, CHUNK, SSM_XBC), bwd),
                  pl.BlockSpec((1, CHUNK, 128), fwd), pl.BlockSpec((1, CHUNK, 128), bwd),
                  _const_spec((2, 128)), _const_spec((2, 128))],
        out_specs=[pl.BlockSpec((1, CHUNK, SSM_D), fwd), pl.BlockSpec((1, CHUNK, SSM_D), bwd)],
        out_shape=[jax.ShapeDtypeStruct((b, t, SSM_D), F32), jax.ShapeDtypeStruct((b, t, SSM_D), F32)],
        scratch_shapes=[pltpu.VMEM((2, 2, SSM_STATE, 512), F32)],
        compiler_params=_cparams(("arbitrary", "arbitrary")),
        name="ssd_scan",
    )(xbc, xbc, dt, dt, pad(a_log), pad(dt_bias))


def _post_ssm_kernel(x_ref, yf_ref, yb_ref, xs_ref, z_ref, v_ref, dsk_ref, sg_ref, lng_ref, lnb_ref, wout_ref,
                     mods_ref, g2_ref, wrh_ref, wrl_ref, br_ref, rt_ref, x1_ref, hp_ref, slab_ref, cnt_ref, carry_ref,
                     *, nct):
    y = yf_ref[0] + yb_ref[0] + dsk_ref[...] * xs_ref[0]
    z = z_ref[0]
    y_ssm = _rms(y * (z * _sigmoid(z)), sg_ref[...])
    v = v_ref[0]
    mu = jnp.mean(v, axis=-1, keepdims=True)
    vc = v - mu
    ln = vc * lax.rsqrt(jnp.mean(vc * vc, axis=-1, keepdims=True) + EPS) * lng_ref[...] + lnb_ref[...]
    y_conv = ln * _sigmoid(ln)
    o = (jnp.dot(y_ssm.astype(BF16), wout_ref[0:1024, :], preferred_element_type=F32)
         + jnp.dot(y_conv.astype(BF16), wout_ref[1024:2048, :], preferred_element_type=F32))
    _residual_norm_router(x_ref[0], o, mods_ref, g2_ref, wrh_ref, wrl_ref, br_ref, rt_ref, x1_ref, hp_ref, slab_ref,
                          cnt_ref, carry_ref, nct=nct, route_ctx=False)


def _post_ssm(x_all, yf, yb, xbc_act, z, v, d_skip, ssm_norm_g, ln_g, ln_b, w_out, mods, g2, w_router, b_router, nct):
    b, t, _ = x_all.shape
    tok = lambda bb, i: (bb, i, 0)
    dsk = jnp.repeat(d_skip[0] + d_skip[1], SSM_HEAD_DIM).reshape(1, SSM_D)
    out_specs, out_shape = _post_out_specs(b, t)
    row = lambda: pl.BlockSpec((1, TM, 1024), tok)
    return pl.pallas_call(
        functools.partial(_post_ssm_kernel, nct=nct),
        grid=(b, t // TM),
        in_specs=[row(), row(), row(), row(), row(), row(),
                  _const_spec((1, SSM_D)), _const_spec((1, SSM_D)), _const_spec((1, D)), _const_spec((1, D)),
                  _const_spec((2048, D)),
                  pl.BlockSpec((1, 1, 6, D), lambda bb, i: (bb, jnp.where(i < nct, 1, 0), 0, 0)),
                  _const_spec((1, D))] + _router_specs(),
        out_specs=out_specs, out_shape=out_shape,
        scratch_shapes=[pltpu.VMEM((N_EXPERTS, LANES), F32)],
        compiler_params=_cparams(("arbitrary", "arbitrary")),
        name="post_ssm",
    )(x_all, yf, yb, xbc_act, z, v, dsk, ssm_norm_g.reshape(1, -1), ln_g.reshape(1, -1), ln_b.reshape(1, -1),
      w_out.astype(BF16), mods, g2.reshape(1, D), *_router_operands(w_router, b_router))


def _final_kernel(x_ref, g_ref, slab_ref, mods_ref, fg_ref, o_ref):
    x = x_ref[0] + mods_ref[0, 0, 5:6, :] * _combine(g_ref, slab_ref[0])
    o_ref[0] = _rms(x, fg_ref[...])


def _final(x_all, g_lat, slab, mods, final_g, nct):
    b, s = g_lat.shape[2], g_lat.shape[3]
    return pl.pallas_call(
        _final_kernel,
        grid=(b, s // TM),
        in_specs=[pl.BlockSpec((1, TM, D), lambda bb, i: (bb, i + nct, 0)),
                  pl.BlockSpec((TOP_K, 2, 1, TM, SC_ROW_WORDS), lambda bb, i: (0, 0, bb, i, 0)),
                  pl.BlockSpec((1, TM, ROUTE_LANES), lambda bb, i: (bb, i + nct, 0)),
                  pl.BlockSpec((1, 1, 6, D), lambda bb, i: (bb, 0, 0, 0)),
                  _const_spec((1, D))],
        out_specs=pl.BlockSpec((1, TM, D), lambda bb, i: (bb, i, 0)),
        out_shape=jax.ShapeDtypeStruct((b, s, D), F32),
        compiler_params=_cparams(("arbitrary", "arbitrary")),
        name="final_norm",
    )(x_all, g_lat, slab, mods, final_g.reshape(1, D))


def kernel(x, c, ctx, c_ctx, w_mod, b_mod, norm_g, attn_w_in, mla_g_cq, mla_w_uq, mla_g_ckv, mla_w_ukv, gqa_g_q, gqa_g_k, attn_w_out, ssm_w_in, ssm_conv_w, ssm_conv_b, ssm_a_log, ssm_dt_bias, ssm_d, ssm_norm_g, conf_dw_w, conf_dw_b, conf_ln_g, conf_ln_b, ssm_w_out, moe_w_router, moe_b_router, moe_w_gate_up, moe_b_gate_up, moe_w_down, moe_b_down, final_g):
    b, s, _ = x.shape
    ctx_len = ctx.shape[1]
    t = ctx_len + s
    assert ctx_len % TM == 0 and s % TM == 0 and s % GRID_W == 0
    nct = ctx_len // TM

    c_rows = jnp.concatenate([c, c_ctx[None, :], jnp.zeros((-(b + 1) % 8, D), F32)], axis=0)
    mod_all = _modulations(c_rows, w_mod, b_mod)
    mods = []
    for i in range(w_mod.shape[0]):
        lat = mod_all[i, :b].reshape(b, 1, 6, D)
        cm = jnp.broadcast_to(mod_all[i, b].reshape(1, 1, 6, D), (b, 1, 6, D))
        mods.append(jnp.concatenate([lat, cm], axis=1))

    tab = _rope_tables(s, ctx_len)
    qm, km, vm, qg, kg, vgs = _attn_in(ctx, x, mods[0], norm_g[0, 0], attn_w_in[0], mla_g_cq[0], mla_w_uq[0], mla_g_ckv[0],
                                       mla_w_ukv[0], gqa_g_q[0], gqa_g_k[0], tab, nct)
    o_m = _mla_attention(qm, km, vm, nct, ctx_len)
    o_g = _gqa_attention(qg, kg, vgs, nct, ctx_len)
    x_all, hp, slab0, counts = _post_attn(ctx, x, o_m, o_g, attn_w_out[0], mods[0], norm_g[0, 1], moe_w_router[0],
                                          moe_b_router[0], nct)
    g_all = _moe(hp, slab0.reshape(b * t, ROUTE_LANES), counts, 0, moe_w_gate_up, moe_b_gate_up, moe_w_down, moe_b_down, t, 0)

    x_all, z, xbc, dt, v = _ssm_in(x_all, g_all, slab0, mods[0], mods[1], norm_g[1, 0], ssm_w_in[0], nct)
    xbc_act = _depthwise_conv(xbc, ssm_conv_w[0], ssm_conv_b[0], ctx_len, True)
    v_conv = _depthwise_conv(v, conf_dw_w[0], conf_dw_b[0], ctx_len, False)
    yf, yb = _ssd(xbc_act, dt, ssm_a_log[0], ssm_dt_bias[0], ctx_len)
    x_all, hp, slab1, counts = _post_ssm(x_all, yf, yb, xbc_act, z, v_conv, ssm_d[0], ssm_norm_g[0], conf_ln_g[0],
                                         conf_ln_b[0], ssm_w_out[0], mods[1], norm_g[1, 1], moe_w_router[1],
                                         moe_b_router[1], nct)
    g_lat = _moe(hp, slab1[:, ctx_len:].reshape(b * s, ROUTE_LANES), counts, 1, moe_w_gate_up, moe_b_gate_up, moe_w_down,
                 moe_b_down, s, ctx_len)
    return _final(x_all, g_lat, slab1, mods[1], final_g, nct)
```

```python
import functools

import jax
import jax.numpy as jnp
from jax import lax
from jax.experimental import pallas as pl
from jax.experimental.pallas import tpu as pltpu
from jax.experimental.pallas import tpu_sc as plsc

F32 = jnp.float32
BF16 = jnp.bfloat16
HIGHEST = lax.Precision.HIGHEST

D = 1024
EPS = 1e-6
GRID_W = 64
ROPE_THETA = 10000.0
LOG2E = 1.4426950408889634

MLA_HEADS = 8
MLA_NOPE = 64
MLA_ROPE = 32
MLA_Q_LORA = 256
MLA_KV_LORA = 128
MLA_SCALE = (MLA_NOPE + MLA_ROPE) ** -0.5
GQA_HEADS = 8
GQA_HEAD_DIM = 64
GQA_SCALE = GQA_HEAD_DIM ** -0.5
ATTN_IN_AUG = 1280

SSM_HEADS = 16
SSM_HEAD_DIM = 64
SSM_STATE = 128
SSM_D = 1024
SSM_XBC = 1536
SSM_CONV = 5
CONF_K = 31
CHUNK = 128
SSM_IN_AUG = 1024 + 1536 + 128 + 2048

N_EXPERTS = 32
TOP_K = 4
MOE_FF = 1024
SWIGLU_LIMIT = 7.0
SWIGLU_ALPHA = 1.702

TM = 256
TMOE = 512
ROUTE_LANES = 128
ROUTE_ROWS = 16
SC_WINDOW = 128
SC_ROW_WORDS = 256
LANES = 128
HALO = 16
VMEM_LIMIT = 56 * 1024 * 1024


def _cparams(sem):
    return pltpu.CompilerParams(dimension_semantics=sem, vmem_limit_bytes=VMEM_LIMIT)


def _rms(x, g):
    return x * lax.rsqrt(jnp.mean(x * x, axis=-1, keepdims=True) + EPS) * g


def _sigmoid(x):
    return 1.0 / (1.0 + jnp.exp(-x))


def _const_spec(shape):
    n = len(shape)
    return pl.BlockSpec(shape, lambda *_: (0,) * n)


def _mod_kernel(c_ref, w_ref, b_ref, o_ref):
    c = c_ref[...]
    o_ref[0] = jnp.dot(c * _sigmoid(c), w_ref[0], precision=HIGHEST, preferred_element_type=F32) + b_ref[0]


def _modulations(c_rows, w_mod, b_mod):
    depth, _, n = w_mod.shape
    tn = 512
    r = c_rows.shape[0]
    return pl.pallas_call(
        _mod_kernel,
        grid=(depth, n // tn),
        in_specs=[pl.BlockSpec((r, D), lambda l, j: (0, 0)),
                  pl.BlockSpec((1, D, tn), lambda l, j: (l, 0, j)),
                  pl.BlockSpec((1, 1, tn), lambda l, j: (l, 0, j))],
        out_specs=pl.BlockSpec((1, r, tn), lambda l, j: (l, 0, j)),
        out_shape=jax.ShapeDtypeStruct((depth, r, n), F32),
        compiler_params=_cparams(("arbitrary", "arbitrary")),
        name="modulations",
    )(c_rows, w_mod, b_mod.reshape(depth, 1, n))


def _stream_tile(ctx_ref, x_ref, nct):
    return jnp.where(pl.program_id(1) < nct, ctx_ref[0], x_ref[0])


def _stream_specs(nct):
    return [pl.BlockSpec((1, TM, D), lambda bb, i: (bb, jnp.minimum(i, nct - 1), 0)),
            pl.BlockSpec((1, TM, D), lambda bb, i: (bb, jnp.maximum(i - nct, 0), 0))]


def _attn_in_kernel(ctx_ref, x_ref, mods_ref, ng_ref, win_ref, gcq_ref, wuq_ref, gckv_ref, wkv_ref, gq_ref, gk_ref, bd_ref,
                    tab_ref, qm_ref, km_ref, vm_ref, qg_ref, kg_ref, vgs_ref, *, nct):
    x = _stream_tile(ctx_ref, x_ref, nct)
    sh = mods_ref[0, 0, 0:1, :]
    sc = mods_ref[0, 0, 1:2, :]
    h = _rms(x, ng_ref[...]) * (1.0 + sc) + sh
    u = jnp.dot(h.astype(BF16), win_ref[...], preferred_element_type=F32)

    tab = tab_ref[...]
    cg, sg = tab[:, 0:128], tab[:, 128:256]
    cm, sm = tab[:, 256:384], tab[:, 384:512]
    lane = lax.broadcasted_iota(jnp.int32, (x.shape[0], LANES), 1)
    even = (lane & 1) == 0
    lo = lane < 64

    def rope(v, c, s):
        partner = jnp.where(even, pltpu.roll(v, LANES - 1, 1), pltpu.roll(v, 1, 1))
        return v * c + partner * s

    cq = _rms(u[:, 0:256], gcq_ref[...])
    qm = jnp.dot(cq.astype(BF16), wuq_ref[...], preferred_element_type=F32)
    for hd in range(MLA_HEADS):
        qm_ref[0, hd] = rope(qm[:, hd * 128:(hd + 1) * 128], cm, sm).astype(BF16)
    ckv = _rms(u[:, 256:384], gckv_ref[...])
    kv = jnp.dot(ckv.astype(BF16), wkv_ref[...], preferred_element_type=F32)
    kr = rope(u[:, 384:512], cm, sm)
    for hd in range(MLA_HEADS):
        km_ref[0, hd] = (kv[:, hd * 128:(hd + 1) * 128] + kr).astype(BF16)
    one = jnp.ones((x.shape[0], LANES), F32)
    for p in range(MLA_HEADS // 2):
        blk = kv[:, 1024 + p * 128:1024 + (p + 1) * 128]
        vm_ref[0, 2 * p] = jnp.where(lo, blk, one).astype(BF16)
        vm_ref[0, 2 * p + 1] = jnp.where(lo, one, blk).astype(BF16)

    qg = u[:, 512:1024]
    ms = jnp.dot((qg * qg).astype(BF16), bd_ref[...], preferred_element_type=F32)
    qg = qg * lax.rsqrt(ms + EPS) * gq_ref[...]
    zero = jnp.zeros((x.shape[0], LANES), F32)
    for p in range(GQA_HEADS // 2):
        blk = rope(qg[:, p * 128:(p + 1) * 128], cg, sg)
        swp = pltpu.roll(blk, 64, 1)
        if p < 2:
            qg_ref[0, 2 * p] = jnp.where(lo, blk, zero).astype(BF16)
            qg_ref[0, 2 * p + 1] = jnp.where(lo, swp, zero).astype(BF16)
        else:
            qg_ref[0, 2 * p] = jnp.where(lo, zero, swp).astype(BF16)
            qg_ref[0, 2 * p + 1] = jnp.where(lo, zero, blk).astype(BF16)
    kg = u[:, 1024:1152]
    msk = jnp.dot((kg * kg).astype(BF16), bd_ref[0:128, 0:128], preferred_element_type=F32)
    kg_ref[0] = rope(kg * lax.rsqrt(msk + EPS) * gk_ref[...], cg, sg).astype(BF16)
    vg = u[:, 1152:1280]
    vsw = pltpu.roll(vg, 64, 1)
    vgs_ref[0, 0] = jnp.where(lo, vg, one).astype(BF16)
    vgs_ref[0, 1] = jnp.where(lo, one, vsw).astype(BF16)
    vgs_ref[0, 2] = jnp.where(lo, vsw, one).astype(BF16)
    vgs_ref[0, 3] = jnp.where(lo, one, vg).astype(BF16)


def _rope_tables(seq, ctx_len):
    rows = seq // GRID_W
    row = jnp.broadcast_to(jnp.arange(rows, dtype=F32)[:, None], (rows, GRID_W)).reshape(-1)
    col = jnp.broadcast_to(jnp.arange(GRID_W, dtype=F32)[None, :], (rows, GRID_W)).reshape(-1)

    def interleaved(rot_dim):
        n_freq = rot_dim // 4
        inv_freq = ROPE_THETA ** (-jnp.arange(n_freq, dtype=F32) / n_freq)
        ang = jnp.concatenate([row[:, None] * inv_freq, col[:, None] * inv_freq], axis=-1)
        cos = jnp.repeat(jnp.cos(ang), 2, axis=-1)
        sin = jnp.repeat(jnp.sin(ang), 2, axis=-1) * jnp.tile(jnp.array([-1.0, 1.0], F32), rot_dim // 2)
        return cos, sin

    cg, sg = interleaved(GQA_HEAD_DIM)
    cg, sg = jnp.tile(cg, (1, 2)), jnp.tile(sg, (1, 2))
    cm32, sm32 = interleaved(MLA_ROPE)
    ones, zeros = jnp.ones((seq, 64), F32), jnp.zeros((seq, 64), F32)
    cm = jnp.concatenate([ones, cm32, ones[:, :32]], axis=-1)
    sm = jnp.concatenate([zeros, sm32, zeros[:, :32]], axis=-1)
    lat = jnp.concatenate([cg, sg, cm, sm], axis=-1)
    ident = jnp.concatenate([jnp.ones((ctx_len, 128), F32), jnp.zeros((ctx_len, 128), F32)] * 2, axis=-1)
    return jnp.concatenate([ident, lat], axis=0)


def _attn_in(ctx, x, mods, ng, w_in, g_cq, w_uq, g_ckv, w_ukv, g_q, g_k, tab, nct):
    b = x.shape[0]
    t = ctx.shape[1] + x.shape[1]
    o1, o2, o3 = MLA_Q_LORA, MLA_Q_LORA + MLA_KV_LORA, MLA_Q_LORA + MLA_KV_LORA + MLA_ROPE
    zc = lambda n: jnp.zeros((D, n), F32)
    w_aug = jnp.concatenate([w_in[:, :o2], zc(64), w_in[:, o2:o3], zc(32), w_in[:, o3:]], axis=1).astype(BF16)
    wuq = jnp.pad(w_uq.reshape(MLA_Q_LORA, MLA_HEADS, 96), ((0, 0), (0, 0), (0, 32))).reshape(MLA_Q_LORA, 1024).astype(BF16)
    wukv = w_ukv.reshape(MLA_KV_LORA, MLA_HEADS, 128)
    wk = jnp.pad(wukv[:, :, :64], ((0, 0), (0, 0), (0, 64))).reshape(MLA_KV_LORA, 1024)
    wkv = jnp.concatenate([wk, wukv[:, :, 64:].reshape(MLA_KV_LORA, 512)], axis=1).astype(BF16)
    gcq = (g_cq * (MLA_SCALE * LOG2E)).reshape(1, -1)
    gq = (jnp.tile(g_q, GQA_HEADS) * (GQA_SCALE * LOG2E)).reshape(1, -1)
    gk = jnp.tile(g_k, 2).reshape(1, -1)
    bd = jnp.kron(jnp.eye(GQA_HEADS, dtype=F32), jnp.full((64, 64), 1.0 / 64, F32)).astype(BF16)
    nt = t // TM
    tok = lambda bb, i: (bb, i, 0)
    hm = lambda bb, i: (bb, 0, i, 0)
    return pl.pallas_call(
        functools.partial(_attn_in_kernel, nct=nct),
        grid=(b, nt),
        in_specs=_stream_specs(nct) + [
                  pl.BlockSpec((1, 1, 6, D), lambda bb, i: (bb, jnp.where(i < nct, 1, 0), 0, 0)),
                  _const_spec((1, D)), _const_spec((D, ATTN_IN_AUG)), _const_spec((1, MLA_Q_LORA)),
                  _const_spec((MLA_Q_LORA, 1024)), _const_spec((1, MLA_KV_LORA)), _const_spec((MLA_KV_LORA, 1536)),
                  _const_spec((1, 512)), _const_spec((1, 128)), _const_spec((512, 512)),
                  pl.BlockSpec((TM, 512), lambda bb, i: (i, 0))],
        out_specs=[pl.BlockSpec((1, 8, TM, 128), hm), pl.BlockSpec((1, 8, TM, 128), hm),
                   pl.BlockSpec((1, 8, TM, 128), hm), pl.BlockSpec((1, 8, TM, 128), hm),
                   pl.BlockSpec((1, TM, 128), tok), pl.BlockSpec((1, 4, TM, 128), hm)],
        out_shape=[jax.ShapeDtypeStruct((b, 8, t, 128), BF16), jax.ShapeDtypeStruct((b, 8, t, 128), BF16),
                   jax.ShapeDtypeStruct((b, 8, t, 128), BF16), jax.ShapeDtypeStruct((b, 8, t, 128), BF16),
                   jax.ShapeDtypeStruct((b, t, 128), BF16), jax.ShapeDtypeStruct((b, 4, t, 128), BF16)],
        compiler_params=_cparams(("arbitrary", "arbitrary")),
        name="attn_in",
    )(ctx, x, mods, ng.reshape(1, D), w_aug, gcq, wuq, g_ckv.reshape(1, -1), wkv, gq, gk, bd, tab)


def _softmax_pv(q, k, v, sum_lane):
    s = lax.dot_general(q, k, (((1,), (1,)), ((), ())), preferred_element_type=F32)
    m = jnp.max(s, axis=-1, keepdims=True)
    o = jnp.dot(jnp.exp2(s - m).astype(BF16), v, preferred_element_type=F32)
    return o / o[:, sum_lane:sum_lane + 1]


def _mla_kernel(q_ref, k_ref, v_ref, o_ref, *, nct, ctx_len):
    i = pl.program_id(1)
    lo = lax.broadcasted_iota(jnp.int32, (q_ref.shape[2], LANES), 1) < 64

    def run(nk):
        for pr in range(MLA_HEADS // 2):
            oa = _softmax_pv(q_ref[0, 2 * pr], k_ref[0, 2 * pr, 0:nk, :], v_ref[0, 2 * pr, 0:nk, :], 64)
            ob = _softmax_pv(q_ref[0, 2 * pr + 1], k_ref[0, 2 * pr + 1, 0:nk, :], v_ref[0, 2 * pr + 1, 0:nk, :], 0)
            o_ref[0, :, pr * 128:(pr + 1) * 128] = jnp.where(lo, oa, ob).astype(BF16)

    @pl.when(i < nct)
    def _():
        run(ctx_len)

    @pl.when(i >= nct)
    def _():
        run(k_ref.shape[2])


def _mla_attention(qm, km, vm, nct, ctx_len):
    b, _, t, _ = qm.shape
    nt = t // TM
    return pl.pallas_call(
        functools.partial(_mla_kernel, nct=nct, ctx_len=ctx_len),
        grid=(b, nt),
        in_specs=[pl.BlockSpec((1, 8, TM, 128), lambda bb, i: (bb, 0, i, 0)),
                  pl.BlockSpec((1, 8, t, 128), lambda bb, i: (bb, 0, 0, 0)),
                  pl.BlockSpec((1, 8, t, 128), lambda bb, i: (bb, 0, 0, 0))],
        out_specs=pl.BlockSpec((1, TM, 512), lambda bb, i: (bb, i, 0)),
        out_shape=jax.ShapeDtypeStruct((b, t, 512), BF16),
        compiler_params=_cparams(("arbitrary", "arbitrary")),
        name="mla_attention",
    )(qm, km, vm)


def _gqa_kernel(q_ref, k_ref, v_ref, o_ref, *, nct, ctx_len):
    i = pl.program_id(1)
    lo = lax.broadcasted_iota(jnp.int32, (q_ref.shape[2], LANES), 1) < 64

    def run(nk):
        k = k_ref[0, 0:nk, :]
        for g in range(2):
            ve = v_ref[0, 2 * g, 0:nk, :]
            vo = v_ref[0, 2 * g + 1, 0:nk, :]
            for pr in range(2):
                hd = 4 * g + 2 * pr
                oa = _softmax_pv(q_ref[0, hd], k, ve, 64)
                ob = _softmax_pv(q_ref[0, hd + 1], k, vo, 0)
                o_ref[0, :, (hd // 2) * 128:(hd // 2 + 1) * 128] = jnp.where(lo, oa, ob).astype(BF16)

    @pl.when(i < nct)
    def _():
        run(ctx_len)

    @pl.when(i >= nct)
    def _():
        run(k_ref.shape[1])


def _gqa_attention(qg, kg, vgs, nct, ctx_len):
    b, _, t, _ = qg.shape
    nt = t // TM
    return pl.pallas_call(
        functools.partial(_gqa_kernel, nct=nct, ctx_len=ctx_len),
        grid=(b, nt),
        in_specs=[pl.BlockSpec((1, 8, TM, 128), lambda bb, i: (bb, 0, i, 0)),
                  pl.BlockSpec((1, t, 128), lambda bb, i: (bb, 0, 0)),
                  pl.BlockSpec((1, 4, t, 128), lambda bb, i: (bb, 0, 0, 0))],
        out_specs=pl.BlockSpec((1, TM, 512), lambda bb, i: (bb, i, 0)),
        out_shape=jax.ShapeDtypeStruct((b, t, 512), BF16),
        compiler_params=_cparams(("arbitrary", "arbitrary")),
        name="gqa_attention",
    )(qg, kg, vgs)


def _pack_bf16_pairs(h):
    half = h.shape[1] // 2
    lo = pltpu.bitcast(h[:, :half].astype(BF16).astype(F32), jnp.uint32) >> 16
    hi = pltpu.bitcast(h[:, half:].astype(BF16).astype(F32), jnp.uint32) & jnp.uint32(0xFFFF0000)
    return lo | hi


def _unpack_bf16_pairs(w):
    return pltpu.bitcast(w << 16, F32), pltpu.bitcast(w & jnp.uint32(0xFFFF0000), F32)


def _store_planes(ref, words):
    ref[0, 0] = words[:, 0:SC_ROW_WORDS]
    ref[1, 0] = words[:, SC_ROW_WORDS:2 * SC_ROW_WORDS]


def _route_tile(lt, rt_ref, carry_ref, counted):
    n_e, rows = lt.shape
    eidx = lax.broadcasted_iota(jnp.int32, (n_e, rows), 0).astype(F32)
    work = lt
    vals, ids, hots = [], [], []
    for _ in range(TOP_K):
        m = jnp.max(work, axis=0, keepdims=True)
        idx = jnp.min(jnp.where(work == m, eidx, float(N_EXPERTS)), axis=0, keepdims=True)
        hot = eidx == idx
        vals.append(m)
        ids.append(idx)
        hots.append(hot)
        work = jnp.where(hot, -jnp.inf, work)
    exps = [jnp.exp(v - vals[0]) for v in vals]
    den = exps[0] + exps[1] + exps[2] + exps[3]
    mask = jnp.where(hots[0] | hots[1] | hots[2] | hots[3], 1.0, 0.0).astype(BF16)
    cum = jnp.dot(mask, rt_ref[...], preferred_element_type=F32)
    carry = carry_ref[...]
    before = cum[:, 0:rows] + jnp.concatenate([carry] * (rows // LANES), axis=1)
    ranks = [jnp.sum(jnp.where(h, before, 0.0), axis=0, keepdims=True) for h in hots]
    carry_ref[...] = carry + counted * cum[:, rows:rows + LANES]
    return jnp.concatenate(ids + ranks + [e / den for e in exps] + [jnp.zeros((ROUTE_LANES - 3 * TOP_K, rows), F32)],
                           axis=0)


def _residual_norm_router(x, o, mods_ref, g2_ref, wrh_ref, wrl_ref, br_ref, rt_ref, x1_ref, hp_ref, slab_ref,
                          rtab_ref, cnt_ref, carry_ref, *, nct, route_ctx):
    first = (pl.program_id(0) == 0) & (pl.program_id(1) == 0)

    @pl.when(first)
    def _():
        carry_ref[...] = jnp.zeros_like(carry_ref)

    g1 = mods_ref[0, 0, 2:3, :]
    sh2 = mods_ref[0, 0, 3:4, :]
    sc2 = mods_ref[0, 0, 4:5, :]
    x1 = x + g1 * o
    h2 = _rms(x1, g2_ref[...]) * (1.0 + sc2) + sh2
    x1_ref[0] = x1
    _store_planes(hp_ref, _pack_bf16_pairs(h2))
    hh = h2.astype(BF16)
    hl = (h2 - hh.astype(F32)).astype(BF16)
    logits = (jnp.dot(hh, wrh_ref[...], preferred_element_type=F32) + jnp.dot(hl, wrh_ref[...], preferred_element_type=F32)
              + jnp.dot(hh, wrl_ref[...], preferred_element_type=F32) + br_ref[...])
    counted = 1.0 if route_ctx else jnp.where(pl.program_id(1) >= nct, 1.0, 0.0)
    slab_t = _route_tile(logits.T[0:N_EXPERTS, :], rt_ref, carry_ref, counted)
    slab_ref[0] = slab_t.T
    rtab_ref[0] = slab_t[0:ROUTE_ROWS, :]
    cnt_ref[...] = carry_ref[...]


def _post_attn_kernel(ctx_ref, x_ref, om_ref, og_ref, wout_ref, mods_ref, g2_ref, wrh_ref, wrl_ref, br_ref, rt_ref,
                      x1_ref, hp_ref, slab_ref, rtab_ref, cnt_ref, carry_ref, *, nct):
    o = (jnp.dot(om_ref[0], wout_ref[0:512, :], preferred_element_type=F32)
         + jnp.dot(og_ref[0], wout_ref[512:1024, :], preferred_element_type=F32))
    _residual_norm_router(_stream_tile(ctx_ref, x_ref, nct), o, mods_ref, g2_ref, wrh_ref, wrl_ref, br_ref, rt_ref,
                          x1_ref, hp_ref, slab_ref, rtab_ref, cnt_ref, carry_ref, nct=nct, route_ctx=True)


def _post_out_specs(b, t):
    tok = lambda bb, i: (bb, i, 0)
    specs = [pl.BlockSpec((1, TM, D), tok), pl.BlockSpec((2, 1, TM, SC_ROW_WORDS), lambda bb, i: (0, bb, i, 0)),
             pl.BlockSpec((1, TM, ROUTE_LANES), tok), pl.BlockSpec((1, ROUTE_ROWS, TM), lambda bb, i: (bb, 0, i)),
             _const_spec((N_EXPERTS, LANES))]
    shapes = [jax.ShapeDtypeStruct((b, t, D), F32), jax.ShapeDtypeStruct((2, b, t, SC_ROW_WORDS), jnp.uint32),
              jax.ShapeDtypeStruct((b, t, ROUTE_LANES), F32), jax.ShapeDtypeStruct((b, ROUTE_ROWS, t), F32),
              jax.ShapeDtypeStruct((N_EXPERTS, LANES), F32)]
    return specs, shapes


def _router_operands(w_router, b_router):
    wp = jnp.pad(w_router, ((0, 0), (0, LANES - N_EXPERTS)))
    wrh = wp.astype(BF16)
    wrl = (wp - wrh.astype(F32)).astype(BF16)
    rt = jnp.concatenate([jnp.triu(jnp.ones((TM, TM), F32), 1), jnp.ones((TM, LANES), F32)], axis=1).astype(BF16)
    return wrh, wrl, jnp.pad(b_router, (0, LANES - N_EXPERTS)).reshape(1, LANES), rt


def _router_specs():
    return [_const_spec((D, LANES)), _const_spec((D, LANES)), _const_spec((1, LANES)), _const_spec((TM, TM + LANES))]


def _post_attn(ctx, x, o_m, o_g, w_out, mods, g2, w_router, b_router, nct):
    b, t, _ = o_m.shape
    tok = lambda bb, i: (bb, i, 0)
    out_specs, out_shape = _post_out_specs(b, t)
    return pl.pallas_call(
        functools.partial(_post_attn_kernel, nct=nct),
        grid=(b, t // TM),
        in_specs=_stream_specs(nct) + [
                  pl.BlockSpec((1, TM, 512), tok), pl.BlockSpec((1, TM, 512), tok),
                  _const_spec((1024, D)),
                  pl.BlockSpec((1, 1, 6, D), lambda bb, i: (bb, jnp.where(i < nct, 1, 0), 0, 0)),
                  _const_spec((1, D))] + _router_specs(),
        out_specs=out_specs, out_shape=out_shape,
        scratch_shapes=[pltpu.VMEM((N_EXPERTS, LANES), F32)],
        compiler_params=_cparams(("arbitrary", "arbitrary")),
        name="post_attn",
    )(ctx, x, o_m, o_g, w_out.astype(BF16), mods, g2.reshape(1, D), *_router_operands(w_router, b_router))


def _route_tables(rtab, counts, n_tiles):
    e = rtab[0:TOP_K].astype(jnp.int32)
    rank = rtab[TOP_K:2 * TOP_K].astype(jnp.int32)
    counts = counts[:, 0].astype(jnp.int32)
    padded = (counts + TMOE - 1) // TMOE * TMOE
    pend = jnp.cumsum(padded)
    pstart = pend - padded
    experts = jnp.arange(N_EXPERTS, dtype=jnp.int32)
    first_live = pend - counts
    pos = rank + jnp.sum(jnp.where(e[..., None] == experts, first_live, 0), axis=-1)
    tile_start = jnp.arange(n_tiles, dtype=jnp.int32) * TMOE
    tile_expert = jnp.minimum(jnp.sum(tile_start[:, None] >= pend[None, :], axis=-1), N_EXPERTS - 1).astype(jnp.int32)
    mine = tile_expert[:, None] == experts
    live = tile_start + TMOE - jnp.sum(jnp.where(mine, first_live, 0), axis=-1)
    tile_valid = jnp.where(tile_start < pend[-1], jnp.clip(live, 0, TMOE), 0).astype(jnp.int32)
    return pos, tile_expert, tile_valid


def _sc_mesh():
    return plsc.VectorSubcoreMesh(core_axis_name="core", subcore_axis_name="subcore")


def _sc_scatter_rows(src, idx, n_out, nb_tok, nb_src, nb_plane, off):
    m = idx.shape[0]
    steps_per_k = m // SC_WINDOW // TOP_K

    def src_block(i):
        q = i % steps_per_k
        plane = q // (steps_per_k // 2)
        r = q % (steps_per_k // 2)
        return (plane * nb_plane + (r // nb_tok) * nb_src + off + r % nb_tok, 0)

    @pl.kernel(out_type=jax.ShapeDtypeStruct((n_out, SC_ROW_WORDS), src.dtype), mesh=_sc_mesh(), scratch_types=[])
    def k(src_hbm, i_hbm, o_hbm):
        def body(x_vmem, i_vmem):
            pltpu.sync_copy(x_vmem, o_hbm.at[i_vmem.at[0]])

        pltpu.emit_pipeline(
            body,
            grid=(m // SC_WINDOW,),
            in_specs=[pl.BlockSpec((SC_WINDOW, SC_ROW_WORDS), src_block),
                      pl.BlockSpec((1, SC_WINDOW), lambda i: (0, i))],
            out_specs=[],
            core_axis_name=("core", "subcore"),
            dimension_semantics=(pltpu.PARALLEL,),
        )(src_hbm, i_hbm)

    return k(src, idx.reshape(1, m))


def _sc_gather_rows(src, idx):
    m = idx.shape[0]

    @pl.kernel(out_type=jax.ShapeDtypeStruct((m, SC_ROW_WORDS), src.dtype), mesh=_sc_mesh(), scratch_types=[])
    def k(src_hbm, i_hbm, o_hbm):
        def body(i_vmem, o_vmem):
            pltpu.sync_copy(src_hbm.at[i_vmem.at[0]], o_vmem)

        pltpu.emit_pipeline(
            body,
            grid=(m // SC_WINDOW,),
            in_specs=[pl.BlockSpec((1, SC_WINDOW), lambda i: (0, i))],
            out_specs=[pl.BlockSpec((SC_WINDOW, SC_ROW_WORDS), lambda i: (i, 0))],
            core_axis_name=("core", "subcore"),
            dimension_semantics=(pltpu.PARALLEL,),
        )(i_hbm, o_hbm)

    return k(src, idx.reshape(1, m))


def _moe_kernel(te_ref, tv_ref, xs_ref, wgu_ref, bgu_ref, wd_ref, bd_ref, ys_ref, wgu_bf, wd_bf):
    t = pl.program_id(0)
    prev = te_ref[jnp.maximum(t - 1, 0)]

    @pl.when((t == 0) | (te_ref[t] != prev))
    def _():
        wgu_bf[...] = wgu_ref[0, 0].astype(BF16)
        wd_bf[...] = wd_ref[0, 0].astype(BF16)

    valid = tv_ref[t]
    half = TMOE // 2

    def experts(rows):
        r0 = TMOE - rows
        lo, hi = _unpack_bf16_pairs(jnp.concatenate([xs_ref[0, r0:TMOE, :], xs_ref[1, r0:TMOE, :]], axis=1))
        live = lax.broadcasted_iota(jnp.int32, (rows, 1), 0) >= rows - valid
        x = jnp.where(live, jnp.concatenate([lo, hi], axis=1), 0.0).astype(BF16)
        gu = jnp.dot(x, wgu_bf[...], preferred_element_type=F32) + bgu_ref[0, 0]
        gate = jnp.minimum(gu[:, :MOE_FF], SWIGLU_LIMIT)
        up = jnp.clip(gu[:, MOE_FF:], -SWIGLU_LIMIT, SWIGLU_LIMIT)
        act = (up + 1.0) * (gate * _sigmoid(SWIGLU_ALPHA * gate))
        y = jnp.dot(act.astype(BF16), wd_bf[...], preferred_element_type=F32) + bd_ref[0, 0]
        words = _pack_bf16_pairs(y)
        ys_ref[0, r0:TMOE, :] = words[:, 0:SC_ROW_WORDS]
        ys_ref[1, r0:TMOE, :] = words[:, SC_ROW_WORDS:2 * SC_ROW_WORDS]

    @pl.when(valid > half)
    def _():
        experts(TMOE)

    @pl.when((valid > 0) & (valid <= half))
    def _():
        experts(half)
        ys_ref[:, 0:half, :] = jnp.zeros((2, half, SC_ROW_WORDS), jnp.uint32)

    @pl.when(valid == 0)
    def _():
        ys_ref[...] = jnp.zeros_like(ys_ref)


def _moe_experts(xs, tile_expert, tile_valid, layer, w_gu, b_gu, w_d, b_d):
    n_rows = xs.shape[1]
    n_tiles = n_rows // TMOE
    depth = w_gu.shape[0]
    grid_spec = pltpu.PrefetchScalarGridSpec(
        num_scalar_prefetch=2,
        grid=(n_tiles,),
        in_specs=[pl.BlockSpec((2, TMOE, SC_ROW_WORDS), lambda t, te, tv: (0, t, 0)),
                  pl.BlockSpec((1, 1, D, 2 * MOE_FF), lambda t, te, tv: (layer, te[t], 0, 0)),
                  pl.BlockSpec((1, 1, 1, 2 * MOE_FF), lambda t, te, tv: (layer, te[t], 0, 0)),
                  pl.BlockSpec((1, 1, MOE_FF, D), lambda t, te, tv: (layer, te[t], 0, 0)),
                  pl.BlockSpec((1, 1, 1, D), lambda t, te, tv: (layer, te[t], 0, 0))],
        out_specs=pl.BlockSpec((2, TMOE, SC_ROW_WORDS), lambda t, te, tv: (0, t, 0)),
        scratch_shapes=[pltpu.VMEM((D, 2 * MOE_FF), BF16), pltpu.VMEM((MOE_FF, D), BF16)],
    )
    return pl.pallas_call(
        _moe_kernel,
        grid_spec=grid_spec,
        out_shape=jax.ShapeDtypeStruct((2, n_rows, SC_ROW_WORDS), jnp.uint32),
        compiler_params=_cparams(("arbitrary",)),
        name="moe_experts",
    )(tile_expert, tile_valid, xs, w_gu, b_gu.reshape(depth, N_EXPERTS, 1, -1), w_d, b_d.reshape(depth, N_EXPERTS, 1, -1))


def _moe(hp, rtab, counts, layer, w_gu, b_gu, w_d, b_d, n_seq, row0):
    _, b, t, _ = hp.shape
    n_tok = b * n_seq
    n_tiles = -(-n_tok * TOP_K // TMOE) + N_EXPERTS
    n_rows = n_tiles * TMOE
    rtab = jnp.transpose(rtab[:, :, row0:row0 + n_seq], (1, 0, 2)).reshape(ROUTE_ROWS, n_tok)
    pos, tile_expert, tile_valid = _route_tables(rtab, counts, n_tiles)
    idx = (pos[:, None, :] + (jnp.arange(2, dtype=jnp.int32) * n_rows)[None, :, None]).reshape(-1)
    xs = _sc_scatter_rows(hp.reshape(2 * b * t, SC_ROW_WORDS), idx, 2 * n_rows, n_seq // SC_WINDOW, t // SC_WINDOW,
                          b * t // SC_WINDOW, row0 // SC_WINDOW)
    ys = _moe_experts(xs.reshape(2, n_rows, SC_ROW_WORDS), tile_expert, tile_valid, layer, w_gu, b_gu, w_d, b_d)
    g = _sc_gather_rows(ys.reshape(2 * n_rows, SC_ROW_WORDS), idx)
    return g.reshape(TOP_K, 2, b, n_seq, SC_ROW_WORDS)


def _combine(g_ref, slab, rows=slice(None)):
    acc_lo = acc_hi = None
    for k in range(TOP_K):
        lo, hi = _unpack_bf16_pairs(jnp.concatenate([g_ref[k, 0, 0, rows, :], g_ref[k, 1, 0, rows, :]], axis=1))
        w = slab[:, 2 * TOP_K + k:2 * TOP_K + k + 1]
        acc_lo = w * lo if acc_lo is None else acc_lo + w * lo
        acc_hi = w * hi if acc_hi is None else acc_hi + w * hi
    return jnp.concatenate([acc_lo, acc_hi], axis=1)


def _ssm_in_kernel(x_ref, g_ref, slab_ref, modsp_ref, mods_ref, ng_ref, w_ref, x1_ref, z_ref, xbc_ref, dt_ref, v_ref,
                   wbf_ref):
    hi = SSM_D + SSM_XBC + SSM_HEADS

    @pl.when((pl.program_id(0) == 0) & (pl.program_id(1) == 0))
    def _():
        rows = 128
        for r0 in range(0, D, rows):
            r = slice(r0, r0 + rows)
            wbf_ref[r, 0:hi] = w_ref[r, 0:hi].astype(BF16)
            wbf_ref[r, hi:hi + LANES - SSM_HEADS] = jnp.zeros((rows, LANES - SSM_HEADS), BF16)
            wbf_ref[r, hi + LANES - SSM_HEADS:SSM_IN_AUG] = w_ref[r, hi:w_ref.shape[1]].astype(BF16)

    sh = mods_ref[0, 0, 0:1, :]
    sc = mods_ref[0, 0, 1:2, :]
    gate = modsp_ref[0, 0, 5:6, :]
    half = x_ref.shape[1] // 2
    for p in range(2):
        r = slice(p * half, (p + 1) * half)
        x = x_ref[0, r, :] + gate * _combine(g_ref, slab_ref[0, r, :], r)
        x1_ref[0, r, :] = x
        h = _rms(x, ng_ref[...]) * (1.0 + sc) + sh
        u = jnp.dot(h.astype(BF16), wbf_ref[...], preferred_element_type=F32)
        z_ref[0, r, :] = u[:, 0:1024]
        xbc_ref[0, r, :] = u[:, 1024:2560]
        dt_ref[0, r, :] = u[:, 2560:2688]
        v_ref[0, r, :] = u[:, 2688:3712] * _sigmoid(u[:, 3712:4736])


def _ssm_in(x_all, g_all, slab, mods_prev, mods, ng, w_in, nct):
    b, t, _ = x_all.shape
    tok = lambda bb, i: (bb, i, 0)
    modspec = pl.BlockSpec((1, 1, 6, D), lambda bb, i: (bb, jnp.where(i < nct, 1, 0), 0, 0))
    return pl.pallas_call(
        _ssm_in_kernel,
        grid=(b, t // TM),
        in_specs=[pl.BlockSpec((1, TM, D), tok),
                  pl.BlockSpec((TOP_K, 2, 1, TM, SC_ROW_WORDS), lambda bb, i: (0, 0, bb, i, 0)),
                  pl.BlockSpec((1, TM, ROUTE_LANES), tok), modspec, modspec,
                  _const_spec((1, D)),
                  pl.BlockSpec(w_in.shape, lambda bb, i: (0, 0), pipeline_mode=pl.Buffered(1))],
        scratch_shapes=[pltpu.VMEM((D, SSM_IN_AUG), BF16)],
        out_specs=[pl.BlockSpec((1, TM, D), tok), pl.BlockSpec((1, TM, 1024), tok), pl.BlockSpec((1, TM, SSM_XBC), tok),
                   pl.BlockSpec((1, TM, 128), tok), pl.BlockSpec((1, TM, 1024), tok)],
        out_shape=[jax.ShapeDtypeStruct((b, t, D), F32), jax.ShapeDtypeStruct((b, t, 1024), F32),
                   jax.ShapeDtypeStruct((b, t, SSM_XBC), F32), jax.ShapeDtypeStruct((b, t, 128), F32),
                   jax.ShapeDtypeStruct((b, t, 1024), F32)],
        compiler_params=_cparams(("arbitrary", "arbitrary")),
        name="ssm_in",
    )(x_all, g_all, slab, mods_prev, mods, ng.reshape(1, D), w_in)


def _conv_kernel(x_ref, w_ref, b_ref, o_ref, pad_ref, *, taps, ctx_len, silu):
    t = x_ref.shape[1]
    ct = x_ref.shape[2]
    half = taps // 2
    zeros = jnp.zeros((HALO, ct), F32)
    pad_ref[0:HALO, :] = zeros
    pad_ref[HALO:HALO + ctx_len, :] = x_ref[0, 0:ctx_len, :]
    pad_ref[HALO + ctx_len:2 * HALO + ctx_len, :] = zeros
    pad_ref[2 * HALO + ctx_len:2 * HALO + t, :] = x_ref[0, ctx_len:t, :]
    pad_ref[2 * HALO + t:3 * HALO + t, :] = zeros
    w = w_ref[...]
    bias = b_ref[...]
    rows = CHUNK + 2 * HALO

    def segment(out_start, length, pad_start):
        def body(c, carry):
            base = pl.multiple_of(c * CHUNK, CHUNK)
            win = pad_ref[pl.ds(pad_start - HALO + base, rows), :]
            acc = jnp.broadcast_to(bias, (CHUNK, ct))
            for r in range(8):
                ks = [k for k in range(taps) if (HALO + k - half) % 8 == r]
                if not ks:
                    continue
                rolled = win if r == 0 else pltpu.roll(win, rows - r, 0)
                for k in ks:
                    off = HALO + k - half - r
                    acc = acc + w[k:k + 1, :] * rolled[off:off + CHUNK, :]
            if silu:
                acc = acc * _sigmoid(acc)
            o_ref[0, pl.ds(out_start + base, CHUNK), :] = acc
            return carry
        lax.fori_loop(0, length // CHUNK, body, 0)

    segment(0, ctx_len, HALO)
    segment(ctx_len, t - ctx_len, 2 * HALO + ctx_len)


def _depthwise_conv(x, w, bias, ctx_len, silu):
    b, t, c = x.shape
    taps = w.shape[0]
    ct = 256
    return pl.pallas_call(
        functools.partial(_conv_kernel, taps=taps, ctx_len=ctx_len, silu=silu),
        grid=(b, c // ct),
        in_specs=[pl.BlockSpec((1, t, ct), lambda bb, j: (bb, 0, j)),
                  pl.BlockSpec((taps, ct), lambda bb, j: (0, j)),
                  pl.BlockSpec((1, ct), lambda bb, j: (0, j))],
        out_specs=pl.BlockSpec((1, t, ct), lambda bb, j: (bb, 0, j)),
        out_shape=jax.ShapeDtypeStruct((b, t, c), F32),
        scratch_shapes=[pltpu.VMEM((t + 3 * HALO, ct), F32)],
        compiler_params=_cparams(("arbitrary", "arbitrary")),
        name=f"depthwise_conv{taps}",
    )(x, w, bias.reshape(1, c))


def _ssd_chunk(xbc, dt_raw, a_row, bias_row, emat, state_ref, d, reverse):
    L = CHUNK
    ri = lax.broadcasted_iota(jnp.int32, (L, L), 0)
    ci = lax.broadcasted_iota(jnp.int32, (L, L), 1)
    lane_lo = lax.broadcasted_iota(jnp.int32, (L, LANES), 1) < 64
    mask = (ci >= ri) if reverse else (ri >= ci)
    tri = mask.astype(F32)

    xdt_in = dt_raw + bias_row
    dt = jnp.maximum(xdt_in, 0.0) + jnp.log1p(jnp.exp(-jnp.abs(xdt_in)))
    da = dt * a_row
    cs = jnp.dot(tri, da, precision=HIGHEST, preferred_element_type=F32)
    cs_t = cs.T
    end = 0 if reverse else L - 1
    a_end = cs[end:end + 1, :]
    dth = dt.astype(BF16)
    dtl = (dt - dth.astype(F32)).astype(BF16)
    dtx = jnp.dot(dth, emat, preferred_element_type=F32) + jnp.dot(dtl, emat, preferred_element_type=F32)

    def lanes(v, hh):
        return jnp.broadcast_to(v[:, hh:hh + 1], (v.shape[0], LANES))

    ys = []
    for g in range(2):
        bm = xbc[:, SSM_D + g * SSM_STATE:SSM_D + (g + 1) * SSM_STATE].astype(BF16)
        cm = xbc[:, SSM_D + 2 * SSM_STATE + g * SSM_STATE:SSM_D + 2 * SSM_STATE + (g + 1) * SSM_STATE].astype(BF16)
        cb = lax.dot_general(cm, bm, (((1,), (1,)), ((), ())), preferred_element_type=F32)
        state = state_ref[d, g]
        y_off = jnp.dot(cm, state.astype(BF16), preferred_element_type=F32)
        xdd_blocks, sdec_blocks = [], []
        for p in range(4):
            hp = g * 4 + p
            gmats, e_cols, d_outs = [], [], []
            for j in range(2):
                hh = 2 * hp + j
                col = lanes(cs, hh)
                row = jnp.broadcast_to(cs_t[hh:hh + 1, :], (L, L))
                dec = jnp.exp(jnp.where(mask, col - row, -jnp.inf))
                gmats.append((cb * dec).astype(BF16))
                e_cols.append(jnp.exp(col))
                d_outs.append(jnp.exp(lanes(a_end, hh) - col))
            din = jnp.where(lane_lo, e_cols[0], e_cols[1])
            dout = jnp.where(lane_lo, d_outs[0], d_outs[1])
            xdt = xbc[:, hp * 128:(hp + 1) * 128] * dtx[:, hp * 128:(hp + 1) * 128]
            xdd_blocks.append((xdt * dout).astype(BF16))
            sdec_blocks.append(din[end:end + 1, :])
            xdt = xdt.astype(BF16)
            ya = jnp.dot(gmats[0], xdt, preferred_element_type=F32)
            yb = jnp.dot(gmats[1], xdt, preferred_element_type=F32)
            ys.append(jnp.where(lane_lo, ya, yb) + din * y_off[:, p * 128:(p + 1) * 128])
        upd = lax.dot_general(bm, jnp.concatenate(xdd_blocks, axis=1), (((0,), (0,)), ((), ())),
                              preferred_element_type=F32)
        state_ref[d, g] = state * jnp.concatenate(sdec_blocks, axis=1) + upd
    return jnp.concatenate(ys, axis=1)


def _ssd_kernel(xf_ref, xb_ref, dtf_ref, dtb_ref, alog_ref, bias_ref, emat_ref, yf_ref, yb_ref, state_ref):
    @pl.when(pl.program_id(1) == 0)
    def _():
        state_ref[...] = jnp.zeros_like(state_ref)

    a = -jnp.exp(alog_ref[...])
    bias = bias_ref[...]
    emat = emat_ref[...]
    yf_ref[0] = _ssd_chunk(xf_ref[0], dtf_ref[0], a[0:1, :], bias[0:1, :], emat, state_ref, 0, False)
    yb_ref[0] = _ssd_chunk(xb_ref[0], dtb_ref[0], a[1:2, :], bias[1:2, :], emat, state_ref, 1, True)


def _ssd(xbc, dt, a_log, dt_bias, ctx_len):
    b, t, _ = xbc.shape
    nc = t // CHUNK
    ncc = ctx_len // CHUNK

    def fwd(bb, j):
        return (bb, j, 0)

    def bwd(bb, j):
        return (bb, jnp.where(j < ncc, ncc - 1 - j, nc - 1 + ncc - j), 0)

    pad = lambda v: jnp.pad(v, ((0, 0), (0, 128 - SSM_HEADS)))
    emat = (jnp.arange(LANES)[:, None] == jnp.arange(SSM_D)[None, :] // SSM_HEAD_DIM).astype(BF16)
    return pl.pallas_call(
        _ssd_kernel,
        grid=(b, nc),
        in_specs=[pl.BlockSpec((1, CHUNK, SSM_XBC), fwd), pl.BlockSpec((1, CHUNK, SSM_XBC), bwd),
                  pl.BlockSpec((1, CHUNK, 128), fwd), pl.BlockSpec((1, CHUNK, 128), bwd),
                  _const_spec((2, 128)), _const_spec((2, 128)), _const_spec((LANES, SSM_D))],
        out_specs=[pl.BlockSpec((1, CHUNK, SSM_D), fwd), pl.BlockSpec((1, CHUNK, SSM_D), bwd)],
        out_shape=[jax.ShapeDtypeStruct((b, t, SSM_D), F32), jax.ShapeDtypeStruct((b, t, SSM_D), F32)],
        scratch_shapes=[pltpu.VMEM((2, 2, SSM_STATE, 512), F32)],
        compiler_params=_cparams(("arbitrary", "arbitrary")),
        name="ssd_scan",
    )(xbc, xbc, dt, dt, pad(a_log), pad(dt_bias), emat)


def _post_ssm_kernel(x_ref, yf_ref, yb_ref, xs_ref, z_ref, v_ref, dsk_ref, sg_ref, lng_ref, lnb_ref, wout_ref,
                     mods_ref, g2_ref, wrh_ref, wrl_ref, br_ref, rt_ref, x1_ref, hp_ref, slab_ref, rtab_ref, cnt_ref,
                     carry_ref, wbf_ref, *, nct):
    @pl.when((pl.program_id(0) == 0) & (pl.program_id(1) == 0))
    def _():
        for r0 in range(0, wbf_ref.shape[0], 256):
            wbf_ref[r0:r0 + 256, :] = wout_ref[r0:r0 + 256, :].astype(BF16)

    y = yf_ref[0] + yb_ref[0] + dsk_ref[...] * xs_ref[0]
    z = z_ref[0]
    y_ssm = _rms(y * (z * _sigmoid(z)), sg_ref[...])
    v = v_ref[0]
    mu = jnp.mean(v, axis=-1, keepdims=True)
    vc = v - mu
    ln = vc * lax.rsqrt(jnp.mean(vc * vc, axis=-1, keepdims=True) + EPS) * lng_ref[...] + lnb_ref[...]
    y_conv = ln * _sigmoid(ln)
    o = (jnp.dot(y_ssm.astype(BF16), wbf_ref[0:1024, :], preferred_element_type=F32)
         + jnp.dot(y_conv.astype(BF16), wbf_ref[1024:2048, :], preferred_element_type=F32))
    _residual_norm_router(x_ref[0], o, mods_ref, g2_ref, wrh_ref, wrl_ref, br_ref, rt_ref, x1_ref, hp_ref, slab_ref,
                          rtab_ref, cnt_ref, carry_ref, nct=nct, route_ctx=False)


def _post_ssm(x_all, yf, yb, xbc_act, z, v, d_skip, ssm_norm_g, ln_g, ln_b, w_out, mods, g2, w_router, b_router, nct):
    b, t, _ = x_all.shape
    tok = lambda bb, i: (bb, i, 0)
    dsk = jnp.repeat(d_skip[0] + d_skip[1], SSM_HEAD_DIM).reshape(1, SSM_D)
    out_specs, out_shape = _post_out_specs(b, t)
    row = lambda: pl.BlockSpec((1, TM, 1024), tok)
    return pl.pallas_call(
        functools.partial(_post_ssm_kernel, nct=nct),
        grid=(b, t // TM),
        in_specs=[row(), row(), row(), row(), row(), row(),
                  _const_spec((1, SSM_D)), _const_spec((1, SSM_D)), _const_spec((1, D)), _const_spec((1, D)),
                  pl.BlockSpec(w_out.shape, lambda bb, i: (0, 0), pipeline_mode=pl.Buffered(1)),
                  pl.BlockSpec((1, 1, 6, D), lambda bb, i: (bb, jnp.where(i < nct, 1, 0), 0, 0)),
                  _const_spec((1, D))] + _router_specs(),
        out_specs=out_specs, out_shape=out_shape,
        scratch_shapes=[pltpu.VMEM((N_EXPERTS, LANES), F32), pltpu.VMEM(w_out.shape, BF16)],
        compiler_params=_cparams(("arbitrary", "arbitrary")),
        name="post_ssm",
    )(x_all, yf, yb, xbc_act, z, v, dsk, ssm_norm_g.reshape(1, -1), ln_g.reshape(1, -1), ln_b.reshape(1, -1),
      w_out, mods, g2.reshape(1, D), *_router_operands(w_router, b_router))


def _final_kernel(x_ref, g_ref, slab_ref, mods_ref, fg_ref, o_ref):
    x = x_ref[0] + mods_ref[0, 0, 5:6, :] * _combine(g_ref, slab_ref[0])
    o_ref[0] = _rms(x, fg_ref[...])


def _final(x_all, g_lat, slab, mods, final_g, nct):
    b, s = g_lat.shape[2], g_lat.shape[3]
    return pl.pallas_call(
        _final_kernel,
        grid=(b, s // TM),
        in_specs=[pl.BlockSpec((1, TM, D), lambda bb, i: (bb, i + nct, 0)),
                  pl.BlockSpec((TOP_K, 2, 1, TM, SC_ROW_WORDS), lambda bb, i: (0, 0, bb, i, 0)),
                  pl.BlockSpec((1, TM, ROUTE_LANES), lambda bb, i: (bb, i + nct, 0)),
                  pl.BlockSpec((1, 1, 6, D), lambda bb, i: (bb, 0, 0, 0)),
                  _const_spec((1, D))],
        out_specs=pl.BlockSpec((1, TM, D), lambda bb, i: (bb, i, 0)),
        out_shape=jax.ShapeDtypeStruct((b, s, D), F32),
        compiler_params=_cparams(("arbitrary", "arbitrary")),
        name="final_norm",
    )(x_all, g_lat, slab, mods, final_g.reshape(1, D))


def kernel(x, c, ctx, c_ctx, w_mod, b_mod, norm_g, attn_w_in, mla_g_cq, mla_w_uq, mla_g_ckv, mla_w_ukv, gqa_g_q, gqa_g_k, attn_w_out, ssm_w_in, ssm_conv_w, ssm_conv_b, ssm_a_log, ssm_dt_bias, ssm_d, ssm_norm_g, conf_dw_w, conf_dw_b, conf_ln_g, conf_ln_b, ssm_w_out, moe_w_router, moe_b_router, moe_w_gate_up, moe_b_gate_up, moe_w_down, moe_b_down, final_g):
    b, s, _ = x.shape
    ctx_len = ctx.shape[1]
    t = ctx_len + s
    assert ctx_len % TM == 0 and s % TM == 0 and s % GRID_W == 0
    nct = ctx_len // TM

    c_rows = jnp.concatenate([c, c_ctx[None, :], jnp.zeros((-(b + 1) % 8, D), F32)], axis=0)
    mod_all = _modulations(c_rows, w_mod, b_mod)
    mods = []
    for i in range(w_mod.shape[0]):
        lat = mod_all[i, :b].reshape(b, 1, 6, D)
        cm = jnp.broadcast_to(mod_all[i, b].reshape(1, 1, 6, D), (b, 1, 6, D))
        mods.append(jnp.concatenate([lat, cm], axis=1))

    tab = _rope_tables(s, ctx_len)
    qm, km, vm, qg, kg, vgs = _attn_in(ctx, x, mods[0], norm_g[0, 0], attn_w_in[0], mla_g_cq[0], mla_w_uq[0], mla_g_ckv[0],
                                       mla_w_ukv[0], gqa_g_q[0], gqa_g_k[0], tab, nct)
    o_m = _mla_attention(qm, km, vm, nct, ctx_len)
    o_g = _gqa_attention(qg, kg, vgs, nct, ctx_len)
    x_all, hp, slab0, rtab, counts = _post_attn(ctx, x, o_m, o_g, attn_w_out[0], mods[0], norm_g[0, 1], moe_w_router[0],
                                                moe_b_router[0], nct)
    g_all = _moe(hp, rtab, counts, 0, moe_w_gate_up, moe_b_gate_up, moe_w_down, moe_b_down, t, 0)

    x_all, z, xbc, dt, v = _ssm_in(x_all, g_all, slab0, mods[0], mods[1], norm_g[1, 0], ssm_w_in[0], nct)
    xbc_act = _depthwise_conv(xbc, ssm_conv_w[0], ssm_conv_b[0], ctx_len, True)
    v_conv = _depthwise_conv(v, conf_dw_w[0], conf_dw_b[0], ctx_len, False)
    yf, yb = _ssd(xbc_act, dt, ssm_a_log[0], ssm_dt_bias[0], ctx_len)
    x_all, hp, slab1, rtab, counts = _post_ssm(x_all, yf, yb, xbc_act, z, v_conv, ssm_d[0], ssm_norm_g[0], conf_ln_g[0],
                                               conf_ln_b[0], ssm_w_out[0], mods[1], norm_g[1, 1], moe_w_router[1],
                                               moe_b_router[1], nct)
    g_lat = _moe(hp, rtab, counts, 1, moe_w_gate_up, moe_b_gate_up, moe_w_down, moe_b_down, s, ctx_len)
    return _final(x_all, g_lat, slab1, mods[1], final_g, nct)
```

```python
import functools

import jax
import jax.numpy as jnp
from jax import lax
from jax.experimental import pallas as pl
from jax.experimental.pallas import tpu as pltpu
from jax.experimental.pallas import tpu_sc as plsc

F32 = jnp.float32
BF16 = jnp.bfloat16
HIGHEST = lax.Precision.HIGHEST

D = 1024
EPS = 1e-6
GRID_W = 64
ROPE_THETA = 10000.0
LOG2E = 1.4426950408889634

MLA_HEADS = 8
MLA_NOPE = 64
MLA_ROPE = 32
MLA_Q_LORA = 256
MLA_KV_LORA = 128
MLA_SCALE = (MLA_NOPE + MLA_ROPE) ** -0.5
GQA_HEADS = 8
GQA_HEAD_DIM = 64
GQA_SCALE = GQA_HEAD_DIM ** -0.5
ATTN_IN_AUG = 1280

SSM_HEADS = 16
SSM_HEAD_DIM = 64
SSM_STATE = 128
SSM_D = 1024
SSM_XBC = 1536
SSM_CONV = 5
CONF_K = 31
CHUNK = 128
SSM_IN_AUG = 1024 + 1536 + 128 + 2048

N_EXPERTS = 32
TOP_K = 4
MOE_FF = 1024
SWIGLU_LIMIT = 7.0
SWIGLU_ALPHA = 1.702

TM = 256
TMOE = 512
ROUTE_LANES = 128
ROUTE_ROWS = 16
SC_WINDOW = 128
SC_ROW_WORDS = 256
LANES = 128
HALO = 16
VMEM_LIMIT = 56 * 1024 * 1024


def _cparams(sem):
    return pltpu.CompilerParams(dimension_semantics=sem, vmem_limit_bytes=VMEM_LIMIT)


def _rms(x, g):
    return x * lax.rsqrt(jnp.mean(x * x, axis=-1, keepdims=True) + EPS) * g


def _sigmoid(x):
    return 1.0 / (1.0 + jnp.exp(-x))


def _const_spec(shape):
    n = len(shape)
    return pl.BlockSpec(shape, lambda *_: (0,) * n)


def _mod_kernel(c_ref, w_ref, b_ref, o_ref):
    c = c_ref[...]
    o_ref[0] = jnp.dot(c * _sigmoid(c), w_ref[0], precision=HIGHEST, preferred_element_type=F32) + b_ref[0]


def _modulations(c_rows, w_mod, b_mod):
    depth, _, n = w_mod.shape
    tn = 512
    r = c_rows.shape[0]
    return pl.pallas_call(
        _mod_kernel,
        grid=(depth, n // tn),
        in_specs=[pl.BlockSpec((r, D), lambda l, j: (0, 0)),
                  pl.BlockSpec((1, D, tn), lambda l, j: (l, 0, j)),
                  pl.BlockSpec((1, 1, tn), lambda l, j: (l, 0, j))],
        out_specs=pl.BlockSpec((1, r, tn), lambda l, j: (l, 0, j)),
        out_shape=jax.ShapeDtypeStruct((depth, r, n), F32),
        compiler_params=_cparams(("arbitrary", "arbitrary")),
        name="modulations",
    )(c_rows, w_mod, b_mod.reshape(depth, 1, n))


def _stream_tile(ctx_ref, x_ref, nct):
    return jnp.where(pl.program_id(1) < nct, ctx_ref[0], x_ref[0])


def _stream_specs(nct):
    return [pl.BlockSpec((1, TM, D), lambda bb, i: (bb, jnp.minimum(i, nct - 1), 0)),
            pl.BlockSpec((1, TM, D), lambda bb, i: (bb, jnp.maximum(i - nct, 0), 0))]


def _attn_in_kernel(ctx_ref, x_ref, mods_ref, ng_ref, win_ref, gcq_ref, wuq_ref, gckv_ref, wkv_ref, gq_ref, gk_ref, bd_ref,
                    tab_ref, qm_ref, km_ref, vm_ref, qg_ref, kg_ref, vgs_ref, *, nct):
    x = _stream_tile(ctx_ref, x_ref, nct)
    sh = mods_ref[0, 0, 0:1, :]
    sc = mods_ref[0, 0, 1:2, :]
    h = _rms(x, ng_ref[...]) * (1.0 + sc) + sh
    u = jnp.dot(h.astype(BF16), win_ref[...], preferred_element_type=F32)

    tab = tab_ref[...]
    cg, sg = tab[:, 0:128], tab[:, 128:256]
    cm, sm = tab[:, 256:384], tab[:, 384:512]
    lane = lax.broadcasted_iota(jnp.int32, (x.shape[0], LANES), 1)
    even = (lane & 1) == 0
    lo = lane < 64

    def rope(v, c, s):
        partner = jnp.where(even, pltpu.roll(v, LANES - 1, 1), pltpu.roll(v, 1, 1))
        return v * c + partner * s

    cq = _rms(u[:, 0:256], gcq_ref[...])
    qm = jnp.dot(cq.astype(BF16), wuq_ref[...], preferred_element_type=F32)
    for hd in range(MLA_HEADS):
        qm_ref[0, hd] = rope(qm[:, hd * 128:(hd + 1) * 128], cm, sm).astype(BF16)
    ckv = _rms(u[:, 256:384], gckv_ref[...])
    kv = jnp.dot(ckv.astype(BF16), wkv_ref[...], preferred_element_type=F32)
    kr = rope(u[:, 384:512], cm, sm)
    for hd in range(MLA_HEADS):
        km_ref[0, hd] = (kv[:, hd * 128:(hd + 1) * 128] + kr).astype(BF16)
    one = jnp.ones((x.shape[0], LANES), F32)
    for p in range(MLA_HEADS // 2):
        blk = kv[:, 1024 + p * 128:1024 + (p + 1) * 128]
        vm_ref[0, 2 * p] = jnp.where(lo, blk, one).astype(BF16)
        vm_ref[0, 2 * p + 1] = jnp.where(lo, one, blk).astype(BF16)

    qg = u[:, 512:1024]
    ms = jnp.dot((qg * qg).astype(BF16), bd_ref[...], preferred_element_type=F32)
    qg = qg * lax.rsqrt(ms + EPS) * gq_ref[...]
    zero = jnp.zeros((x.shape[0], LANES), F32)
    for p in range(GQA_HEADS // 2):
        blk = rope(qg[:, p * 128:(p + 1) * 128], cg, sg)
        swp = pltpu.roll(blk, 64, 1)
        if p < 2:
            qg_ref[0, 2 * p] = jnp.where(lo, blk, zero).astype(BF16)
            qg_ref[0, 2 * p + 1] = jnp.where(lo, swp, zero).astype(BF16)
        else:
            qg_ref[0, 2 * p] = jnp.where(lo, zero, swp).astype(BF16)
            qg_ref[0, 2 * p + 1] = jnp.where(lo, zero, blk).astype(BF16)
    kg = u[:, 1024:1152]
    msk = jnp.dot((kg * kg).astype(BF16), bd_ref[0:128, 0:128], preferred_element_type=F32)
    kg_ref[0] = rope(kg * lax.rsqrt(msk + EPS) * gk_ref[...], cg, sg).astype(BF16)
    vg = u[:, 1152:1280]
    vsw = pltpu.roll(vg, 64, 1)
    vgs_ref[0, 0] = jnp.where(lo, vg, one).astype(BF16)
    vgs_ref[0, 1] = jnp.where(lo, one, vsw).astype(BF16)
    vgs_ref[0, 2] = jnp.where(lo, vsw, one).astype(BF16)
    vgs_ref[0, 3] = jnp.where(lo, one, vg).astype(BF16)


def _rope_tables(seq, ctx_len):
    rows = seq // GRID_W
    row = jnp.broadcast_to(jnp.arange(rows, dtype=F32)[:, None], (rows, GRID_W)).reshape(-1)
    col = jnp.broadcast_to(jnp.arange(GRID_W, dtype=F32)[None, :], (rows, GRID_W)).reshape(-1)

    def interleaved(rot_dim):
        n_freq = rot_dim // 4
        inv_freq = ROPE_THETA ** (-jnp.arange(n_freq, dtype=F32) / n_freq)
        ang = jnp.concatenate([row[:, None] * inv_freq, col[:, None] * inv_freq], axis=-1)
        cos = jnp.repeat(jnp.cos(ang), 2, axis=-1)
        sin = jnp.repeat(jnp.sin(ang), 2, axis=-1) * jnp.tile(jnp.array([-1.0, 1.0], F32), rot_dim // 2)
        return cos, sin

    cg, sg = interleaved(GQA_HEAD_DIM)
    cg, sg = jnp.tile(cg, (1, 2)), jnp.tile(sg, (1, 2))
    cm32, sm32 = interleaved(MLA_ROPE)
    ones, zeros = jnp.ones((seq, 64), F32), jnp.zeros((seq, 64), F32)
    cm = jnp.concatenate([ones, cm32, ones[:, :32]], axis=-1)
    sm = jnp.concatenate([zeros, sm32, zeros[:, :32]], axis=-1)
    lat = jnp.concatenate([cg, sg, cm, sm], axis=-1)
    ident = jnp.concatenate([jnp.ones((ctx_len, 128), F32), jnp.zeros((ctx_len, 128), F32)] * 2, axis=-1)
    return jnp.concatenate([ident, lat], axis=0)


def _attn_in(ctx, x, mods, ng, w_in, g_cq, w_uq, g_ckv, w_ukv, g_q, g_k, tab, nct):
    b = x.shape[0]
    t = ctx.shape[1] + x.shape[1]
    o1, o2, o3 = MLA_Q_LORA, MLA_Q_LORA + MLA_KV_LORA, MLA_Q_LORA + MLA_KV_LORA + MLA_ROPE
    zc = lambda n: jnp.zeros((D, n), F32)
    w_aug = jnp.concatenate([w_in[:, :o2], zc(64), w_in[:, o2:o3], zc(32), w_in[:, o3:]], axis=1).astype(BF16)
    wuq = jnp.pad(w_uq.reshape(MLA_Q_LORA, MLA_HEADS, 96), ((0, 0), (0, 0), (0, 32))).reshape(MLA_Q_LORA, 1024).astype(BF16)
    wukv = w_ukv.reshape(MLA_KV_LORA, MLA_HEADS, 128)
    wk = jnp.pad(wukv[:, :, :64], ((0, 0), (0, 0), (0, 64))).reshape(MLA_KV_LORA, 1024)
    wkv = jnp.concatenate([wk, wukv[:, :, 64:].reshape(MLA_KV_LORA, 512)], axis=1).astype(BF16)
    gcq = (g_cq * (MLA_SCALE * LOG2E)).reshape(1, -1)
    gq = (jnp.tile(g_q, GQA_HEADS) * (GQA_SCALE * LOG2E)).reshape(1, -1)
    gk = jnp.tile(g_k, 2).reshape(1, -1)
    bd = jnp.kron(jnp.eye(GQA_HEADS, dtype=F32), jnp.full((64, 64), 1.0 / 64, F32)).astype(BF16)
    nt = t // TM
    tok = lambda bb, i: (bb, i, 0)
    hm = lambda bb, i: (bb, 0, i, 0)
    return pl.pallas_call(
        functools.partial(_attn_in_kernel, nct=nct),
        grid=(b, nt),
        in_specs=_stream_specs(nct) + [
                  pl.BlockSpec((1, 1, 6, D), lambda bb, i: (bb, jnp.where(i < nct, 1, 0), 0, 0)),
                  _const_spec((1, D)), _const_spec((D, ATTN_IN_AUG)), _const_spec((1, MLA_Q_LORA)),
                  _const_spec((MLA_Q_LORA, 1024)), _const_spec((1, MLA_KV_LORA)), _const_spec((MLA_KV_LORA, 1536)),
                  _const_spec((1, 512)), _const_spec((1, 128)), _const_spec((512, 512)),
                  pl.BlockSpec((TM, 512), lambda bb, i: (i, 0))],
        out_specs=[pl.BlockSpec((1, 8, TM, 128), hm), pl.BlockSpec((1, 8, TM, 128), hm),
                   pl.BlockSpec((1, 8, TM, 128), hm), pl.BlockSpec((1, 8, TM, 128), hm),
                   pl.BlockSpec((1, TM, 128), tok), pl.BlockSpec((1, 4, TM, 128), hm)],
        out_shape=[jax.ShapeDtypeStruct((b, 8, t, 128), BF16), jax.ShapeDtypeStruct((b, 8, t, 128), BF16),
                   jax.ShapeDtypeStruct((b, 8, t, 128), BF16), jax.ShapeDtypeStruct((b, 8, t, 128), BF16),
                   jax.ShapeDtypeStruct((b, t, 128), BF16), jax.ShapeDtypeStruct((b, 4, t, 128), BF16)],
        compiler_params=_cparams(("arbitrary", "arbitrary")),
        name="attn_in",
    )(ctx, x, mods, ng.reshape(1, D), w_aug, gcq, wuq, g_ckv.reshape(1, -1), wkv, gq, gk, bd, tab)


def _softmax_pv(q, k, v, sum_lane):
    s = lax.dot_general(q, k, (((1,), (1,)), ((), ())), preferred_element_type=F32)
    m = jnp.max(s, axis=-1, keepdims=True)
    o = jnp.dot(jnp.exp2(s - m).astype(BF16), v, preferred_element_type=F32)
    return o / o[:, sum_lane:sum_lane + 1]


def _mla_kernel(q_ref, k_ref, v_ref, o_ref, *, nct, ctx_len):
    i = pl.program_id(1)
    lo = lax.broadcasted_iota(jnp.int32, (q_ref.shape[2], LANES), 1) < 64

    def run(nk):
        for pr in range(MLA_HEADS // 2):
            oa = _softmax_pv(q_ref[0, 2 * pr], k_ref[0, 2 * pr, 0:nk, :], v_ref[0, 2 * pr, 0:nk, :], 64)
            ob = _softmax_pv(q_ref[0, 2 * pr + 1], k_ref[0, 2 * pr + 1, 0:nk, :], v_ref[0, 2 * pr + 1, 0:nk, :], 0)
            o_ref[0, :, pr * 128:(pr + 1) * 128] = jnp.where(lo, oa, ob).astype(BF16)

    @pl.when(i < nct)
    def _():
        run(ctx_len)

    @pl.when(i >= nct)
    def _():
        run(k_ref.shape[2])


def _mla_attention(qm, km, vm, nct, ctx_len):
    b, _, t, _ = qm.shape
    nt = t // TM
    return pl.pallas_call(
        functools.partial(_mla_kernel, nct=nct, ctx_len=ctx_len),
        grid=(b, nt),
        in_specs=[pl.BlockSpec((1, 8, TM, 128), lambda bb, i: (bb, 0, i, 0)),
                  pl.BlockSpec((1, 8, t, 128), lambda bb, i: (bb, 0, 0, 0)),
                  pl.BlockSpec((1, 8, t, 128), lambda bb, i: (bb, 0, 0, 0))],
        out_specs=pl.BlockSpec((1, TM, 512), lambda bb, i: (bb, i, 0)),
        out_shape=jax.ShapeDtypeStruct((b, t, 512), BF16),
        compiler_params=_cparams(("arbitrary", "arbitrary")),
        name="mla_attention",
    )(qm, km, vm)


def _gqa_kernel(q_ref, k_ref, v_ref, o_ref, *, nct, ctx_len):
    i = pl.program_id(1)
    lo = lax.broadcasted_iota(jnp.int32, (q_ref.shape[2], LANES), 1) < 64

    def run(nk):
        k = k_ref[0, 0:nk, :]
        for g in range(2):
            ve = v_ref[0, 2 * g, 0:nk, :]
            vo = v_ref[0, 2 * g + 1, 0:nk, :]
            for pr in range(2):
                hd = 4 * g + 2 * pr
                oa = _softmax_pv(q_ref[0, hd], k, ve, 64)
                ob = _softmax_pv(q_ref[0, hd + 1], k, vo, 0)
                o_ref[0, :, (hd // 2) * 128:(hd // 2 + 1) * 128] = jnp.where(lo, oa, ob).astype(BF16)

    @pl.when(i < nct)
    def _():
        run(ctx_len)

    @pl.when(i >= nct)
    def _():
        run(k_ref.shape[1])


def _gqa_attention(qg, kg, vgs, nct, ctx_len):
    b, _, t, _ = qg.shape
    nt = t // TM
    return pl.pallas_call(
        functools.partial(_gqa_kernel, nct=nct, ctx_len=ctx_len),
        grid=(b, nt),
        in_specs=[pl.BlockSpec((1, 8, TM, 128), lambda bb, i: (bb, 0, i, 0)),
                  pl.BlockSpec((1, t, 128), lambda bb, i: (bb, 0, 0)),
                  pl.BlockSpec((1, 4, t, 128), lambda bb, i: (bb, 0, 0, 0))],
        out_specs=pl.BlockSpec((1, TM, 512), lambda bb, i: (bb, i, 0)),
        out_shape=jax.ShapeDtypeStruct((b, t, 512), BF16),
        compiler_params=_cparams(("arbitrary", "arbitrary")),
        name="gqa_attention",
    )(qg, kg, vgs)


def _pack_bf16_pairs(h):
    half = h.shape[1] // 2
    lo = pltpu.bitcast(h[:, :half].astype(BF16).astype(F32), jnp.uint32) >> 16
    hi = pltpu.bitcast(h[:, half:].astype(BF16).astype(F32), jnp.uint32) & jnp.uint32(0xFFFF0000)
    return lo | hi


def _unpack_bf16_pairs(w):
    return pltpu.bitcast(w << 16, F32), pltpu.bitcast(w & jnp.uint32(0xFFFF0000), F32)


def _store_planes(ref, words):
    ref[0, 0] = words[:, 0:SC_ROW_WORDS]
    ref[1, 0] = words[:, SC_ROW_WORDS:2 * SC_ROW_WORDS]


def _route_tile(lt, rt_ref, carry_ref, counted):
    n_e, rows = lt.shape
    eidx = lax.broadcasted_iota(jnp.int32, (n_e, rows), 0).astype(F32)
    work = lt
    vals, ids, hots = [], [], []
    for _ in range(TOP_K):
        m = jnp.max(work, axis=0, keepdims=True)
        idx = jnp.min(jnp.where(work == m, eidx, float(N_EXPERTS)), axis=0, keepdims=True)
        hot = eidx == idx
        vals.append(m)
        ids.append(idx)
        hots.append(hot)
        work = jnp.where(hot, -jnp.inf, work)
    exps = [jnp.exp(v - vals[0]) for v in vals]
    den = exps[0] + exps[1] + exps[2] + exps[3]
    mask = jnp.where(hots[0] | hots[1] | hots[2] | hots[3], 1.0, 0.0).astype(BF16)
    cum = jnp.dot(mask, rt_ref[...], preferred_element_type=F32)
    carry = carry_ref[...]
    before = cum[:, 0:rows] + jnp.concatenate([carry] * (rows // LANES), axis=1)
    ranks = [jnp.sum(jnp.where(h, before, 0.0), axis=0, keepdims=True) for h in hots]
    carry_ref[...] = carry + counted * cum[:, rows:rows + LANES]
    return jnp.concatenate(ids + ranks + [e / den for e in exps] + [jnp.zeros((ROUTE_LANES - 3 * TOP_K, rows), F32)],
                           axis=0)


def _residual_norm_router(x, o, mods_ref, g2_ref, wrh_ref, wrl_ref, br_ref, rt_ref, x1_ref, hp_ref, slab_ref,
                          rtab_ref, cnt_ref, carry_ref, *, nct, route_ctx):
    first = (pl.program_id(0) == 0) & (pl.program_id(1) == 0)

    @pl.when(first)
    def _():
        carry_ref[...] = jnp.zeros_like(carry_ref)

    g1 = mods_ref[0, 0, 2:3, :]
    sh2 = mods_ref[0, 0, 3:4, :]
    sc2 = mods_ref[0, 0, 4:5, :]
    x1 = x + g1 * o
    h2 = _rms(x1, g2_ref[...]) * (1.0 + sc2) + sh2
    x1_ref[0] = x1
    _store_planes(hp_ref, _pack_bf16_pairs(h2))
    hh = h2.astype(BF16)
    hl = (h2 - hh.astype(F32)).astype(BF16)
    logits = (jnp.dot(hh, wrh_ref[...], preferred_element_type=F32) + jnp.dot(hl, wrh_ref[...], preferred_element_type=F32)
              + jnp.dot(hh, wrl_ref[...], preferred_element_type=F32) + br_ref[...])
    counted = 1.0 if route_ctx else jnp.where(pl.program_id(1) >= nct, 1.0, 0.0)
    slab_t = _route_tile(logits.T[0:N_EXPERTS, :], rt_ref, carry_ref, counted)
    slab_ref[0] = slab_t.T
    rtab_ref[0] = slab_t[0:ROUTE_ROWS, :]
    cnt_ref[...] = carry_ref[...]


def _post_attn_kernel(ctx_ref, x_ref, om_ref, og_ref, wout_ref, mods_ref, g2_ref, wrh_ref, wrl_ref, br_ref, rt_ref,
                      x1_ref, hp_ref, slab_ref, rtab_ref, cnt_ref, carry_ref, *, nct):
    o = (jnp.dot(om_ref[0], wout_ref[0:512, :], preferred_element_type=F32)
         + jnp.dot(og_ref[0], wout_ref[512:1024, :], preferred_element_type=F32))
    _residual_norm_router(_stream_tile(ctx_ref, x_ref, nct), o, mods_ref, g2_ref, wrh_ref, wrl_ref, br_ref, rt_ref,
                          x1_ref, hp_ref, slab_ref, rtab_ref, cnt_ref, carry_ref, nct=nct, route_ctx=True)


def _post_out_specs(b, t):
    tok = lambda bb, i: (bb, i, 0)
    specs = [pl.BlockSpec((1, TM, D), tok), pl.BlockSpec((2, 1, TM, SC_ROW_WORDS), lambda bb, i: (0, bb, i, 0)),
             pl.BlockSpec((1, TM, ROUTE_LANES), tok), pl.BlockSpec((1, ROUTE_ROWS, TM), lambda bb, i: (bb, 0, i)),
             _const_spec((N_EXPERTS, LANES))]
    shapes = [jax.ShapeDtypeStruct((b, t, D), F32), jax.ShapeDtypeStruct((2, b, t, SC_ROW_WORDS), jnp.uint32),
              jax.ShapeDtypeStruct((b, t, ROUTE_LANES), F32), jax.ShapeDtypeStruct((b, ROUTE_ROWS, t), F32),
              jax.ShapeDtypeStruct((N_EXPERTS, LANES), F32)]
    return specs, shapes


def _router_operands(w_router, b_router):
    wp = jnp.pad(w_router, ((0, 0), (0, LANES - N_EXPERTS)))
    wrh = wp.astype(BF16)
    wrl = (wp - wrh.astype(F32)).astype(BF16)
    rt = jnp.concatenate([jnp.triu(jnp.ones((TM, TM), F32), 1), jnp.ones((TM, LANES), F32)], axis=1).astype(BF16)
    return wrh, wrl, jnp.pad(b_router, (0, LANES - N_EXPERTS)).reshape(1, LANES), rt


def _router_specs():
    return [_const_spec((D, LANES)), _const_spec((D, LANES)), _const_spec((1, LANES)), _const_spec((TM, TM + LANES))]


def _post_attn(ctx, x, o_m, o_g, w_out, mods, g2, w_router, b_router, nct):
    b, t, _ = o_m.shape
    tok = lambda bb, i: (bb, i, 0)
    out_specs, out_shape = _post_out_specs(b, t)
    return pl.pallas_call(
        functools.partial(_post_attn_kernel, nct=nct),
        grid=(b, t // TM),
        in_specs=_stream_specs(nct) + [
                  pl.BlockSpec((1, TM, 512), tok), pl.BlockSpec((1, TM, 512), tok),
                  _const_spec((1024, D)),
                  pl.BlockSpec((1, 1, 6, D), lambda bb, i: (bb, jnp.where(i < nct, 1, 0), 0, 0)),
                  _const_spec((1, D))] + _router_specs(),
        out_specs=out_specs, out_shape=out_shape,
        scratch_shapes=[pltpu.VMEM((N_EXPERTS, LANES), F32)],
        compiler_params=_cparams(("arbitrary", "arbitrary")),
        name="post_attn",
    )(ctx, x, o_m, o_g, w_out.astype(BF16), mods, g2.reshape(1, D), *_router_operands(w_router, b_router))


def _route_tables(rtab, counts, n_tiles):
    e = rtab[0:TOP_K].astype(jnp.int32)
    rank = rtab[TOP_K:2 * TOP_K].astype(jnp.int32)
    counts = counts[:, 0].astype(jnp.int32)
    padded = (counts + TMOE - 1) // TMOE * TMOE
    pend = jnp.cumsum(padded)
    pstart = pend - padded
    experts = jnp.arange(N_EXPERTS, dtype=jnp.int32)
    first_live = pend - counts
    pos = rank + jnp.sum(jnp.where(e[..., None] == experts, first_live, 0), axis=-1)
    tile_start = jnp.arange(n_tiles, dtype=jnp.int32) * TMOE
    tile_expert = jnp.minimum(jnp.sum(tile_start[:, None] >= pend[None, :], axis=-1), N_EXPERTS - 1).astype(jnp.int32)
    mine = tile_expert[:, None] == experts
    live = tile_start + TMOE - jnp.sum(jnp.where(mine, first_live, 0), axis=-1)
    tile_valid = jnp.where(tile_start < pend[-1], jnp.clip(live, 0, TMOE), 0).astype(jnp.int32)
    prev_expert = jnp.concatenate([jnp.full((1,), -1, jnp.int32), tile_expert[:-1]])
    tile_first = ((tile_valid > 0) & (tile_expert != prev_expert)).astype(jnp.int32)
    later = (experts[None, :] > experts[:, None]) & (counts[None, :] > 0)
    next_of = jnp.min(jnp.where(later, experts[None, :], N_EXPERTS), axis=-1)
    next_of = jnp.where(next_of < N_EXPERTS, next_of, -1)
    tile_next = jnp.sum(jnp.where(mine, next_of, 0), axis=-1).astype(jnp.int32)
    return pos, (tile_expert, tile_valid, tile_first, tile_next)


def _sc_mesh():
    return plsc.VectorSubcoreMesh(core_axis_name="core", subcore_axis_name="subcore")


def _sc_scatter_rows(src, idx, n_out, nb_tok, nb_src, nb_plane, off):
    m = idx.shape[0]
    steps_per_k = m // SC_WINDOW // TOP_K

    def src_block(i):
        q = i % steps_per_k
        plane = q // (steps_per_k // 2)
        r = q % (steps_per_k // 2)
        return (plane * nb_plane + (r // nb_tok) * nb_src + off + r % nb_tok, 0)

    @pl.kernel(out_type=jax.ShapeDtypeStruct((n_out, SC_ROW_WORDS), src.dtype), mesh=_sc_mesh(), scratch_types=[])
    def k(src_hbm, i_hbm, o_hbm):
        def body(x_vmem, i_vmem):
            pltpu.sync_copy(x_vmem, o_hbm.at[i_vmem.at[0]])

        pltpu.emit_pipeline(
            body,
            grid=(m // SC_WINDOW,),
            in_specs=[pl.BlockSpec((SC_WINDOW, SC_ROW_WORDS), src_block),
                      pl.BlockSpec((1, SC_WINDOW), lambda i: (0, i))],
            out_specs=[],
            core_axis_name=("core", "subcore"),
            dimension_semantics=(pltpu.PARALLEL,),
        )(src_hbm, i_hbm)

    return k(src, idx.reshape(1, m))


def _sc_gather_rows(src, idx):
    m = idx.shape[0]

    @pl.kernel(out_type=jax.ShapeDtypeStruct((m, SC_ROW_WORDS), src.dtype), mesh=_sc_mesh(), scratch_types=[])
    def k(src_hbm, i_hbm, o_hbm):
        def body(i_vmem, o_vmem):
            pltpu.sync_copy(src_hbm.at[i_vmem.at[0]], o_vmem)

        pltpu.emit_pipeline(
            body,
            grid=(m // SC_WINDOW,),
            in_specs=[pl.BlockSpec((1, SC_WINDOW), lambda i: (0, i))],
            out_specs=[pl.BlockSpec((SC_WINDOW, SC_ROW_WORDS), lambda i: (i, 0))],
            core_axis_name=("core", "subcore"),
            dimension_semantics=(pltpu.PARALLEL,),
        )(i_hbm, o_hbm)

    return k(src, idx.reshape(1, m))


def _moe_kernel(te_ref, tv_ref, tf_ref, tn_ref, xs_ref, wgu_hbm, bgu_ref, wd_hbm, bd_ref, ys_ref,
                wgu_bf, wd_bf, wgu_st, wd_st, sems, *, layer):
    t = pl.program_id(0)

    def fetch(e):
        return (pltpu.make_async_copy(wgu_hbm.at[layer, e], wgu_st, sems.at[0]),
                pltpu.make_async_copy(wd_hbm.at[layer, e], wd_st, sems.at[1]))

    @pl.when(t == 0)
    def _():
        for cp in fetch(te_ref[0]):
            cp.start()

    @pl.when(tf_ref[t] == 1)
    def _():
        for cp in fetch(te_ref[t]):
            cp.wait()
        wgu_bf[...] = wgu_st[...].astype(BF16)
        wd_bf[...] = wd_st[...].astype(BF16)

        @pl.when(tn_ref[t] >= 0)
        def _():
            for cp in fetch(tn_ref[t]):
                cp.start()

    valid = tv_ref[t]
    half = TMOE // 2

    def experts(rows):
        r0 = TMOE - rows
        lo, hi = _unpack_bf16_pairs(jnp.concatenate([xs_ref[0, r0:TMOE, :], xs_ref[1, r0:TMOE, :]], axis=1))
        live = lax.broadcasted_iota(jnp.int32, (rows, 1), 0) >= rows - valid
        x = jnp.where(live, jnp.concatenate([lo, hi], axis=1), 0.0).astype(BF16)
        gu = jnp.dot(x, wgu_bf[...], preferred_element_type=F32) + bgu_ref[0, 0]
        gate = jnp.minimum(gu[:, :MOE_FF], SWIGLU_LIMIT)
        up = jnp.clip(gu[:, MOE_FF:], -SWIGLU_LIMIT, SWIGLU_LIMIT)
        act = (up + 1.0) * (gate * _sigmoid(SWIGLU_ALPHA * gate))
        y = jnp.dot(act.astype(BF16), wd_bf[...], preferred_element_type=F32) + bd_ref[0, 0]
        words = _pack_bf16_pairs(y)
        ys_ref[0, r0:TMOE, :] = words[:, 0:SC_ROW_WORDS]
        ys_ref[1, r0:TMOE, :] = words[:, SC_ROW_WORDS:2 * SC_ROW_WORDS]

    @pl.when(valid > half)
    def _():
        experts(TMOE)

    @pl.when((valid > 0) & (valid <= half))
    def _():
        experts(half)
        ys_ref[:, 0:half, :] = jnp.zeros((2, half, SC_ROW_WORDS), jnp.uint32)

    @pl.when(valid == 0)
    def _():
        ys_ref[...] = jnp.zeros_like(ys_ref)


def _moe_experts(xs, tiles, layer, w_gu, b_gu, w_d, b_d):
    n_rows = xs.shape[1]
    n_tiles = n_rows // TMOE
    depth = w_gu.shape[0]
    grid_spec = pltpu.PrefetchScalarGridSpec(
        num_scalar_prefetch=4,
        grid=(n_tiles,),
        in_specs=[pl.BlockSpec((2, TMOE, SC_ROW_WORDS), lambda t, te, tv, tf, tn: (0, t, 0)),
                  pl.BlockSpec(memory_space=pl.ANY),
                  pl.BlockSpec((1, 1, 1, 2 * MOE_FF), lambda t, te, tv, tf, tn: (layer, te[t], 0, 0)),
                  pl.BlockSpec(memory_space=pl.ANY),
                  pl.BlockSpec((1, 1, 1, D), lambda t, te, tv, tf, tn: (layer, te[t], 0, 0))],
        out_specs=pl.BlockSpec((2, TMOE, SC_ROW_WORDS), lambda t, te, tv, tf, tn: (0, t, 0)),
        scratch_shapes=[pltpu.VMEM((D, 2 * MOE_FF), BF16), pltpu.VMEM((MOE_FF, D), BF16),
                        pltpu.VMEM((D, 2 * MOE_FF), F32), pltpu.VMEM((MOE_FF, D), F32),
                        pltpu.SemaphoreType.DMA((2,))],
    )
    return pl.pallas_call(
        functools.partial(_moe_kernel, layer=layer),
        grid_spec=grid_spec,
        out_shape=jax.ShapeDtypeStruct((2, n_rows, SC_ROW_WORDS), jnp.uint32),
        compiler_params=_cparams(("arbitrary",)),
        name="moe_experts",
    )(*tiles, xs, w_gu, b_gu.reshape(depth, N_EXPERTS, 1, -1), w_d, b_d.reshape(depth, N_EXPERTS, 1, -1))


def _moe(hp, rtab, counts, layer, w_gu, b_gu, w_d, b_d, n_seq, row0):
    _, b, t, _ = hp.shape
    n_tok = b * n_seq
    n_tiles = -(-n_tok * TOP_K // TMOE) + N_EXPERTS
    n_rows = n_tiles * TMOE
    rtab = jnp.transpose(rtab[:, :, row0:row0 + n_seq], (1, 0, 2)).reshape(ROUTE_ROWS, n_tok)
    pos, tiles = _route_tables(rtab, counts, n_tiles)
    idx = (pos[:, None, :] + (jnp.arange(2, dtype=jnp.int32) * n_rows)[None, :, None]).reshape(-1)
    xs = _sc_scatter_rows(hp.reshape(2 * b * t, SC_ROW_WORDS), idx, 2 * n_rows, n_seq // SC_WINDOW, t // SC_WINDOW,
                          b * t // SC_WINDOW, row0 // SC_WINDOW)
    ys = _moe_experts(xs.reshape(2, n_rows, SC_ROW_WORDS), tiles, layer, w_gu, b_gu, w_d, b_d)
    g = _sc_gather_rows(ys.reshape(2 * n_rows, SC_ROW_WORDS), idx)
    return g.reshape(TOP_K, 2, b, n_seq, SC_ROW_WORDS)


def _combine(g_ref, slab, rows=slice(None)):
    acc_lo = acc_hi = None
    for k in range(TOP_K):
        lo, hi = _unpack_bf16_pairs(jnp.concatenate([g_ref[k, 0, 0, rows, :], g_ref[k, 1, 0, rows, :]], axis=1))
        w = slab[:, 2 * TOP_K + k:2 * TOP_K + k + 1]
        acc_lo = w * lo if acc_lo is None else acc_lo + w * lo
        acc_hi = w * hi if acc_hi is None else acc_hi + w * hi
    return jnp.concatenate([acc_lo, acc_hi], axis=1)


def _ssm_in_kernel(x_ref, g_ref, slab_ref, modsp_ref, mods_ref, ng_ref, w_ref, x1_ref, z_ref, xbc_ref, dt_ref, v_ref,
                   wbf_ref):
    hi = SSM_D + SSM_XBC + SSM_HEADS

    @pl.when((pl.program_id(0) == 0) & (pl.program_id(1) == 0))
    def _():
        rows = 128
        for r0 in range(0, D, rows):
            r = slice(r0, r0 + rows)
            wbf_ref[r, 0:hi] = w_ref[0, r, 0:hi].astype(BF16)
            wbf_ref[r, hi:hi + LANES - SSM_HEADS] = jnp.zeros((rows, LANES - SSM_HEADS), BF16)
            wbf_ref[r, hi + LANES - SSM_HEADS:SSM_IN_AUG] = w_ref[0, r, hi:w_ref.shape[2]].astype(BF16)

    sh = mods_ref[0, 0, 0:1, :]
    sc = mods_ref[0, 0, 1:2, :]
    gate = modsp_ref[0, 0, 5:6, :]
    half = x_ref.shape[1] // 2
    for p in range(2):
        r = slice(p * half, (p + 1) * half)
        x = x_ref[0, r, :] + gate * _combine(g_ref, slab_ref[0, r, :], r)
        x1_ref[0, r, :] = x
        h = _rms(x, ng_ref[...]) * (1.0 + sc) + sh
        u = jnp.dot(h.astype(BF16), wbf_ref[...], preferred_element_type=F32)
        z_ref[0, r, :] = u[:, 0:1024]
        xbc_ref[0, r, :] = u[:, 1024:2560]
        dt_ref[0, r, :] = u[:, 2560:2688]
        v_ref[0, r, :] = u[:, 2688:3712] * _sigmoid(u[:, 3712:4736])


def _ssm_in(x_all, g_all, slab, mods_prev, mods, ng, w_in, nct):
    b, t, _ = x_all.shape
    tok = lambda bb, i: (bb, i, 0)
    modspec = pl.BlockSpec((1, 1, 6, D), lambda bb, i: (bb, jnp.where(i < nct, 1, 0), 0, 0))
    return pl.pallas_call(
        _ssm_in_kernel,
        grid=(b, t // TM),
        in_specs=[pl.BlockSpec((1, TM, D), tok),
                  pl.BlockSpec((TOP_K, 2, 1, TM, SC_ROW_WORDS), lambda bb, i: (0, 0, bb, i, 0)),
                  pl.BlockSpec((1, TM, ROUTE_LANES), tok), modspec, modspec,
                  _const_spec((1, D)),
                  pl.BlockSpec((1,) + w_in.shape[1:], lambda bb, i: (0, 0, 0), pipeline_mode=pl.Buffered(1))],
        scratch_shapes=[pltpu.VMEM((D, SSM_IN_AUG), BF16)],
        out_specs=[pl.BlockSpec((1, TM, D), tok), pl.BlockSpec((1, TM, 1024), tok), pl.BlockSpec((1, TM, SSM_XBC), tok),
                   pl.BlockSpec((1, TM, 128), tok), pl.BlockSpec((1, TM, 1024), tok)],
        out_shape=[jax.ShapeDtypeStruct((b, t, D), F32), jax.ShapeDtypeStruct((b, t, 1024), F32),
                   jax.ShapeDtypeStruct((b, t, SSM_XBC), F32), jax.ShapeDtypeStruct((b, t, 128), F32),
                   jax.ShapeDtypeStruct((b, t, 1024), F32)],
        compiler_params=_cparams(("arbitrary", "arbitrary")),
        name="ssm_in",
    )(x_all, g_all, slab, mods_prev, mods, ng.reshape(1, D), w_in)


def _conv_kernel(x_ref, w_ref, b_ref, o_ref, pad_ref, *, taps, ctx_len, silu):
    t = x_ref.shape[1]
    ct = x_ref.shape[2]
    half = taps // 2
    zeros = jnp.zeros((HALO, ct), F32)
    pad_ref[0:HALO, :] = zeros
    pad_ref[HALO:HALO + ctx_len, :] = x_ref[0, 0:ctx_len, :]
    pad_ref[HALO + ctx_len:2 * HALO + ctx_len, :] = zeros
    pad_ref[2 * HALO + ctx_len:2 * HALO + t, :] = x_ref[0, ctx_len:t, :]
    pad_ref[2 * HALO + t:3 * HALO + t, :] = zeros
    w = w_ref[...]
    bias = b_ref[...]
    rows = CHUNK + 2 * HALO

    def segment(out_start, length, pad_start):
        def body(c, carry):
            base = pl.multiple_of(c * CHUNK, CHUNK)
            win = pad_ref[pl.ds(pad_start - HALO + base, rows), :]
            acc = jnp.broadcast_to(bias, (CHUNK, ct))
            for r in range(8):
                ks = [k for k in range(taps) if (HALO + k - half) % 8 == r]
                if not ks:
                    continue
                rolled = win if r == 0 else pltpu.roll(win, rows - r, 0)
                for k in ks:
                    off = HALO + k - half - r
                    acc = acc + w[k:k + 1, :] * rolled[off:off + CHUNK, :]
            if silu:
                acc = acc * _sigmoid(acc)
            o_ref[0, pl.ds(out_start + base, CHUNK), :] = acc
            return carry
        lax.fori_loop(0, length // CHUNK, body, 0)

    segment(0, ctx_len, HALO)
    segment(ctx_len, t - ctx_len, 2 * HALO + ctx_len)


def _depthwise_conv(x, w, bias, ctx_len, silu):
    b, t, c = x.shape
    taps = w.shape[0]
    ct = 256
    return pl.pallas_call(
        functools.partial(_conv_kernel, taps=taps, ctx_len=ctx_len, silu=silu),
        grid=(b, c // ct),
        in_specs=[pl.BlockSpec((1, t, ct), lambda bb, j: (bb, 0, j)),
                  pl.BlockSpec((taps, ct), lambda bb, j: (0, j)),
                  pl.BlockSpec((1, ct), lambda bb, j: (0, j))],
        out_specs=pl.BlockSpec((1, t, ct), lambda bb, j: (bb, 0, j)),
        out_shape=jax.ShapeDtypeStruct((b, t, c), F32),
        scratch_shapes=[pltpu.VMEM((t + 3 * HALO, ct), F32)],
        compiler_params=_cparams(("arbitrary", "arbitrary")),
        name=f"depthwise_conv{taps}",
    )(x, w, bias.reshape(1, c))


def _ssd_chunk(xbc, dt_raw, a_row, bias_row, emat, state_ref, d, reverse):
    L = CHUNK
    ri = lax.broadcasted_iota(jnp.int32, (L, L), 0)
    ci = lax.broadcasted_iota(jnp.int32, (L, L), 1)
    lane_lo = lax.broadcasted_iota(jnp.int32, (L, LANES), 1) < 64
    mask = (ci >= ri) if reverse else (ri >= ci)
    tri = mask.astype(F32)

    xdt_in = dt_raw + bias_row
    dt = jnp.maximum(xdt_in, 0.0) + jnp.log1p(jnp.exp(-jnp.abs(xdt_in)))
    da = dt * a_row
    cs = jnp.dot(tri, da, precision=HIGHEST, preferred_element_type=F32)
    cs_t = cs.T
    end = 0 if reverse else L - 1
    a_end = cs[end:end + 1, :]
    dth = dt.astype(BF16)
    dtl = (dt - dth.astype(F32)).astype(BF16)
    dtx = jnp.dot(dth, emat, preferred_element_type=F32) + jnp.dot(dtl, emat, preferred_element_type=F32)

    def lanes(v, hh):
        return jnp.broadcast_to(v[:, hh:hh + 1], (v.shape[0], LANES))

    ys = []
    for g in range(2):
        bm = xbc[:, SSM_D + g * SSM_STATE:SSM_D + (g + 1) * SSM_STATE].astype(BF16)
        cm = xbc[:, SSM_D + 2 * SSM_STATE + g * SSM_STATE:SSM_D + 2 * SSM_STATE + (g + 1) * SSM_STATE].astype(BF16)
        cb = lax.dot_general(cm, bm, (((1,), (1,)), ((), ())), preferred_element_type=F32)
        state = state_ref[d, g]
        y_off = jnp.dot(cm, state.astype(BF16), preferred_element_type=F32)
        xdd_blocks, sdec_blocks = [], []
        for p in range(4):
            hp = g * 4 + p
            gmats, e_cols, d_outs = [], [], []
            for j in range(2):
                hh = 2 * hp + j
                col = lanes(cs, hh)
                row = jnp.broadcast_to(cs_t[hh:hh + 1, :], (L, L))
                dec = jnp.exp(jnp.where(mask, col - row, -jnp.inf))
                gmats.append((cb * dec).astype(BF16))
                e_cols.append(jnp.exp(col))
                d_outs.append(jnp.exp(lanes(a_end, hh) - col))
            din = jnp.where(lane_lo, e_cols[0], e_cols[1])
            dout = jnp.where(lane_lo, d_outs[0], d_outs[1])
            xdt = xbc[:, hp * 128:(hp + 1) * 128] * dtx[:, hp * 128:(hp + 1) * 128]
            xdd_blocks.append((xdt * dout).astype(BF16))
            sdec_blocks.append(din[end:end + 1, :])
            xdt = xdt.astype(BF16)
            ya = jnp.dot(gmats[0], xdt, preferred_element_type=F32)
            yb = jnp.dot(gmats[1], xdt, preferred_element_type=F32)
            ys.append(jnp.where(lane_lo, ya, yb) + din * y_off[:, p * 128:(p + 1) * 128])
        upd = lax.dot_general(bm, jnp.concatenate(xdd_blocks, axis=1), (((0,), (0,)), ((), ())),
                              preferred_element_type=F32)
        state_ref[d, g] = state * jnp.concatenate(sdec_blocks, axis=1) + upd
    return jnp.concatenate(ys, axis=1)


def _ssd_kernel(xf_ref, xb_ref, dtf_ref, dtb_ref, alog_ref, bias_ref, emat_ref, yf_ref, yb_ref, state_ref):
    @pl.when(pl.program_id(1) == 0)
    def _():
        state_ref[...] = jnp.zeros_like(state_ref)

    a = -jnp.exp(alog_ref[...])
    bias = bias_ref[...]
    emat = emat_ref[...]
    yf_ref[0] = _ssd_chunk(xf_ref[0], dtf_ref[0], a[0:1, :], bias[0:1, :], emat, state_ref, 0, False)
    yb_ref[0] = _ssd_chunk(xb_ref[0], dtb_ref[0], a[1:2, :], bias[1:2, :], emat, state_ref, 1, True)


def _ssd(xbc, dt, a_log, dt_bias, ctx_len):
    b, t, _ = xbc.shape
    nc = t // CHUNK
    ncc = ctx_len // CHUNK

    def fwd(bb, j):
        return (bb, j, 0)

    def bwd(bb, j):
        return (bb, jnp.where(j < ncc, ncc - 1 - j, nc - 1 + ncc - j), 0)

    pad = lambda v: jnp.pad(v, ((0, 0), (0, 128 - SSM_HEADS)))
    emat = (jnp.arange(LANES)[:, None] == jnp.arange(SSM_D)[None, :] // SSM_HEAD_DIM).astype(BF16)
    return pl.pallas_call(
        _ssd_kernel,
        grid=(b, nc),
        in_specs=[pl.BlockSpec((1, CHUNK, SSM_XBC), fwd), pl.BlockSpec((1, CHUNK, SSM_XBC), bwd),
                  pl.BlockSpec((1, CHUNK, 128), fwd), pl.BlockSpec((1, CHUNK, 128), bwd),
                  _const_spec((2, 128)), _const_spec((2, 128)), _const_spec((LANES, SSM_D))],
        out_specs=[pl.BlockSpec((1, CHUNK, SSM_D), fwd), pl.BlockSpec((1, CHUNK, SSM_D), bwd)],
        out_shape=[jax.ShapeDtypeStruct((b, t, SSM_D), F32), jax.ShapeDtypeStruct((b, t, SSM_D), F32)],
        scratch_shapes=[pltpu.VMEM((2, 2, SSM_STATE, 512), F32)],
        compiler_params=_cparams(("arbitrary", "arbitrary")),
        name="ssd_scan",
    )(xbc, xbc, dt, dt, pad(a_log), pad(dt_bias), emat)


def _post_ssm_kernel(x_ref, yf_ref, yb_ref, xs_ref, z_ref, v_ref, dsk_ref, sg_ref, lng_ref, lnb_ref, wout_ref,
                     mods_ref, g2_ref, wrh_ref, wrl_ref, br_ref, rt_ref, x1_ref, hp_ref, slab_ref, rtab_ref, cnt_ref,
                     carry_ref, wbf_ref, *, nct):
    @pl.when((pl.program_id(0) == 0) & (pl.program_id(1) == 0))
    def _():
        for r0 in range(0, wbf_ref.shape[0], 256):
            wbf_ref[r0:r0 + 256, :] = wout_ref[0, r0:r0 + 256, :].astype(BF16)

    y = yf_ref[0] + yb_ref[0] + dsk_ref[...] * xs_ref[0]
    z = z_ref[0]
    y_ssm = _rms(y * (z * _sigmoid(z)), sg_ref[...])
    v = v_ref[0]
    mu = jnp.mean(v, axis=-1, keepdims=True)
    vc = v - mu
    ln = vc * lax.rsqrt(jnp.mean(vc * vc, axis=-1, keepdims=True) + EPS) * lng_ref[...] + lnb_ref[...]
    y_conv = ln * _sigmoid(ln)
    o = (jnp.dot(y_ssm.astype(BF16), wbf_ref[0:1024, :], preferred_element_type=F32)
         + jnp.dot(y_conv.astype(BF16), wbf_ref[1024:2048, :], preferred_element_type=F32))
    _residual_norm_router(x_ref[0], o, mods_ref, g2_ref, wrh_ref, wrl_ref, br_ref, rt_ref, x1_ref, hp_ref, slab_ref,
                          rtab_ref, cnt_ref, carry_ref, nct=nct, route_ctx=False)


def _post_ssm(x_all, yf, yb, xbc_act, z, v, d_skip, ssm_norm_g, ln_g, ln_b, w_out, mods, g2, w_router, b_router, nct):
    b, t, _ = x_all.shape
    tok = lambda bb, i: (bb, i, 0)
    dsk = jnp.repeat(d_skip[0] + d_skip[1], SSM_HEAD_DIM).reshape(1, SSM_D)
    out_specs, out_shape = _post_out_specs(b, t)
    row = lambda: pl.BlockSpec((1, TM, 1024), tok)
    return pl.pallas_call(
        functools.partial(_post_ssm_kernel, nct=nct),
        grid=(b, t // TM),
        in_specs=[row(), row(), row(), row(), row(), row(),
                  _const_spec((1, SSM_D)), _const_spec((1, SSM_D)), _const_spec((1, D)), _const_spec((1, D)),
                  pl.BlockSpec((1,) + w_out.shape[1:], lambda bb, i: (0, 0, 0), pipeline_mode=pl.Buffered(1)),
                  pl.BlockSpec((1, 1, 6, D), lambda bb, i: (bb, jnp.where(i < nct, 1, 0), 0, 0)),
                  _const_spec((1, D))] + _router_specs(),
        out_specs=out_specs, out_shape=out_shape,
        scratch_shapes=[pltpu.VMEM((N_EXPERTS, LANES), F32), pltpu.VMEM(w_out.shape[1:], BF16)],
        compiler_params=_cparams(("arbitrary", "arbitrary")),
        name="post_ssm",
    )(x_all, yf, yb, xbc_act, z, v, dsk, ssm_norm_g.reshape(1, -1), ln_g.reshape(1, -1), ln_b.reshape(1, -1),
      w_out, mods, g2.reshape(1, D), *_router_operands(w_router, b_router))


def _final_kernel(x_ref, g_ref, slab_ref, mods_ref, fg_ref, o_ref):
    x = x_ref[0] + mods_ref[0, 0, 5:6, :] * _combine(g_ref, slab_ref[0])
    o_ref[0] = _rms(x, fg_ref[...])


def _final(x_all, g_lat, slab, mods, final_g, nct):
    b, s = g_lat.shape[2], g_lat.shape[3]
    return pl.pallas_call(
        _final_kernel,
        grid=(b, s // TM),
        in_specs=[pl.BlockSpec((1, TM, D), lambda bb, i: (bb, i + nct, 0)),
                  pl.BlockSpec((TOP_K, 2, 1, TM, SC_ROW_WORDS), lambda bb, i: (0, 0, bb, i, 0)),
                  pl.BlockSpec((1, TM, ROUTE_LANES), lambda bb, i: (bb, i + nct, 0)),
                  pl.BlockSpec((1, 1, 6, D), lambda bb, i: (bb, 0, 0, 0)),
                  _const_spec((1, D))],
        out_specs=pl.BlockSpec((1, TM, D), lambda bb, i: (bb, i, 0)),
        out_shape=jax.ShapeDtypeStruct((b, s, D), F32),
        compiler_params=_cparams(("arbitrary", "arbitrary")),
        name="final_norm",
    )(x_all, g_lat, slab, mods, final_g.reshape(1, D))


def kernel(x, c, ctx, c_ctx, w_mod, b_mod, norm_g, attn_w_in, mla_g_cq, mla_w_uq, mla_g_ckv, mla_w_ukv, gqa_g_q, gqa_g_k, attn_w_out, ssm_w_in, ssm_conv_w, ssm_conv_b, ssm_a_log, ssm_dt_bias, ssm_d, ssm_norm_g, conf_dw_w, conf_dw_b, conf_ln_g, conf_ln_b, ssm_w_out, moe_w_router, moe_b_router, moe_w_gate_up, moe_b_gate_up, moe_w_down, moe_b_down, final_g):
    b, s, _ = x.shape
    ctx_len = ctx.shape[1]
    t = ctx_len + s
    assert ctx_len % TM == 0 and s % TM == 0 and s % GRID_W == 0
    nct = ctx_len // TM

    c_rows = jnp.concatenate([c, c_ctx[None, :], jnp.zeros((-(b + 1) % 8, D), F32)], axis=0)
    mod_all = _modulations(c_rows, w_mod, b_mod)
    mods = []
    for i in range(w_mod.shape[0]):
        lat = mod_all[i, :b].reshape(b, 1, 6, D)
        cm = jnp.broadcast_to(mod_all[i, b].reshape(1, 1, 6, D), (b, 1, 6, D))
        mods.append(jnp.concatenate([lat, cm], axis=1))

    tab = _rope_tables(s, ctx_len)
    qm, km, vm, qg, kg, vgs = _attn_in(ctx, x, mods[0], norm_g[0, 0], attn_w_in[0], mla_g_cq[0], mla_w_uq[0], mla_g_ckv[0],
                                       mla_w_ukv[0], gqa_g_q[0], gqa_g_k[0], tab, nct)
    o_m = _mla_attention(qm, km, vm, nct, ctx_len)
    o_g = _gqa_attention(qg, kg, vgs, nct, ctx_len)
    x_all, hp, slab0, rtab, counts = _post_attn(ctx, x, o_m, o_g, attn_w_out[0], mods[0], norm_g[0, 1], moe_w_router[0],
                                                moe_b_router[0], nct)
    g_all = _moe(hp, rtab, counts, 0, moe_w_gate_up, moe_b_gate_up, moe_w_down, moe_b_down, t, 0)

    x_all, z, xbc, dt, v = _ssm_in(x_all, g_all, slab0, mods[0], mods[1], norm_g[1, 0], ssm_w_in, nct)
    xbc_act = _depthwise_conv(xbc, ssm_conv_w[0], ssm_conv_b[0], ctx_len, True)
    v_conv = _depthwise_conv(v, conf_dw_w[0], conf_dw_b[0], ctx_len, False)
    yf, yb = _ssd(xbc_act, dt, ssm_a_log[0], ssm_dt_bias[0], ctx_len)
    x_all, hp, slab1, rtab, counts = _post_ssm(x_all, yf, yb, xbc_act, z, v_conv, ssm_d[0], ssm_norm_g[0], conf_ln_g[0],
                                               conf_ln_b[0], ssm_w_out, mods[1], norm_g[1, 1], moe_w_router[1],
                                               moe_b_router[1], nct)
    g_lat = _moe(hp, rtab, counts, 1, moe_w_gate_up, moe_b_gate_up, moe_w_down, moe_b_down, s, ctx_len)
    return _final(x_all, g_lat, slab1, mods[1], final_g, nct)
```

```python
import functools

import jax
import jax.numpy as jnp
from jax import lax
from jax.experimental import pallas as pl
from jax.experimental.pallas import tpu as pltpu
from jax.experimental.pallas import tpu_sc as plsc

F32 = jnp.float32
BF16 = jnp.bfloat16
HIGHEST = lax.Precision.HIGHEST

D = 1024
EPS = 1e-6
GRID_W = 64
ROPE_THETA = 10000.0
LOG2E = 1.4426950408889634

MLA_HEADS = 8
MLA_NOPE = 64
MLA_ROPE = 32
MLA_Q_LORA = 256
MLA_KV_LORA = 128
MLA_SCALE = (MLA_NOPE + MLA_ROPE) ** -0.5
GQA_HEADS = 8
GQA_HEAD_DIM = 64
GQA_SCALE = GQA_HEAD_DIM ** -0.5
ATTN_IN_AUG = 1280

SSM_HEADS = 16
SSM_HEAD_DIM = 64
SSM_STATE = 128
SSM_D = 1024
SSM_XBC = 1536
SSM_CONV = 5
CONF_K = 31
CHUNK = 128
SSM_IN_AUG = 1024 + 1536 + 128 + 2048

N_EXPERTS = 32
TOP_K = 4
MOE_FF = 1024
SWIGLU_LIMIT = 7.0
SWIGLU_ALPHA = 1.702

TM = 256
TMOE = 512
ROUTE_LANES = 128
ROUTE_ROWS = 16
COMBINE_GROUPS = 2
SC_WINDOW = 128
SC_ROW_WORDS = 256
LANES = 128
HALO = 16
VMEM_LIMIT = 56 * 1024 * 1024


def _cparams(sem):
    return pltpu.CompilerParams(dimension_semantics=sem, vmem_limit_bytes=VMEM_LIMIT)


def _rms(x, g):
    return x * lax.rsqrt(jnp.mean(x * x, axis=-1, keepdims=True) + EPS) * g


def _sigmoid(x):
    return 1.0 / (1.0 + jnp.exp(-x))


def _const_spec(shape):
    n = len(shape)
    return pl.BlockSpec(shape, lambda *_: (0,) * n)


def _mod_kernel(c_ref, w_ref, b_ref, o_ref):
    c = c_ref[...]
    o_ref[0] = jnp.dot(c * _sigmoid(c), w_ref[0], precision=HIGHEST, preferred_element_type=F32) + b_ref[0]


def _modulations(c_rows, w_mod, b_mod):
    depth, _, n = w_mod.shape
    tn = 512
    r = c_rows.shape[0]
    return pl.pallas_call(
        _mod_kernel,
        grid=(depth, n // tn),
        in_specs=[pl.BlockSpec((r, D), lambda l, j: (0, 0)),
                  pl.BlockSpec((1, D, tn), lambda l, j: (l, 0, j)),
                  pl.BlockSpec((1, 1, tn), lambda l, j: (l, 0, j))],
        out_specs=pl.BlockSpec((1, r, tn), lambda l, j: (l, 0, j)),
        out_shape=jax.ShapeDtypeStruct((depth, r, n), F32),
        compiler_params=_cparams(("arbitrary", "arbitrary")),
        name="modulations",
    )(c_rows, w_mod, b_mod.reshape(depth, 1, n))


def _stream_tile(ctx_ref, x_ref, nct):
    return jnp.where(pl.program_id(1) < nct, ctx_ref[0], x_ref[0])


def _stream_specs(nct):
    return [pl.BlockSpec((1, TM, D), lambda bb, i: (bb, jnp.minimum(i, nct - 1), 0)),
            pl.BlockSpec((1, TM, D), lambda bb, i: (bb, jnp.maximum(i - nct, 0), 0))]


def _attn_in_kernel(ctx_ref, x_ref, mods_ref, ng_ref, win_ref, gcq_ref, wuq_ref, gckv_ref, wkv_ref, gq_ref, gk_ref, bd_ref,
                    tab_ref, qm_ref, km_ref, vm_ref, qg_ref, kg_ref, vgs_ref, *, nct):
    x = _stream_tile(ctx_ref, x_ref, nct)
    sh = mods_ref[0, 0, 0:1, :]
    sc = mods_ref[0, 0, 1:2, :]
    h = _rms(x, ng_ref[...]) * (1.0 + sc) + sh
    u = jnp.dot(h.astype(BF16), win_ref[...], preferred_element_type=F32)

    tab = tab_ref[...]
    cg, sg = tab[:, 0:128], tab[:, 128:256]
    cm, sm = tab[:, 256:384], tab[:, 384:512]
    lane = lax.broadcasted_iota(jnp.int32, (x.shape[0], LANES), 1)
    even = (lane & 1) == 0
    lo = lane < 64

    def rope(v, c, s):
        partner = jnp.where(even, pltpu.roll(v, LANES - 1, 1), pltpu.roll(v, 1, 1))
        return v * c + partner * s

    cq = _rms(u[:, 0:256], gcq_ref[...])
    qm = jnp.dot(cq.astype(BF16), wuq_ref[...], preferred_element_type=F32)
    for hd in range(MLA_HEADS):
        qm_ref[0, hd] = rope(qm[:, hd * 128:(hd + 1) * 128], cm, sm).astype(BF16)
    ckv = _rms(u[:, 256:384], gckv_ref[...])
    kv = jnp.dot(ckv.astype(BF16), wkv_ref[...], preferred_element_type=F32)
    kr = rope(u[:, 384:512], cm, sm)
    for hd in range(MLA_HEADS):
        km_ref[0, hd] = (kv[:, hd * 128:(hd + 1) * 128] + kr).astype(BF16)
    one = jnp.ones((x.shape[0], LANES), F32)
    for p in range(MLA_HEADS // 2):
        blk = kv[:, 1024 + p * 128:1024 + (p + 1) * 128]
        vm_ref[0, 2 * p] = jnp.where(lo, blk, one).astype(BF16)
        vm_ref[0, 2 * p + 1] = jnp.where(lo, one, blk).astype(BF16)

    qg = u[:, 512:1024]
    ms = jnp.dot((qg * qg).astype(BF16), bd_ref[...], preferred_element_type=F32)
    qg = qg * lax.rsqrt(ms + EPS) * gq_ref[...]
    zero = jnp.zeros((x.shape[0], LANES), F32)
    for p in range(GQA_HEADS // 2):
        blk = rope(qg[:, p * 128:(p + 1) * 128], cg, sg)
        swp = pltpu.roll(blk, 64, 1)
        if p < 2:
            qg_ref[0, 2 * p] = jnp.where(lo, blk, zero).astype(BF16)
            qg_ref[0, 2 * p + 1] = jnp.where(lo, swp, zero).astype(BF16)
        else:
            qg_ref[0, 2 * p] = jnp.where(lo, zero, swp).astype(BF16)
            qg_ref[0, 2 * p + 1] = jnp.where(lo, zero, blk).astype(BF16)
    kg = u[:, 1024:1152]
    msk = jnp.dot((kg * kg).astype(BF16), bd_ref[0:128, 0:128], preferred_element_type=F32)
    kg_ref[0] = rope(kg * lax.rsqrt(msk + EPS) * gk_ref[...], cg, sg).astype(BF16)
    vg = u[:, 1152:1280]
    vsw = pltpu.roll(vg, 64, 1)
    vgs_ref[0, 0] = jnp.where(lo, vg, one).astype(BF16)
    vgs_ref[0, 1] = jnp.where(lo, one, vsw).astype(BF16)
    vgs_ref[0, 2] = jnp.where(lo, vsw, one).astype(BF16)
    vgs_ref[0, 3] = jnp.where(lo, one, vg).astype(BF16)


def _rope_tables(seq, ctx_len):
    rows = seq // GRID_W
    row = jnp.broadcast_to(jnp.arange(rows, dtype=F32)[:, None], (rows, GRID_W)).reshape(-1)
    col = jnp.broadcast_to(jnp.arange(GRID_W, dtype=F32)[None, :], (rows, GRID_W)).reshape(-1)

    def interleaved(rot_dim):
        n_freq = rot_dim // 4
        inv_freq = ROPE_THETA ** (-jnp.arange(n_freq, dtype=F32) / n_freq)
        ang = jnp.concatenate([row[:, None] * inv_freq, col[:, None] * inv_freq], axis=-1)
        cos = jnp.repeat(jnp.cos(ang), 2, axis=-1)
        sin = jnp.repeat(jnp.sin(ang), 2, axis=-1) * jnp.tile(jnp.array([-1.0, 1.0], F32), rot_dim // 2)
        return cos, sin

    cg, sg = interleaved(GQA_HEAD_DIM)
    cg, sg = jnp.tile(cg, (1, 2)), jnp.tile(sg, (1, 2))
    cm32, sm32 = interleaved(MLA_ROPE)
    ones, zeros = jnp.ones((seq, 64), F32), jnp.zeros((seq, 64), F32)
    cm = jnp.concatenate([ones, cm32, ones[:, :32]], axis=-1)
    sm = jnp.concatenate([zeros, sm32, zeros[:, :32]], axis=-1)
    lat = jnp.concatenate([cg, sg, cm, sm], axis=-1)
    ident = jnp.concatenate([jnp.ones((ctx_len, 128), F32), jnp.zeros((ctx_len, 128), F32)] * 2, axis=-1)
    return jnp.concatenate([ident, lat], axis=0)


def _attn_in(ctx, x, mods, ng, w_in, g_cq, w_uq, g_ckv, w_ukv, g_q, g_k, tab, nct):
    b = x.shape[0]
    t = ctx.shape[1] + x.shape[1]
    o1, o2, o3 = MLA_Q_LORA, MLA_Q_LORA + MLA_KV_LORA, MLA_Q_LORA + MLA_KV_LORA + MLA_ROPE
    zc = lambda n: jnp.zeros((D, n), F32)
    w_aug = jnp.concatenate([w_in[:, :o2], zc(64), w_in[:, o2:o3], zc(32), w_in[:, o3:]], axis=1).astype(BF16)
    wuq = jnp.pad(w_uq.reshape(MLA_Q_LORA, MLA_HEADS, 96), ((0, 0), (0, 0), (0, 32))).reshape(MLA_Q_LORA, 1024).astype(BF16)
    wukv = w_ukv.reshape(MLA_KV_LORA, MLA_HEADS, 128)
    wk = jnp.pad(wukv[:, :, :64], ((0, 0), (0, 0), (0, 64))).reshape(MLA_KV_LORA, 1024)
    wkv = jnp.concatenate([wk, wukv[:, :, 64:].reshape(MLA_KV_LORA, 512)], axis=1).astype(BF16)
    gcq = (g_cq * (MLA_SCALE * LOG2E)).reshape(1, -1)
    gq = (jnp.tile(g_q, GQA_HEADS) * (GQA_SCALE * LOG2E)).reshape(1, -1)
    gk = jnp.tile(g_k, 2).reshape(1, -1)
    bd = jnp.kron(jnp.eye(GQA_HEADS, dtype=F32), jnp.full((64, 64), 1.0 / 64, F32)).astype(BF16)
    nt = t // TM
    tok = lambda bb, i: (bb, i, 0)
    hm = lambda bb, i: (bb, 0, i, 0)
    return pl.pallas_call(
        functools.partial(_attn_in_kernel, nct=nct),
        grid=(b, nt),
        in_specs=_stream_specs(nct) + [
                  pl.BlockSpec((1, 1, 6, D), lambda bb, i: (bb, jnp.where(i < nct, 1, 0), 0, 0)),
                  _const_spec((1, D)), _const_spec((D, ATTN_IN_AUG)), _const_spec((1, MLA_Q_LORA)),
                  _const_spec((MLA_Q_LORA, 1024)), _const_spec((1, MLA_KV_LORA)), _const_spec((MLA_KV_LORA, 1536)),
                  _const_spec((1, 512)), _const_spec((1, 128)), _const_spec((512, 512)),
                  pl.BlockSpec((TM, 512), lambda bb, i: (i, 0))],
        out_specs=[pl.BlockSpec((1, 8, TM, 128), hm), pl.BlockSpec((1, 8, TM, 128), hm),
                   pl.BlockSpec((1, 8, TM, 128), hm), pl.BlockSpec((1, 8, TM, 128), hm),
                   pl.BlockSpec((1, TM, 128), tok), pl.BlockSpec((1, 4, TM, 128), hm)],
        out_shape=[jax.ShapeDtypeStruct((b, 8, t, 128), BF16), jax.ShapeDtypeStruct((b, 8, t, 128), BF16),
                   jax.ShapeDtypeStruct((b, 8, t, 128), BF16), jax.ShapeDtypeStruct((b, 8, t, 128), BF16),
                   jax.ShapeDtypeStruct((b, t, 128), BF16), jax.ShapeDtypeStruct((b, 4, t, 128), BF16)],
        compiler_params=_cparams(("arbitrary", "arbitrary")),
        name="attn_in",
    )(ctx, x, mods, ng.reshape(1, D), w_aug, gcq, wuq, g_ckv.reshape(1, -1), wkv, gq, gk, bd, tab)


def _softmax_pv(q, k, v, sum_lane):
    s = lax.dot_general(q, k, (((1,), (1,)), ((), ())), preferred_element_type=F32)
    m = jnp.max(s, axis=-1, keepdims=True)
    o = jnp.dot(jnp.exp2(s - m).astype(BF16), v, preferred_element_type=F32)
    return o / o[:, sum_lane:sum_lane + 1]


def _mla_kernel(q_ref, k_ref, v_ref, o_ref, *, nct, ctx_len):
    i = pl.program_id(1)
    lo = lax.broadcasted_iota(jnp.int32, (q_ref.shape[2], LANES), 1) < 64

    def run(nk):
        for pr in range(MLA_HEADS // 2):
            oa = _softmax_pv(q_ref[0, 2 * pr], k_ref[0, 2 * pr, 0:nk, :], v_ref[0, 2 * pr, 0:nk, :], 64)
            ob = _softmax_pv(q_ref[0, 2 * pr + 1], k_ref[0, 2 * pr + 1, 0:nk, :], v_ref[0, 2 * pr + 1, 0:nk, :], 0)
            o_ref[0, :, pr * 128:(pr + 1) * 128] = jnp.where(lo, oa, ob).astype(BF16)

    @pl.when(i < nct)
    def _():
        run(ctx_len)

    @pl.when(i >= nct)
    def _():
        run(k_ref.shape[2])


def _mla_attention(qm, km, vm, nct, ctx_len):
    b, _, t, _ = qm.shape
    nt = t // TM
    return pl.pallas_call(
        functools.partial(_mla_kernel, nct=nct, ctx_len=ctx_len),
        grid=(b, nt),
        in_specs=[pl.BlockSpec((1, 8, TM, 128), lambda bb, i: (bb, 0, i, 0)),
                  pl.BlockSpec((1, 8, t, 128), lambda bb, i: (bb, 0, 0, 0)),
                  pl.BlockSpec((1, 8, t, 128), lambda bb, i: (bb, 0, 0, 0))],
        out_specs=pl.BlockSpec((1, TM, 512), lambda bb, i: (bb, i, 0)),
        out_shape=jax.ShapeDtypeStruct((b, t, 512), BF16),
        compiler_params=_cparams(("arbitrary", "arbitrary")),
        name="mla_attention",
    )(qm, km, vm)


def _gqa_kernel(q_ref, k_ref, v_ref, o_ref, *, nct, ctx_len):
    i = pl.program_id(1)
    lo = lax.broadcasted_iota(jnp.int32, (q_ref.shape[2], LANES), 1) < 64

    def run(nk):
        k = k_ref[0, 0:nk, :]
        for g in range(2):
            ve = v_ref[0, 2 * g, 0:nk, :]
            vo = v_ref[0, 2 * g + 1, 0:nk, :]
            for pr in range(2):
                hd = 4 * g + 2 * pr
                oa = _softmax_pv(q_ref[0, hd], k, ve, 64)
                ob = _softmax_pv(q_ref[0, hd + 1], k, vo, 0)
                o_ref[0, :, (hd // 2) * 128:(hd // 2 + 1) * 128] = jnp.where(lo, oa, ob).astype(BF16)

    @pl.when(i < nct)
    def _():
        run(ctx_len)

    @pl.when(i >= nct)
    def _():
        run(k_ref.shape[1])


def _gqa_attention(qg, kg, vgs, nct, ctx_len):
    b, _, t, _ = qg.shape
    nt = t // TM
    return pl.pallas_call(
        functools.partial(_gqa_kernel, nct=nct, ctx_len=ctx_len),
        grid=(b, nt),
        in_specs=[pl.BlockSpec((1, 8, TM, 128), lambda bb, i: (bb, 0, i, 0)),
                  pl.BlockSpec((1, t, 128), lambda bb, i: (bb, 0, 0)),
                  pl.BlockSpec((1, 4, t, 128), lambda bb, i: (bb, 0, 0, 0))],
        out_specs=pl.BlockSpec((1, TM, 512), lambda bb, i: (bb, i, 0)),
        out_shape=jax.ShapeDtypeStruct((b, t, 512), BF16),
        compiler_params=_cparams(("arbitrary", "arbitrary")),
        name="gqa_attention",
    )(qg, kg, vgs)


def _pack_bf16_pairs(h):
    half = h.shape[1] // 2
    lo = pltpu.bitcast(h[:, :half].astype(BF16).astype(F32), jnp.uint32) >> 16
    hi = pltpu.bitcast(h[:, half:].astype(BF16).astype(F32), jnp.uint32) & jnp.uint32(0xFFFF0000)
    return lo | hi


def _unpack_bf16_pairs(w):
    return pltpu.bitcast(w << 16, F32), pltpu.bitcast(w & jnp.uint32(0xFFFF0000), F32)


def _store_planes(ref, words):
    ref[0, 0] = words[:, 0:SC_ROW_WORDS]
    ref[1, 0] = words[:, SC_ROW_WORDS:2 * SC_ROW_WORDS]


def _route_tile(lt, rt_ref, carry_ref, counted):
    n_e, rows = lt.shape
    eidx = lax.broadcasted_iota(jnp.int32, (n_e, rows), 0).astype(F32)
    work = lt
    vals, ids, hots = [], [], []
    for _ in range(TOP_K):
        m = jnp.max(work, axis=0, keepdims=True)
        idx = jnp.min(jnp.where(work == m, eidx, float(N_EXPERTS)), axis=0, keepdims=True)
        hot = eidx == idx
        vals.append(m)
        ids.append(idx)
        hots.append(hot)
        work = jnp.where(hot, -jnp.inf, work)
    exps = [jnp.exp(v - vals[0]) for v in vals]
    den = exps[0] + exps[1] + exps[2] + exps[3]
    mask = jnp.where(hots[0] | hots[1] | hots[2] | hots[3], 1.0, 0.0).astype(BF16)
    cum = jnp.dot(mask, rt_ref[...], preferred_element_type=F32)
    carry = carry_ref[...]
    before = cum[:, 0:rows] + jnp.concatenate([carry] * (rows // LANES), axis=1)
    ranks = [jnp.sum(jnp.where(h, before, 0.0), axis=0, keepdims=True) for h in hots]
    carry_ref[...] = carry + counted * cum[:, rows:rows + LANES]
    return jnp.concatenate(ids + ranks + [e / den for e in exps] + [jnp.zeros((ROUTE_LANES - 3 * TOP_K, rows), F32)],
                           axis=0)


def _residual_norm_router(x, o, mods_ref, g2_ref, wrh_ref, wrl_ref, br_ref, rt_ref, x1_ref, hp_ref, slab_ref,
                          rtab_ref, cnt_ref, carry_ref, *, nct, route_ctx):
    first = (pl.program_id(0) == 0) & (pl.program_id(1) == 0)

    @pl.when(first)
    def _():
        carry_ref[...] = jnp.zeros_like(carry_ref)

    g1 = mods_ref[0, 0, 2:3, :]
    sh2 = mods_ref[0, 0, 3:4, :]
    sc2 = mods_ref[0, 0, 4:5, :]
    x1 = x + g1 * o
    h2 = _rms(x1, g2_ref[...]) * (1.0 + sc2) + sh2
    x1_ref[0] = x1
    _store_planes(hp_ref, _pack_bf16_pairs(h2))
    hh = h2.astype(BF16)
    hl = (h2 - hh.astype(F32)).astype(BF16)
    logits = (jnp.dot(hh, wrh_ref[...], preferred_element_type=F32) + jnp.dot(hl, wrh_ref[...], preferred_element_type=F32)
              + jnp.dot(hh, wrl_ref[...], preferred_element_type=F32) + br_ref[...])
    counted = 1.0 if route_ctx else jnp.where(pl.program_id(1) >= nct, 1.0, 0.0)
    slab_t = _route_tile(logits.T[0:N_EXPERTS, :], rt_ref, carry_ref, counted)
    slab_ref[0] = slab_t.T
    rtab_ref[0] = slab_t[0:ROUTE_ROWS, :]
    cnt_ref[...] = carry_ref[...]


def _post_attn_kernel(ctx_ref, x_ref, om_ref, og_ref, wout_ref, mods_ref, g2_ref, wrh_ref, wrl_ref, br_ref, rt_ref,
                      x1_ref, hp_ref, slab_ref, rtab_ref, cnt_ref, carry_ref, *, nct):
    o = (jnp.dot(om_ref[0], wout_ref[0:512, :], preferred_element_type=F32)
         + jnp.dot(og_ref[0], wout_ref[512:1024, :], preferred_element_type=F32))
    _residual_norm_router(_stream_tile(ctx_ref, x_ref, nct), o, mods_ref, g2_ref, wrh_ref, wrl_ref, br_ref, rt_ref,
                          x1_ref, hp_ref, slab_ref, rtab_ref, cnt_ref, carry_ref, nct=nct, route_ctx=True)


def _post_out_specs(b, t):
    tok = lambda bb, i: (bb, i, 0)
    specs = [pl.BlockSpec((1, TM, D), tok), pl.BlockSpec((2, 1, TM, SC_ROW_WORDS), lambda bb, i: (0, bb, i, 0)),
             pl.BlockSpec((1, TM, ROUTE_LANES), tok), pl.BlockSpec((1, ROUTE_ROWS, TM), lambda bb, i: (bb, 0, i)),
             _const_spec((N_EXPERTS, LANES))]
    shapes = [jax.ShapeDtypeStruct((b, t, D), F32), jax.ShapeDtypeStruct((2, b, t, SC_ROW_WORDS), jnp.uint32),
              jax.ShapeDtypeStruct((b, t, ROUTE_LANES), F32), jax.ShapeDtypeStruct((b, ROUTE_ROWS, t), F32),
              jax.ShapeDtypeStruct((N_EXPERTS, LANES), F32)]
    return specs, shapes


def _router_operands(w_router, b_router):
    wp = jnp.pad(w_router, ((0, 0), (0, LANES - N_EXPERTS)))
    wrh = wp.astype(BF16)
    wrl = (wp - wrh.astype(F32)).astype(BF16)
    rt = jnp.concatenate([jnp.triu(jnp.ones((TM, TM), F32), 1), jnp.ones((TM, LANES), F32)], axis=1).astype(BF16)
    return wrh, wrl, jnp.pad(b_router, (0, LANES - N_EXPERTS)).reshape(1, LANES), rt


def _router_specs():
    return [_const_spec((D, LANES)), _const_spec((D, LANES)), _const_spec((1, LANES)), _const_spec((TM, TM + LANES))]


def _post_attn(ctx, x, o_m, o_g, w_out, mods, g2, w_router, b_router, nct):
    b, t, _ = o_m.shape
    tok = lambda bb, i: (bb, i, 0)
    out_specs, out_shape = _post_out_specs(b, t)
    return pl.pallas_call(
        functools.partial(_post_attn_kernel, nct=nct),
        grid=(b, t // TM),
        in_specs=_stream_specs(nct) + [
                  pl.BlockSpec((1, TM, 512), tok), pl.BlockSpec((1, TM, 512), tok),
                  _const_spec((1024, D)),
                  pl.BlockSpec((1, 1, 6, D), lambda bb, i: (bb, jnp.where(i < nct, 1, 0), 0, 0)),
                  _const_spec((1, D))] + _router_specs(),
        out_specs=out_specs, out_shape=out_shape,
        scratch_shapes=[pltpu.VMEM((N_EXPERTS, LANES), F32)],
        compiler_params=_cparams(("arbitrary", "arbitrary")),
        name="post_attn",
    )(ctx, x, o_m, o_g, w_out.astype(BF16), mods, g2.reshape(1, D), *_router_operands(w_router, b_router))


def _route_tables(rtab, counts, n_tiles):
    e = rtab[0:TOP_K].astype(jnp.int32)
    rank = rtab[TOP_K:2 * TOP_K].astype(jnp.int32)
    counts = counts[:, 0].astype(jnp.int32)
    padded = (counts + TMOE - 1) // TMOE * TMOE
    pend = jnp.cumsum(padded)
    pstart = pend - padded
    experts = jnp.arange(N_EXPERTS, dtype=jnp.int32)
    first_live = pend - counts
    pos = rank + jnp.sum(jnp.where(e[..., None] == experts, first_live, 0), axis=-1)
    tile_start = jnp.arange(n_tiles, dtype=jnp.int32) * TMOE
    tile_expert = jnp.minimum(jnp.sum(tile_start[:, None] >= pend[None, :], axis=-1), N_EXPERTS - 1).astype(jnp.int32)
    mine = tile_expert[:, None] == experts
    live = tile_start + TMOE - jnp.sum(jnp.where(mine, first_live, 0), axis=-1)
    tile_valid = jnp.where(tile_start < pend[-1], jnp.clip(live, 0, TMOE), 0).astype(jnp.int32)
    prev_expert = jnp.concatenate([jnp.full((1,), -1, jnp.int32), tile_expert[:-1]])
    tile_first = ((tile_valid > 0) & (tile_expert != prev_expert)).astype(jnp.int32)
    later = (experts[None, :] > experts[:, None]) & (counts[None, :] > 0)
    next_of = jnp.min(jnp.where(later, experts[None, :], N_EXPERTS), axis=-1)
    next_of = jnp.where(next_of < N_EXPERTS, next_of, -1)
    tile_next = jnp.sum(jnp.where(mine, next_of, 0), axis=-1).astype(jnp.int32)
    return pos, (tile_expert, tile_valid, tile_first, tile_next)


def _sc_mesh():
    return plsc.VectorSubcoreMesh(core_axis_name="core", subcore_axis_name="subcore")


def _sc_scatter_rows(src, idx, n_out, nb_tok, nb_src, nb_plane, off):
    m = idx.shape[0]
    steps_per_k = m // SC_WINDOW // TOP_K

    def src_block(i):
        q = i % steps_per_k
        plane = q // (steps_per_k // 2)
        r = q % (steps_per_k // 2)
        return (plane * nb_plane + (r // nb_tok) * nb_src + off + r % nb_tok, 0)

    @pl.kernel(out_type=jax.ShapeDtypeStruct((n_out, SC_ROW_WORDS), src.dtype), mesh=_sc_mesh(), scratch_types=[])
    def k(src_hbm, i_hbm, o_hbm):
        def body(x_vmem, i_vmem):
            pltpu.sync_copy(x_vmem, o_hbm.at[i_vmem.at[0]])

        pltpu.emit_pipeline(
            body,
            grid=(m // SC_WINDOW,),
            in_specs=[pl.BlockSpec((SC_WINDOW, SC_ROW_WORDS), src_block),
                      pl.BlockSpec((1, SC_WINDOW), lambda i: (0, i))],
            out_specs=[],
            core_axis_name=("core", "subcore"),
            dimension_semantics=(pltpu.PARALLEL,),
        )(src_hbm, i_hbm)

    return k(src, idx.reshape(1, m))


def _sc_gather_rows(src, idx):
    m = idx.shape[0]

    @pl.kernel(out_type=jax.ShapeDtypeStruct((m, SC_ROW_WORDS), src.dtype), mesh=_sc_mesh(), scratch_types=[])
    def k(src_hbm, i_hbm, o_hbm):
        def body(i_vmem, o_vmem):
            pltpu.sync_copy(src_hbm.at[i_vmem.at[0]], o_vmem)

        pltpu.emit_pipeline(
            body,
            grid=(m // SC_WINDOW,),
            in_specs=[pl.BlockSpec((1, SC_WINDOW), lambda i: (0, i))],
            out_specs=[pl.BlockSpec((SC_WINDOW, SC_ROW_WORDS), lambda i: (i, 0))],
            core_axis_name=("core", "subcore"),
            dimension_semantics=(pltpu.PARALLEL,),
        )(i_hbm, o_hbm)

    return k(src, idx.reshape(1, m))


def _moe_kernel(te_ref, tv_ref, tf_ref, tn_ref, xs_ref, wgu_hbm, bgu_ref, wd_hbm, bd_ref, ys_ref,
                wgu_bf, wd_bf, wgu_st, wd_st, sems, *, layer):
    t = pl.program_id(0)

    def fetch(e):
        return (pltpu.make_async_copy(wgu_hbm.at[layer, e], wgu_st, sems.at[0]),
                pltpu.make_async_copy(wd_hbm.at[layer, e], wd_st, sems.at[1]))

    @pl.when(t == 0)
    def _():
        for cp in fetch(te_ref[0]):
            cp.start()

    @pl.when(tf_ref[t] == 1)
    def _():
        for cp in fetch(te_ref[t]):
            cp.wait()
        wgu_bf[...] = wgu_st[...].astype(BF16)
        wd_bf[...] = wd_st[...].astype(BF16)

        @pl.when(tn_ref[t] >= 0)
        def _():
            for cp in fetch(tn_ref[t]):
                cp.start()

    valid = tv_ref[t]
    half = TMOE // 2

    def experts(rows):
        r0 = TMOE - rows
        lo, hi = _unpack_bf16_pairs(jnp.concatenate([xs_ref[0, r0:TMOE, :], xs_ref[1, r0:TMOE, :]], axis=1))
        live = lax.broadcasted_iota(jnp.int32, (rows, 1), 0) >= rows - valid
        x = jnp.where(live, jnp.concatenate([lo, hi], axis=1), 0.0).astype(BF16)
        gu = jnp.dot(x, wgu_bf[...], preferred_element_type=F32) + bgu_ref[0, 0]
        gate = jnp.minimum(gu[:, :MOE_FF], SWIGLU_LIMIT)
        up = jnp.clip(gu[:, MOE_FF:], -SWIGLU_LIMIT, SWIGLU_LIMIT)
        act = (up + 1.0) * (gate * _sigmoid(SWIGLU_ALPHA * gate))
        y = jnp.dot(act.astype(BF16), wd_bf[...], preferred_element_type=F32) + bd_ref[0, 0]
        words = _pack_bf16_pairs(y)
        ys_ref[0, r0:TMOE, :] = words[:, 0:SC_ROW_WORDS]
        ys_ref[1, r0:TMOE, :] = words[:, SC_ROW_WORDS:2 * SC_ROW_WORDS]

    @pl.when(valid > half)
    def _():
        experts(TMOE)

    @pl.when((valid > 0) & (valid <= half))
    def _():
        experts(half)
        ys_ref[:, 0:half, :] = jnp.zeros((2, half, SC_ROW_WORDS), jnp.uint32)

    @pl.when(valid == 0)
    def _():
        ys_ref[...] = jnp.zeros_like(ys_ref)


def _moe_experts(xs, tiles, layer, w_gu, b_gu, w_d, b_d):
    n_rows = xs.shape[1]
    n_tiles = n_rows // TMOE
    depth = w_gu.shape[0]
    grid_spec = pltpu.PrefetchScalarGridSpec(
        num_scalar_prefetch=4,
        grid=(n_tiles,),
        in_specs=[pl.BlockSpec((2, TMOE, SC_ROW_WORDS), lambda t, te, tv, tf, tn: (0, t, 0)),
                  pl.BlockSpec(memory_space=pl.ANY),
                  pl.BlockSpec((1, 1, 1, 2 * MOE_FF), lambda t, te, tv, tf, tn: (layer, te[t], 0, 0)),
                  pl.BlockSpec(memory_space=pl.ANY),
                  pl.BlockSpec((1, 1, 1, D), lambda t, te, tv, tf, tn: (layer, te[t], 0, 0))],
        out_specs=pl.BlockSpec((2, TMOE, SC_ROW_WORDS), lambda t, te, tv, tf, tn: (0, t, 0)),
        scratch_shapes=[pltpu.VMEM((D, 2 * MOE_FF), BF16), pltpu.VMEM((MOE_FF, D), BF16),
                        pltpu.VMEM((D, 2 * MOE_FF), F32), pltpu.VMEM((MOE_FF, D), F32),
                        pltpu.SemaphoreType.DMA((2,))],
    )
    return pl.pallas_call(
        functools.partial(_moe_kernel, layer=layer),
        grid_spec=grid_spec,
        out_shape=jax.ShapeDtypeStruct((2, n_rows, SC_ROW_WORDS), jnp.uint32),
        compiler_params=_cparams(("arbitrary",)),
        name="moe_experts",
    )(*tiles, xs, w_gu, b_gu.reshape(depth, N_EXPERTS, 1, -1), w_d, b_d.reshape(depth, N_EXPERTS, 1, -1))


def _moe(hp, rtab, counts, layer, w_gu, b_gu, w_d, b_d, n_seq, row0):
    _, b, t, _ = hp.shape
    n_tok = b * n_seq
    n_tiles = -(-n_tok * TOP_K // TMOE) + N_EXPERTS
    n_rows = n_tiles * TMOE
    rtab = jnp.transpose(rtab[:, :, row0:row0 + n_seq], (1, 0, 2)).reshape(ROUTE_ROWS, n_tok)
    pos, tiles = _route_tables(rtab, counts, n_tiles)
    idx = (pos[:, None, :] + (jnp.arange(2, dtype=jnp.int32) * n_rows)[None, :, None]).reshape(-1)
    xs = _sc_scatter_rows(hp.reshape(2 * b * t, SC_ROW_WORDS), idx, 2 * n_rows, n_seq // SC_WINDOW, t // SC_WINDOW,
                          b * t // SC_WINDOW, row0 // SC_WINDOW)
    ys = _moe_experts(xs.reshape(2, n_rows, SC_ROW_WORDS), tiles, layer, w_gu, b_gu, w_d, b_d).reshape(2 * n_rows, SC_ROW_WORDS)
    idx = idx.reshape(TOP_K, 2, b, n_seq)
    nb = b // COMBINE_GROUPS if b % COMBINE_GROUPS == 0 else b
    return [_sc_gather_rows(ys, idx[:, :, b0:b0 + nb].reshape(-1)).reshape(TOP_K, 2, nb, n_seq, SC_ROW_WORDS)
            for b0 in range(0, b, nb)]


def _combine(g_ref, slab, rows=slice(None)):
    acc_lo = acc_hi = None
    for k in range(TOP_K):
        lo, hi = _unpack_bf16_pairs(jnp.concatenate([g_ref[k, 0, 0, rows, :], g_ref[k, 1, 0, rows, :]], axis=1))
        w = slab[:, 2 * TOP_K + k:2 * TOP_K + k + 1]
        acc_lo = w * lo if acc_lo is None else acc_lo + w * lo
        acc_hi = w * hi if acc_hi is None else acc_hi + w * hi
    return jnp.concatenate([acc_lo, acc_hi], axis=1)


def _ssm_in_kernel(x_ref, g_ref, slab_ref, modsp_ref, mods_ref, ng_ref, w_ref, *rest):
    x1_ref, z_ref, xbc_ref, dt_ref, v_ref, wbf_ref = rest[-6:]
    hi = SSM_D + SSM_XBC + SSM_HEADS

    @pl.when((pl.program_id(0) == 0) & (pl.program_id(1) == 0))
    def _():
        rows = 128
        for r0 in range(0, D, rows):
            r = slice(r0, r0 + rows)
            wbf_ref[r, 0:hi] = w_ref[0, r, 0:hi].astype(BF16)
            wbf_ref[r, hi:hi + LANES - SSM_HEADS] = jnp.zeros((rows, LANES - SSM_HEADS), BF16)
            wbf_ref[r, hi + LANES - SSM_HEADS:SSM_IN_AUG] = w_ref[0, r, hi:w_ref.shape[2]].astype(BF16)

    sh = mods_ref[0, 0, 0:1, :]
    sc = mods_ref[0, 0, 1:2, :]
    gate = modsp_ref[0, 0, 5:6, :]
    half = x_ref.shape[1] // 2
    for p in range(2):
        r = slice(p * half, (p + 1) * half)
        x = x_ref[0, r, :] + gate * _combine(g_ref, slab_ref[0, r, :], r)
        x1_ref[0, r, :] = x
        h = _rms(x, ng_ref[...]) * (1.0 + sc) + sh
        u = jnp.dot(h.astype(BF16), wbf_ref[...], preferred_element_type=F32)
        z_ref[0, r, :] = u[:, 0:1024]
        xbc_ref[0, r, :] = u[:, 1024:2560]
        dt_ref[0, r, :] = u[:, 2560:2688]
        v_ref[0, r, :] = u[:, 2688:3712] * _sigmoid(u[:, 3712:4736])


def _ssm_in(x_all, g_groups, slab, mods_prev, mods, ng, w_in, nct):
    b, t, _ = x_all.shape
    widths = (D, 1024, SSM_XBC, 128, 1024)
    outs, b0 = None, 0
    for g in g_groups:
        nb = g.shape[2]
        tok = lambda bb, i, b0=b0: (bb + b0, i, 0)
        modspec = pl.BlockSpec((1, 1, 6, D), lambda bb, i, b0=b0: (bb + b0, jnp.where(i < nct, 1, 0), 0, 0))
        in_specs = [pl.BlockSpec((1, TM, D), tok),
                    pl.BlockSpec((TOP_K, 2, 1, TM, SC_ROW_WORDS), lambda bb, i: (0, 0, bb, i, 0)),
                    pl.BlockSpec((1, TM, ROUTE_LANES), tok), modspec, modspec,
                    _const_spec((1, D)),
                    pl.BlockSpec((1,) + w_in.shape[1:], lambda bb, i: (0, 0, 0), pipeline_mode=pl.Buffered(1))]
        args = [x_all, g, slab, mods_prev, mods, ng.reshape(1, D), w_in]
        aliases = {}
        if outs is not None:
            aliases = {len(args) + j: j for j in range(len(outs))}
            in_specs += [pl.BlockSpec(memory_space=pl.ANY)] * len(outs)
            args += list(outs)
        outs = pl.pallas_call(
            _ssm_in_kernel,
            grid=(nb, t // TM),
            in_specs=in_specs,
            scratch_shapes=[pltpu.VMEM((D, SSM_IN_AUG), BF16)],
            out_specs=[pl.BlockSpec((1, TM, w), tok) for w in widths],
            out_shape=[jax.ShapeDtypeStruct((b, t, w), F32) for w in widths],
            input_output_aliases=aliases,
            compiler_params=_cparams(("arbitrary", "arbitrary")),
            name="ssm_in",
        )(*args)
        b0 += nb
    return outs


def _conv_kernel(x_ref, w_ref, b_ref, o_ref, pad_ref, *, taps, ctx_len, silu):
    t = x_ref.shape[1]
    ct = x_ref.shape[2]
    half = taps // 2
    zeros = jnp.zeros((HALO, ct), F32)
    pad_ref[0:HALO, :] = zeros
    pad_ref[HALO:HALO + ctx_len, :] = x_ref[0, 0:ctx_len, :]
    pad_ref[HALO + ctx_len:2 * HALO + ctx_len, :] = zeros
    pad_ref[2 * HALO + ctx_len:2 * HALO + t, :] = x_ref[0, ctx_len:t, :]
    pad_ref[2 * HALO + t:3 * HALO + t, :] = zeros
    w = w_ref[...]
    bias = b_ref[...]
    rows = CHUNK + 2 * HALO

    def segment(out_start, length, pad_start):
        def body(c, carry):
            base = pl.multiple_of(c * CHUNK, CHUNK)
            win = pad_ref[pl.ds(pad_start - HALO + base, rows), :]
            acc = jnp.broadcast_to(bias, (CHUNK, ct))
            for r in range(8):
                ks = [k for k in range(taps) if (HALO + k - half) % 8 == r]
                if not ks:
                    continue
                rolled = win if r == 0 else pltpu.roll(win, rows - r, 0)
                for k in ks:
                    off = HALO + k - half - r
                    acc = acc + w[k:k + 1, :] * rolled[off:off + CHUNK, :]
            if silu:
                acc = acc * _sigmoid(acc)
            o_ref[0, pl.ds(out_start + base, CHUNK), :] = acc
            return carry
        lax.fori_loop(0, length // CHUNK, body, 0)

    segment(0, ctx_len, HALO)
    segment(ctx_len, t - ctx_len, 2 * HALO + ctx_len)


def _depthwise_conv(x, w, bias, ctx_len, silu):
    b, t, c = x.shape
    taps = w.shape[0]
    ct = 256
    return pl.pallas_call(
        functools.partial(_conv_kernel, taps=taps, ctx_len=ctx_len, silu=silu),
        grid=(b, c // ct),
        in_specs=[pl.BlockSpec((1, t, ct), lambda bb, j: (bb, 0, j)),
                  pl.BlockSpec((taps, ct), lambda bb, j: (0, j)),
                  pl.BlockSpec((1, ct), lambda bb, j: (0, j))],
        out_specs=pl.BlockSpec((1, t, ct), lambda bb, j: (bb, 0, j)),
        out_shape=jax.ShapeDtypeStruct((b, t, c), F32),
        scratch_shapes=[pltpu.VMEM((t + 3 * HALO, ct), F32)],
        compiler_params=_cparams(("arbitrary", "arbitrary")),
        name=f"depthwise_conv{taps}",
    )(x, w, bias.reshape(1, c))


def _ssd_chunk(xbc, dt_raw, a_row, bias_row, emat, state_ref, d, reverse):
    L = CHUNK
    ri = lax.broadcasted_iota(jnp.int32, (L, L), 0)
    ci = lax.broadcasted_iota(jnp.int32, (L, L), 1)
    lane_lo = lax.broadcasted_iota(jnp.int32, (L, LANES), 1) < 64
    mask = (ci >= ri) if reverse else (ri >= ci)
    tri = mask.astype(F32)

    xdt_in = dt_raw + bias_row
    dt = jnp.maximum(xdt_in, 0.0) + jnp.log1p(jnp.exp(-jnp.abs(xdt_in)))
    da = dt * a_row
    cs = jnp.dot(tri, da, precision=HIGHEST, preferred_element_type=F32)
    cs_t = cs.T
    end = 0 if reverse else L - 1
    a_end = cs[end:end + 1, :]
    dth = dt.astype(BF16)
    dtl = (dt - dth.astype(F32)).astype(BF16)
    dtx = jnp.dot(dth, emat, preferred_element_type=F32) + jnp.dot(dtl, emat, preferred_element_type=F32)

    def lanes(v, hh):
        return jnp.broadcast_to(v[:, hh:hh + 1], (v.shape[0], LANES))

    ys = []
    for g in range(2):
        bm = xbc[:, SSM_D + g * SSM_STATE:SSM_D + (g + 1) * SSM_STATE].astype(BF16)
        cm = xbc[:, SSM_D + 2 * SSM_STATE + g * SSM_STATE:SSM_D + 2 * SSM_STATE + (g + 1) * SSM_STATE].astype(BF16)
        cb = lax.dot_general(cm, bm, (((1,), (1,)), ((), ())), preferred_element_type=F32)
        state = state_ref[d, g]
        y_off = jnp.dot(cm, state.astype(BF16), preferred_element_type=F32)
        xdd_blocks, sdec_blocks = [], []
        for p in range(4):
            hp = g * 4 + p
            gmats, e_cols, d_outs = [], [], []
            for j in range(2):
                hh = 2 * hp + j
                col = lanes(cs, hh)
                row = jnp.broadcast_to(cs_t[hh:hh + 1, :], (L, L))
                dec = jnp.exp(jnp.where(mask, col - row, -jnp.inf))
                gmats.append((cb * dec).astype(BF16))
                e_cols.append(jnp.exp(col))
                d_outs.append(jnp.exp(lanes(a_end, hh) - col))
            din = jnp.where(lane_lo, e_cols[0], e_cols[1])
            dout = jnp.where(lane_lo, d_outs[0], d_outs[1])
            xdt = xbc[:, hp * 128:(hp + 1) * 128] * dtx[:, hp * 128:(hp + 1) * 128]
            xdd_blocks.append((xdt * dout).astype(BF16))
            sdec_blocks.append(din[end:end + 1, :])
            xdt = xdt.astype(BF16)
            ya = jnp.dot(gmats[0], xdt, preferred_element_type=F32)
            yb = jnp.dot(gmats[1], xdt, preferred_element_type=F32)
            ys.append(jnp.where(lane_lo, ya, yb) + din * y_off[:, p * 128:(p + 1) * 128])
        upd = lax.dot_general(bm, jnp.concatenate(xdd_blocks, axis=1), (((0,), (0,)), ((), ())),
                              preferred_element_type=F32)
        state_ref[d, g] = state * jnp.concatenate(sdec_blocks, axis=1) + upd
    return jnp.concatenate(ys, axis=1)


def _ssd_kernel(xf_ref, xb_ref, dtf_ref, dtb_ref, alog_ref, bias_ref, emat_ref, yf_ref, yb_ref, state_ref):
    @pl.when(pl.program_id(1) == 0)
    def _():
        state_ref[...] = jnp.zeros_like(state_ref)

    a = -jnp.exp(alog_ref[...])
    bias = bias_ref[...]
    emat = emat_ref[...]
    yf_ref[0] = _ssd_chunk(xf_ref[0], dtf_ref[0], a[0:1, :], bias[0:1, :], emat, state_ref, 0, False)
    yb_ref[0] = _ssd_chunk(xb_ref[0], dtb_ref[0], a[1:2, :], bias[1:2, :], emat, state_ref, 1, True)


def _ssd(xbc, dt, a_log, dt_bias, ctx_len):
    b, t, _ = xbc.shape
    nc = t // CHUNK
    ncc = ctx_len // CHUNK

    def fwd(bb, j):
        return (bb, j, 0)

    def bwd(bb, j):
        return (bb, jnp.where(j < ncc, ncc - 1 - j, nc - 1 + ncc - j), 0)

    pad = lambda v: jnp.pad(v, ((0, 0), (0, 128 - SSM_HEADS)))
    emat = (jnp.arange(LANES)[:, None] == jnp.arange(SSM_D)[None, :] // SSM_HEAD_DIM).astype(BF16)
    return pl.pallas_call(
        _ssd_kernel,
        grid=(b, nc),
        in_specs=[pl.BlockSpec((1, CHUNK, SSM_XBC), fwd), pl.BlockSpec((1, CHUNK, SSM_XBC), bwd),
                  pl.BlockSpec((1, CHUNK, 128), fwd), pl.BlockSpec((1, CHUNK, 128), bwd),
                  _const_spec((2, 128)), _const_spec((2, 128)), _const_spec((LANES, SSM_D))],
        out_specs=[pl.BlockSpec((1, CHUNK, SSM_D), fwd), pl.BlockSpec((1, CHUNK, SSM_D), bwd)],
        out_shape=[jax.ShapeDtypeStruct((b, t, SSM_D), F32), jax.ShapeDtypeStruct((b, t, SSM_D), F32)],
        scratch_shapes=[pltpu.VMEM((2, 2, SSM_STATE, 512), F32)],
        compiler_params=_cparams(("arbitrary", "arbitrary")),
        name="ssd_scan",
    )(xbc, xbc, dt, dt, pad(a_log), pad(dt_bias), emat)


def _post_ssm_kernel(x_ref, yf_ref, yb_ref, xs_ref, z_ref, v_ref, dsk_ref, sg_ref, lng_ref, lnb_ref, wout_ref,
                     mods_ref, g2_ref, wrh_ref, wrl_ref, br_ref, rt_ref, x1_ref, hp_ref, slab_ref, rtab_ref, cnt_ref,
                     carry_ref, wbf_ref, *, nct):
    @pl.when((pl.program_id(0) == 0) & (pl.program_id(1) == 0))
    def _():
        for r0 in range(0, wbf_ref.shape[0], 256):
            wbf_ref[r0:r0 + 256, :] = wout_ref[0, r0:r0 + 256, :].astype(BF16)

    y = yf_ref[0] + yb_ref[0] + dsk_ref[...] * xs_ref[0]
    z = z_ref[0]
    y_ssm = _rms(y * (z * _sigmoid(z)), sg_ref[...])
    v = v_ref[0]
    mu = jnp.mean(v, axis=-1, keepdims=True)
    vc = v - mu
    ln = vc * lax.rsqrt(jnp.mean(vc * vc, axis=-1, keepdims=True) + EPS) * lng_ref[...] + lnb_ref[...]
    y_conv = ln * _sigmoid(ln)
    o = (jnp.dot(y_ssm.astype(BF16), wbf_ref[0:1024, :], preferred_element_type=F32)
         + jnp.dot(y_conv.astype(BF16), wbf_ref[1024:2048, :], preferred_element_type=F32))
    _residual_norm_router(x_ref[0], o, mods_ref, g2_ref, wrh_ref, wrl_ref, br_ref, rt_ref, x1_ref, hp_ref, slab_ref,
                          rtab_ref, cnt_ref, carry_ref, nct=nct, route_ctx=False)


def _post_ssm(x_all, yf, yb, xbc_act, z, v, d_skip, ssm_norm_g, ln_g, ln_b, w_out, mods, g2, w_router, b_router, nct):
    b, t, _ = x_all.shape
    tok = lambda bb, i: (bb, i, 0)
    dsk = jnp.repeat(d_skip[0] + d_skip[1], SSM_HEAD_DIM).reshape(1, SSM_D)
    out_specs, out_shape = _post_out_specs(b, t)
    row = lambda: pl.BlockSpec((1, TM, 1024), tok)
    return pl.pallas_call(
        functools.partial(_post_ssm_kernel, nct=nct),
        grid=(b, t // TM),
        in_specs=[row(), row(), row(), row(), row(), row(),
                  _const_spec((1, SSM_D)), _const_spec((1, SSM_D)), _const_spec((1, D)), _const_spec((1, D)),
                  pl.BlockSpec((1,) + w_out.shape[1:], lambda bb, i: (0, 0, 0), pipeline_mode=pl.Buffered(1)),
                  pl.BlockSpec((1, 1, 6, D), lambda bb, i: (bb, jnp.where(i < nct, 1, 0), 0, 0)),
                  _const_spec((1, D))] + _router_specs(),
        out_specs=out_specs, out_shape=out_shape,
        scratch_shapes=[pltpu.VMEM((N_EXPERTS, LANES), F32), pltpu.VMEM(w_out.shape[1:], BF16)],
        compiler_params=_cparams(("arbitrary", "arbitrary")),
        name="post_ssm",
    )(x_all, yf, yb, xbc_act, z, v, dsk, ssm_norm_g.reshape(1, -1), ln_g.reshape(1, -1), ln_b.reshape(1, -1),
      w_out, mods, g2.reshape(1, D), *_router_operands(w_router, b_router))


def _final_kernel(x_ref, g_ref, slab_ref, mods_ref, fg_ref, *rest):
    o_ref = rest[-1]
    x = x_ref[0] + mods_ref[0, 0, 5:6, :] * _combine(g_ref, slab_ref[0])
    o_ref[0] = _rms(x, fg_ref[...])


def _final(x_all, g_groups, slab, mods, final_g, nct):
    b = x_all.shape[0]
    s = g_groups[0].shape[3]
    out, b0 = None, 0
    for g in g_groups:
        nb = g.shape[2]
        in_specs = [pl.BlockSpec((1, TM, D), lambda bb, i, b0=b0: (bb + b0, i + nct, 0)),
                    pl.BlockSpec((TOP_K, 2, 1, TM, SC_ROW_WORDS), lambda bb, i: (0, 0, bb, i, 0)),
                    pl.BlockSpec((1, TM, ROUTE_LANES), lambda bb, i, b0=b0: (bb + b0, i + nct, 0)),
                    pl.BlockSpec((1, 1, 6, D), lambda bb, i, b0=b0: (bb + b0, 0, 0, 0)),
                    _const_spec((1, D))]
        args = [x_all, g, slab, mods, final_g.reshape(1, D)]
        aliases = {}
        if out is not None:
            in_specs.append(pl.BlockSpec(memory_space=pl.ANY))
            args.append(out)
            aliases = {len(args) - 1: 0}
        out = pl.pallas_call(
            _final_kernel,
            grid=(nb, s // TM),
            in_specs=in_specs,
            out_specs=pl.BlockSpec((1, TM, D), lambda bb, i, b0=b0: (bb + b0, i, 0)),
            out_shape=jax.ShapeDtypeStruct((b, s, D), F32),
            input_output_aliases=aliases,
            compiler_params=_cparams(("arbitrary", "arbitrary")),
            name="final_norm",
        )(*args)
        b0 += nb
    return out


def kernel(x, c, ctx, c_ctx, w_mod, b_mod, norm_g, attn_w_in, mla_g_cq, mla_w_uq, mla_g_ckv, mla_w_ukv, gqa_g_q, gqa_g_k, attn_w_out, ssm_w_in, ssm_conv_w, ssm_conv_b, ssm_a_log, ssm_dt_bias, ssm_d, ssm_norm_g, conf_dw_w, conf_dw_b, conf_ln_g, conf_ln_b, ssm_w_out, moe_w_router, moe_b_router, moe_w_gate_up, moe_b_gate_up, moe_w_down, moe_b_down, final_g):
    b, s, _ = x.shape
    ctx_len = ctx.shape[1]
    t = ctx_len + s
    assert ctx_len % TM == 0 and s % TM == 0 and s % GRID_W == 0
    nct = ctx_len // TM

    c_rows = jnp.concatenate([c, c_ctx[None, :], jnp.zeros((-(b + 1) % 8, D), F32)], axis=0)
    mod_all = _modulations(c_rows, w_mod, b_mod)
    mods = []
    for i in range(w_mod.shape[0]):
        lat = mod_all[i, :b].reshape(b, 1, 6, D)
        cm = jnp.broadcast_to(mod_all[i, b].reshape(1, 1, 6, D), (b, 1, 6, D))
        mods.append(jnp.concatenate([lat, cm], axis=1))

    tab = _rope_tables(s, ctx_len)
    qm, km, vm, qg, kg, vgs = _attn_in(ctx, x, mods[0], norm_g[0, 0], attn_w_in[0], mla_g_cq[0], mla_w_uq[0], mla_g_ckv[0],
                                       mla_w_ukv[0], gqa_g_q[0], gqa_g_k[0], tab, nct)
    o_m = _mla_attention(qm, km, vm, nct, ctx_len)
    o_g = _gqa_attention(qg, kg, vgs, nct, ctx_len)
    x_all, hp, slab0, rtab, counts = _post_attn(ctx, x, o_m, o_g, attn_w_out[0], mods[0], norm_g[0, 1], moe_w_router[0],
                                                moe_b_router[0], nct)
    g_all = _moe(hp, rtab, counts, 0, moe_w_gate_up, moe_b_gate_up, moe_w_down, moe_b_down, t, 0)

    x_all, z, xbc, dt, v = _ssm_in(x_all, g_all, slab0, mods[0], mods[1], norm_g[1, 0], ssm_w_in, nct)
    xbc_act = _depthwise_conv(xbc, ssm_conv_w[0], ssm_conv_b[0], ctx_len, True)
    v_conv = _depthwise_conv(v, conf_dw_w[0], conf_dw_b[0], ctx_len, False)
    yf, yb = _ssd(xbc_act, dt, ssm_a_log[0], ssm_dt_bias[0], ctx_len)
    x_all, hp, slab1, rtab, counts = _post_ssm(x_all, yf, yb, xbc_act, z, v_conv, ssm_d[0], ssm_norm_g[0], conf_ln_g[0],
                                               conf_ln_b[0], ssm_w_out, mods[1], norm_g[1, 1], moe_w_router[1],
                                               moe_b_router[1], nct)
    g_lat = _moe(hp, rtab, counts, 1, moe_w_gate_up, moe_b_gate_up, moe_w_down, moe_b_down, s, ctx_len)
    return _final(x_all, g_lat, slab1, mods[1], final_g, nct)
```

```python
import functools

import jax
import jax.numpy as jnp
from jax import lax
from jax.experimental import pallas as pl
from jax.experimental.pallas import tpu as pltpu
from jax.experimental.pallas import tpu_sc as plsc

F32 = jnp.float32
BF16 = jnp.bfloat16
HIGHEST = lax.Precision.HIGHEST

D = 1024
EPS = 1e-6
GRID_W = 64
ROPE_THETA = 10000.0
LOG2E = 1.4426950408889634

MLA_HEADS = 8
MLA_NOPE = 64
MLA_ROPE = 32
MLA_Q_LORA = 256
MLA_KV_LORA = 128
MLA_SCALE = (MLA_NOPE + MLA_ROPE) ** -0.5
GQA_HEADS = 8
GQA_HEAD_DIM = 64
GQA_SCALE = GQA_HEAD_DIM ** -0.5
ATTN_IN_AUG = 1280

SSM_HEADS = 16
SSM_HEAD_DIM = 64
SSM_STATE = 128
SSM_D = 1024
SSM_XBC = 1536
SSM_CONV = 5
CONF_K = 31
CHUNK = 128
SSM_IN_AUG = 1024 + 1536 + 128 + 2048

N_EXPERTS = 32
TOP_K = 4
MOE_FF = 1024
SWIGLU_LIMIT = 7.0
SWIGLU_ALPHA = 1.702

TM = 256
TMOE = 512
ROUTE_LANES = 128
ROUTE_ROWS = 16
SC_WINDOW = 128
SC_ROW_WORDS = 256
LANES = 128
HALO = 16
VMEM_LIMIT = 56 * 1024 * 1024


def _cparams(sem):
    return pltpu.CompilerParams(dimension_semantics=sem, vmem_limit_bytes=VMEM_LIMIT)


def _rms(x, g):
    return x * lax.rsqrt(jnp.mean(x * x, axis=-1, keepdims=True) + EPS) * g


def _sigmoid(x):
    return 1.0 / (1.0 + jnp.exp(-x))


def _const_spec(shape):
    n = len(shape)
    return pl.BlockSpec(shape, lambda *_: (0,) * n)


def _mod_kernel(c_ref, w_ref, b_ref, o_ref):
    c = c_ref[...]
    o_ref[0] = jnp.dot(c * _sigmoid(c), w_ref[0], precision=HIGHEST, preferred_element_type=F32) + b_ref[0]


def _modulations(c_rows, w_mod, b_mod):
    depth, _, n = w_mod.shape
    tn = 512
    r = c_rows.shape[0]
    return pl.pallas_call(
        _mod_kernel,
        grid=(depth, n // tn),
        in_specs=[pl.BlockSpec((r, D), lambda l, j: (0, 0)),
                  pl.BlockSpec((1, D, tn), lambda l, j: (l, 0, j)),
                  pl.BlockSpec((1, 1, tn), lambda l, j: (l, 0, j))],
        out_specs=pl.BlockSpec((1, r, tn), lambda l, j: (l, 0, j)),
        out_shape=jax.ShapeDtypeStruct((depth, r, n), F32),
        compiler_params=_cparams(("arbitrary", "arbitrary")),
        name="modulations",
    )(c_rows, w_mod, b_mod.reshape(depth, 1, n))


def _stream_tile(ctx_ref, x_ref, nct):
    return jnp.where(pl.program_id(1) < nct, ctx_ref[0], x_ref[0])


def _stream_specs(nct):
    return [pl.BlockSpec((1, TM, D), lambda bb, i: (bb, jnp.minimum(i, nct - 1), 0)),
            pl.BlockSpec((1, TM, D), lambda bb, i: (bb, jnp.maximum(i - nct, 0), 0))]


def _attn_in_kernel(ctx_ref, x_ref, mods_ref, ng_ref, win_ref, gcq_ref, wuq_ref, gckv_ref, wkv_ref, gq_ref, gk_ref, bd_ref,
                    tab_ref, qm_ref, km_ref, vm_ref, qg_ref, kg_ref, vgs_ref, *, nct):
    x = _stream_tile(ctx_ref, x_ref, nct)
    sh = mods_ref[0, 0, 0:1, :]
    sc = mods_ref[0, 0, 1:2, :]
    h = _rms(x, ng_ref[...]) * (1.0 + sc) + sh
    u = jnp.dot(h.astype(BF16), win_ref[...], preferred_element_type=F32)

    tab = tab_ref[...]
    cg, sg = tab[:, 0:128], tab[:, 128:256]
    cm, sm = tab[:, 256:384], tab[:, 384:512]
    lane = lax.broadcasted_iota(jnp.int32, (x.shape[0], LANES), 1)
    even = (lane & 1) == 0
    lo = lane < 64

    def rope(v, c, s):
        partner = jnp.where(even, pltpu.roll(v, LANES - 1, 1), pltpu.roll(v, 1, 1))
        return v * c + partner * s

    cq = _rms(u[:, 0:256], gcq_ref[...])
    qm = jnp.dot(cq.astype(BF16), wuq_ref[...], preferred_element_type=F32)
    for hd in range(MLA_HEADS):
        qm_ref[0, hd] = rope(qm[:, hd * 128:(hd + 1) * 128], cm, sm).astype(BF16)
    ckv = _rms(u[:, 256:384], gckv_ref[...])
    kv = jnp.dot(ckv.astype(BF16), wkv_ref[...], preferred_element_type=F32)
    kr = rope(u[:, 384:512], cm, sm)
    for hd in range(MLA_HEADS):
        km_ref[0, hd] = (kv[:, hd * 128:(hd + 1) * 128] + kr).astype(BF16)
    one = jnp.ones((x.shape[0], LANES), F32)
    for p in range(MLA_HEADS // 2):
        blk = kv[:, 1024 + p * 128:1024 + (p + 1) * 128]
        vm_ref[0, 2 * p] = jnp.where(lo, blk, one).astype(BF16)
        vm_ref[0, 2 * p + 1] = jnp.where(lo, one, blk).astype(BF16)

    qg = u[:, 512:1024]
    ms = jnp.dot((qg * qg).astype(BF16), bd_ref[...], preferred_element_type=F32)
    qg = qg * lax.rsqrt(ms + EPS) * gq_ref[...]
    zero = jnp.zeros((x.shape[0], LANES), F32)
    for p in range(GQA_HEADS // 2):
        blk = rope(qg[:, p * 128:(p + 1) * 128], cg, sg)
        swp = pltpu.roll(blk, 64, 1)
        if p < 2:
            qg_ref[0, 2 * p] = jnp.where(lo, blk, zero).astype(BF16)
            qg_ref[0, 2 * p + 1] = jnp.where(lo, swp, zero).astype(BF16)
        else:
            qg_ref[0, 2 * p] = jnp.where(lo, zero, swp).astype(BF16)
            qg_ref[0, 2 * p + 1] = jnp.where(lo, zero, blk).astype(BF16)
    kg = u[:, 1024:1152]
    msk = jnp.dot((kg * kg).astype(BF16), bd_ref[0:128, 0:128], preferred_element_type=F32)
    kg_ref[0] = rope(kg * lax.rsqrt(msk + EPS) * gk_ref[...], cg, sg).astype(BF16)
    vg = u[:, 1152:1280]
    vsw = pltpu.roll(vg, 64, 1)
    vgs_ref[0, 0] = jnp.where(lo, vg, one).astype(BF16)
    vgs_ref[0, 1] = jnp.where(lo, one, vsw).astype(BF16)
    vgs_ref[0, 2] = jnp.where(lo, vsw, one).astype(BF16)
    vgs_ref[0, 3] = jnp.where(lo, one, vg).astype(BF16)


def _rope_tables(seq, ctx_len):
    rows = seq // GRID_W
    row = jnp.broadcast_to(jnp.arange(rows, dtype=F32)[:, None], (rows, GRID_W)).reshape(-1)
    col = jnp.broadcast_to(jnp.arange(GRID_W, dtype=F32)[None, :], (rows, GRID_W)).reshape(-1)

    def interleaved(rot_dim):
        n_freq = rot_dim // 4
        inv_freq = ROPE_THETA ** (-jnp.arange(n_freq, dtype=F32) / n_freq)
        ang = jnp.concatenate([row[:, None] * inv_freq, col[:, None] * inv_freq], axis=-1)
        cos = jnp.repeat(jnp.cos(ang), 2, axis=-1)
        sin = jnp.repeat(jnp.sin(ang), 2, axis=-1) * jnp.tile(jnp.array([-1.0, 1.0], F32), rot_dim // 2)
        return cos, sin

    cg, sg = interleaved(GQA_HEAD_DIM)
    cg, sg = jnp.tile(cg, (1, 2)), jnp.tile(sg, (1, 2))
    cm32, sm32 = interleaved(MLA_ROPE)
    ones, zeros = jnp.ones((seq, 64), F32), jnp.zeros((seq, 64), F32)
    cm = jnp.concatenate([ones, cm32, ones[:, :32]], axis=-1)
    sm = jnp.concatenate([zeros, sm32, zeros[:, :32]], axis=-1)
    lat = jnp.concatenate([cg, sg, cm, sm], axis=-1)
    ident = jnp.concatenate([jnp.ones((ctx_len, 128), F32), jnp.zeros((ctx_len, 128), F32)] * 2, axis=-1)
    return jnp.concatenate([ident, lat], axis=0)


def _attn_in(ctx, x, mods, ng, w_in, g_cq, w_uq, g_ckv, w_ukv, g_q, g_k, tab, nct):
    b = x.shape[0]
    t = ctx.shape[1] + x.shape[1]
    o1, o2, o3 = MLA_Q_LORA, MLA_Q_LORA + MLA_KV_LORA, MLA_Q_LORA + MLA_KV_LORA + MLA_ROPE
    zc = lambda n: jnp.zeros((D, n), F32)
    w_aug = jnp.concatenate([w_in[:, :o2], zc(64), w_in[:, o2:o3], zc(32), w_in[:, o3:]], axis=1).astype(BF16)
    wuq = jnp.pad(w_uq.reshape(MLA_Q_LORA, MLA_HEADS, 96), ((0, 0), (0, 0), (0, 32))).reshape(MLA_Q_LORA, 1024).astype(BF16)
    wukv = w_ukv.reshape(MLA_KV_LORA, MLA_HEADS, 128)
    wk = jnp.pad(wukv[:, :, :64], ((0, 0), (0, 0), (0, 64))).reshape(MLA_KV_LORA, 1024)
    wkv = jnp.concatenate([wk, wukv[:, :, 64:].reshape(MLA_KV_LORA, 512)], axis=1).astype(BF16)
    gcq = (g_cq * (MLA_SCALE * LOG2E)).reshape(1, -1)
    gq = (jnp.tile(g_q, GQA_HEADS) * (GQA_SCALE * LOG2E)).reshape(1, -1)
    gk = jnp.tile(g_k, 2).reshape(1, -1)
    bd = jnp.kron(jnp.eye(GQA_HEADS, dtype=F32), jnp.full((64, 64), 1.0 / 64, F32)).astype(BF16)
    nt = t // TM
    tok = lambda bb, i: (bb, i, 0)
    hm = lambda bb, i: (bb, 0, i, 0)
    return pl.pallas_call(
        functools.partial(_attn_in_kernel, nct=nct),
        grid=(b, nt),
        in_specs=_stream_specs(nct) + [
                  pl.BlockSpec((1, 1, 6, D), lambda bb, i: (bb, jnp.where(i < nct, 1, 0), 0, 0)),
                  _const_spec((1, D)), _const_spec((D, ATTN_IN_AUG)), _const_spec((1, MLA_Q_LORA)),
                  _const_spec((MLA_Q_LORA, 1024)), _const_spec((1, MLA_KV_LORA)), _const_spec((MLA_KV_LORA, 1536)),
                  _const_spec((1, 512)), _const_spec((1, 128)), _const_spec((512, 512)),
                  pl.BlockSpec((TM, 512), lambda bb, i: (i, 0))],
        out_specs=[pl.BlockSpec((1, 8, TM, 128), hm), pl.BlockSpec((1, 8, TM, 128), hm),
                   pl.BlockSpec((1, 8, TM, 128), hm), pl.BlockSpec((1, 8, TM, 128), hm),
                   pl.BlockSpec((1, TM, 128), tok), pl.BlockSpec((1, 4, TM, 128), hm)],
        out_shape=[jax.ShapeDtypeStruct((b, 8, t, 128), BF16), jax.ShapeDtypeStruct((b, 8, t, 128), BF16),
                   jax.ShapeDtypeStruct((b, 8, t, 128), BF16), jax.ShapeDtypeStruct((b, 8, t, 128), BF16),
                   jax.ShapeDtypeStruct((b, t, 128), BF16), jax.ShapeDtypeStruct((b, 4, t, 128), BF16)],
        compiler_params=_cparams(("arbitrary", "arbitrary")),
        name="attn_in",
    )(ctx, x, mods, ng.reshape(1, D), w_aug, gcq, wuq, g_ckv.reshape(1, -1), wkv, gq, gk, bd, tab)


def _softmax_pv(q, k, v, sum_lane):
    s = lax.dot_general(q, k, (((1,), (1,)), ((), ())), preferred_element_type=F32)
    m = jnp.max(s, axis=-1, keepdims=True)
    o = jnp.dot(jnp.exp2(s - m).astype(BF16), v, preferred_element_type=F32)
    return o / o[:, sum_lane:sum_lane + 1]


def _mla_kernel(q_ref, k_ref, v_ref, o_ref, *, nct, ctx_len):
    i = pl.program_id(1)
    lo = lax.broadcasted_iota(jnp.int32, (q_ref.shape[2], LANES), 1) < 64

    def run(nk):
        for pr in range(MLA_HEADS // 2):
            oa = _softmax_pv(q_ref[0, 2 * pr], k_ref[0, 2 * pr, 0:nk, :], v_ref[0, 2 * pr, 0:nk, :], 64)
            ob = _softmax_pv(q_ref[0, 2 * pr + 1], k_ref[0, 2 * pr + 1, 0:nk, :], v_ref[0, 2 * pr + 1, 0:nk, :], 0)
            o_ref[0, :, pr * 128:(pr + 1) * 128] = jnp.where(lo, oa, ob).astype(BF16)

    @pl.when(i < nct)
    def _():
        run(ctx_len)

    @pl.when(i >= nct)
    def _():
        run(k_ref.shape[2])


def _mla_attention(qm, km, vm, nct, ctx_len):
    b, _, t, _ = qm.shape
    nt = t // TM
    return pl.pallas_call(
        functools.partial(_mla_kernel, nct=nct, ctx_len=ctx_len),
        grid=(b, nt),
        in_specs=[pl.BlockSpec((1, 8, TM, 128), lambda bb, i: (bb, 0, i, 0)),
                  pl.BlockSpec((1, 8, t, 128), lambda bb, i: (bb, 0, 0, 0)),
                  pl.BlockSpec((1, 8, t, 128), lambda bb, i: (bb, 0, 0, 0))],
        out_specs=pl.BlockSpec((1, TM, 512), lambda bb, i: (bb, i, 0)),
        out_shape=jax.ShapeDtypeStruct((b, t, 512), BF16),
        compiler_params=_cparams(("arbitrary", "arbitrary")),
        name="mla_attention",
    )(qm, km, vm)


def _gqa_kernel(q_ref, k_ref, v_ref, o_ref, *, nct, ctx_len):
    i = pl.program_id(1)
    lo = lax.broadcasted_iota(jnp.int32, (q_ref.shape[2], LANES), 1) < 64

    def run(nk):
        k = k_ref[0, 0:nk, :]
        for g in range(2):
            ve = v_ref[0, 2 * g, 0:nk, :]
            vo = v_ref[0, 2 * g + 1, 0:nk, :]
            for pr in range(2):
                hd = 4 * g + 2 * pr
                oa = _softmax_pv(q_ref[0, hd], k, ve, 64)
                ob = _softmax_pv(q_ref[0, hd + 1], k, vo, 0)
                o_ref[0, :, (hd // 2) * 128:(hd // 2 + 1) * 128] = jnp.where(lo, oa, ob).astype(BF16)

    @pl.when(i < nct)
    def _():
        run(ctx_len)

    @pl.when(i >= nct)
    def _():
        run(k_ref.shape[1])


def _gqa_attention(qg, kg, vgs, nct, ctx_len):
    b, _, t, _ = qg.shape
    nt = t // TM
    return pl.pallas_call(
        functools.partial(_gqa_kernel, nct=nct, ctx_len=ctx_len),
        grid=(b, nt),
        in_specs=[pl.BlockSpec((1, 8, TM, 128), lambda bb, i: (bb, 0, i, 0)),
                  pl.BlockSpec((1, t, 128), lambda bb, i: (bb, 0, 0)),
                  pl.BlockSpec((1, 4, t, 128), lambda bb, i: (bb, 0, 0, 0))],
        out_specs=pl.BlockSpec((1, TM, 512), lambda bb, i: (bb, i, 0)),
        out_shape=jax.ShapeDtypeStruct((b, t, 512), BF16),
        compiler_params=_cparams(("arbitrary", "arbitrary")),
        name="gqa_attention",
    )(qg, kg, vgs)


def _pack_bf16_pairs(h):
    half = h.shape[1] // 2
    lo = pltpu.bitcast(h[:, :half].astype(BF16).astype(F32), jnp.uint32) >> 16
    hi = pltpu.bitcast(h[:, half:].astype(BF16).astype(F32), jnp.uint32) & jnp.uint32(0xFFFF0000)
    return lo | hi


def _unpack_bf16_pairs(w):
    return pltpu.bitcast(w << 16, F32), pltpu.bitcast(w & jnp.uint32(0xFFFF0000), F32)


def _store_planes(ref, words):
    ref[0, 0] = words[:, 0:SC_ROW_WORDS]
    ref[1, 0] = words[:, SC_ROW_WORDS:2 * SC_ROW_WORDS]


def _route_tile(lt, rt_ref, carry_ref, counted):
    n_e, rows = lt.shape
    eidx = lax.broadcasted_iota(jnp.int32, (n_e, rows), 0).astype(F32)
    work = lt
    vals, ids, hots = [], [], []
    for _ in range(TOP_K):
        m = jnp.max(work, axis=0, keepdims=True)
        idx = jnp.min(jnp.where(work == m, eidx, float(N_EXPERTS)), axis=0, keepdims=True)
        hot = eidx == idx
        vals.append(m)
        ids.append(idx)
        hots.append(hot)
        work = jnp.where(hot, -jnp.inf, work)
    exps = [jnp.exp(v - vals[0]) for v in vals]
    den = exps[0] + exps[1] + exps[2] + exps[3]
    mask = jnp.where(hots[0] | hots[1] | hots[2] | hots[3], 1.0, 0.0).astype(BF16)
    cum = jnp.dot(mask, rt_ref[...], preferred_element_type=F32)
    carry = carry_ref[...]
    before = cum[:, 0:rows] + jnp.concatenate([carry] * (rows // LANES), axis=1)
    ranks = [jnp.sum(jnp.where(h, before, 0.0), axis=0, keepdims=True) for h in hots]
    carry_ref[...] = carry + counted * cum[:, rows:rows + LANES]
    return jnp.concatenate(ids + ranks + [e / den for e in exps] + [jnp.zeros((ROUTE_LANES - 3 * TOP_K, rows), F32)],
                           axis=0)


def _residual_norm_router(x, o, mods_ref, g2_ref, wrh_ref, wrl_ref, br_ref, rt_ref, x1_ref, hp_ref, slab_ref,
                          rtab_ref, cnt_ref, carry_ref, *, nct, route_ctx):
    first = (pl.program_id(0) == 0) & (pl.program_id(1) == 0)

    @pl.when(first)
    def _():
        carry_ref[...] = jnp.zeros_like(carry_ref)

    g1 = mods_ref[0, 0, 2:3, :]
    sh2 = mods_ref[0, 0, 3:4, :]
    sc2 = mods_ref[0, 0, 4:5, :]
    x1 = x + g1 * o
    h2 = _rms(x1, g2_ref[...]) * (1.0 + sc2) + sh2
    x1_ref[0] = x1
    _store_planes(hp_ref, _pack_bf16_pairs(h2))
    hh = h2.astype(BF16)
    hl = (h2 - hh.astype(F32)).astype(BF16)
    logits = (jnp.dot(hh, wrh_ref[...], preferred_element_type=F32) + jnp.dot(hl, wrh_ref[...], preferred_element_type=F32)
              + jnp.dot(hh, wrl_ref[...], preferred_element_type=F32) + br_ref[...])
    counted = 1.0 if route_ctx else jnp.where(pl.program_id(1) >= nct, 1.0, 0.0)
    slab_t = _route_tile(logits.T[0:N_EXPERTS, :], rt_ref, carry_ref, counted)
    slab_ref[0] = slab_t.T
    rtab_ref[0] = slab_t[0:ROUTE_ROWS, :]
    cnt_ref[...] = carry_ref[...]


def _post_attn_kernel(ctx_ref, x_ref, om_ref, og_ref, wout_ref, mods_ref, g2_ref, wrh_ref, wrl_ref, br_ref, rt_ref,
                      x1_ref, hp_ref, slab_ref, rtab_ref, cnt_ref, carry_ref, *, nct):
    o = (jnp.dot(om_ref[0], wout_ref[0:512, :], preferred_element_type=F32)
         + jnp.dot(og_ref[0], wout_ref[512:1024, :], preferred_element_type=F32))
    _residual_norm_router(_stream_tile(ctx_ref, x_ref, nct), o, mods_ref, g2_ref, wrh_ref, wrl_ref, br_ref, rt_ref,
                          x1_ref, hp_ref, slab_ref, rtab_ref, cnt_ref, carry_ref, nct=nct, route_ctx=True)


def _post_out_specs(b, t):
    tok = lambda bb, i: (bb, i, 0)
    specs = [pl.BlockSpec((1, TM, D), tok), pl.BlockSpec((2, 1, TM, SC_ROW_WORDS), lambda bb, i: (0, bb, i, 0)),
             pl.BlockSpec((1, TM, ROUTE_LANES), tok), pl.BlockSpec((1, ROUTE_ROWS, TM), lambda bb, i: (bb, 0, i)),
             _const_spec((N_EXPERTS, LANES))]
    shapes = [jax.ShapeDtypeStruct((b, t, D), F32), jax.ShapeDtypeStruct((2, b, t, SC_ROW_WORDS), jnp.uint32),
              jax.ShapeDtypeStruct((b, t, ROUTE_LANES), F32), jax.ShapeDtypeStruct((b, ROUTE_ROWS, t), F32),
              jax.ShapeDtypeStruct((N_EXPERTS, LANES), F32)]
    return specs, shapes


def _router_operands(w_router, b_router):
    wp = jnp.pad(w_router, ((0, 0), (0, LANES - N_EXPERTS)))
    wrh = wp.astype(BF16)
    wrl = (wp - wrh.astype(F32)).astype(BF16)
    rt = jnp.concatenate([jnp.triu(jnp.ones((TM, TM), F32), 1), jnp.ones((TM, LANES), F32)], axis=1).astype(BF16)
    return wrh, wrl, jnp.pad(b_router, (0, LANES - N_EXPERTS)).reshape(1, LANES), rt


def _router_specs():
    return [_const_spec((D, LANES)), _const_spec((D, LANES)), _const_spec((1, LANES)), _const_spec((TM, TM + LANES))]


def _post_attn(ctx, x, o_m, o_g, w_out, mods, g2, w_router, b_router, nct):
    b, t, _ = o_m.shape
    tok = lambda bb, i: (bb, i, 0)
    out_specs, out_shape = _post_out_specs(b, t)
    return pl.pallas_call(
        functools.partial(_post_attn_kernel, nct=nct),
        grid=(b, t // TM),
        in_specs=_stream_specs(nct) + [
                  pl.BlockSpec((1, TM, 512), tok), pl.BlockSpec((1, TM, 512), tok),
                  _const_spec((1024, D)),
                  pl.BlockSpec((1, 1, 6, D), lambda bb, i: (bb, jnp.where(i < nct, 1, 0), 0, 0)),
                  _const_spec((1, D))] + _router_specs(),
        out_specs=out_specs, out_shape=out_shape,
        scratch_shapes=[pltpu.VMEM((N_EXPERTS, LANES), F32)],
        compiler_params=_cparams(("arbitrary", "arbitrary")),
        name="post_attn",
    )(ctx, x, o_m, o_g, w_out.astype(BF16), mods, g2.reshape(1, D), *_router_operands(w_router, b_router))


def _route_tables(rtab, counts, n_tiles):
    e = rtab[0:TOP_K].astype(jnp.int32)
    rank = rtab[TOP_K:2 * TOP_K].astype(jnp.int32)
    counts = counts[:, 0].astype(jnp.int32)
    padded = (counts + TMOE - 1) // TMOE * TMOE
    pend = jnp.cumsum(padded)
    pstart = pend - padded
    experts = jnp.arange(N_EXPERTS, dtype=jnp.int32)
    first_live = pend - counts
    pos = rank + jnp.sum(jnp.where(e[..., None] == experts, first_live, 0), axis=-1)
    tile_start = jnp.arange(n_tiles, dtype=jnp.int32) * TMOE
    tile_expert = jnp.minimum(jnp.sum(tile_start[:, None] >= pend[None, :], axis=-1), N_EXPERTS - 1).astype(jnp.int32)
    mine = tile_expert[:, None] == experts
    live = tile_start + TMOE - jnp.sum(jnp.where(mine, first_live, 0), axis=-1)
    tile_valid = jnp.where(tile_start < pend[-1], jnp.clip(live, 0, TMOE), 0).astype(jnp.int32)
    prev_expert = jnp.concatenate([jnp.full((1,), -1, jnp.int32), tile_expert[:-1]])
    tile_first = ((tile_valid > 0) & (tile_expert != prev_expert)).astype(jnp.int32)
    later = (experts[None, :] > experts[:, None]) & (counts[None, :] > 0)
    next_of = jnp.min(jnp.where(later, experts[None, :], N_EXPERTS), axis=-1)
    next_of = jnp.where(next_of < N_EXPERTS, next_of, -1)
    tile_next = jnp.sum(jnp.where(mine, next_of, 0), axis=-1).astype(jnp.int32)
    return pos, (tile_expert, tile_valid, tile_first, tile_next)


def _sc_mesh():
    return plsc.VectorSubcoreMesh(core_axis_name="core", subcore_axis_name="subcore")


def _sc_scatter_rows(src, idx, n_out, nb_tok, nb_src, nb_plane, off):
    m = idx.shape[0]
    steps_per_k = m // SC_WINDOW // TOP_K

    def src_block(i):
        q = i % steps_per_k
        plane = q // (steps_per_k // 2)
        r = q % (steps_per_k // 2)
        return (plane * nb_plane + (r // nb_tok) * nb_src + off + r % nb_tok, 0)

    @pl.kernel(out_type=jax.ShapeDtypeStruct((n_out, SC_ROW_WORDS), src.dtype), mesh=_sc_mesh(), scratch_types=[])
    def k(src_hbm, i_hbm, o_hbm):
        def body(x_vmem, i_vmem):
            pltpu.sync_copy(x_vmem, o_hbm.at[i_vmem.at[0]])

        pltpu.emit_pipeline(
            body,
            grid=(m // SC_WINDOW,),
            in_specs=[pl.BlockSpec((SC_WINDOW, SC_ROW_WORDS), src_block),
                      pl.BlockSpec((1, SC_WINDOW), lambda i: (0, i))],
            out_specs=[],
            core_axis_name=("core", "subcore"),
            dimension_semantics=(pltpu.PARALLEL,),
        )(src_hbm, i_hbm)

    return k(src, idx.reshape(1, m))


def _sc_gather_rows(src, idx):
    m = idx.shape[0]

    @pl.kernel(out_type=jax.ShapeDtypeStruct((m, SC_ROW_WORDS), src.dtype), mesh=_sc_mesh(), scratch_types=[])
    def k(src_hbm, i_hbm, o_hbm):
        def body(i_vmem, o_vmem):
            pltpu.sync_copy(src_hbm.at[i_vmem.at[0]], o_vmem)

        pltpu.emit_pipeline(
            body,
            grid=(m // SC_WINDOW,),
            in_specs=[pl.BlockSpec((1, SC_WINDOW), lambda i: (0, i))],
            out_specs=[pl.BlockSpec((SC_WINDOW, SC_ROW_WORDS), lambda i: (i, 0))],
            core_axis_name=("core", "subcore"),
            dimension_semantics=(pltpu.PARALLEL,),
        )(i_hbm, o_hbm)

    return k(src, idx.reshape(1, m))


def _moe_kernel(te_ref, tv_ref, tf_ref, tn_ref, xs_ref, wgu_hbm, bgu_ref, wd_hbm, bd_ref, ys_ref,
                wgu_bf, wd_bf, wgu_st, wd_st, sems, *, layer):
    t = pl.program_id(0)

    def fetch(e):
        return (pltpu.make_async_copy(wgu_hbm.at[layer, e], wgu_st, sems.at[0]),
                pltpu.make_async_copy(wd_hbm.at[layer, e], wd_st, sems.at[1]))

    @pl.when(t == 0)
    def _():
        for cp in fetch(te_ref[0]):
            cp.start()

    @pl.when(tf_ref[t] == 1)
    def _():
        for cp in fetch(te_ref[t]):
            cp.wait()
        wgu_bf[...] = wgu_st[...].astype(BF16)
        wd_bf[...] = wd_st[...].astype(BF16)

        @pl.when(tn_ref[t] >= 0)
        def _():
            for cp in fetch(tn_ref[t]):
                cp.start()

    valid = tv_ref[t]
    half = TMOE // 2

    def experts(rows):
        r0 = TMOE - rows
        lo, hi = _unpack_bf16_pairs(jnp.concatenate([xs_ref[0, r0:TMOE, :], xs_ref[1, r0:TMOE, :]], axis=1))
        live = lax.broadcasted_iota(jnp.int32, (rows, 1), 0) >= rows - valid
        x = jnp.where(live, jnp.concatenate([lo, hi], axis=1), 0.0).astype(BF16)
        gu = jnp.dot(x, wgu_bf[...], preferred_element_type=F32) + bgu_ref[0, 0]
        gate = jnp.minimum(gu[:, :MOE_FF], SWIGLU_LIMIT)
        up = jnp.clip(gu[:, MOE_FF:], -SWIGLU_LIMIT, SWIGLU_LIMIT)
        act = (up + 1.0) * (gate * _sigmoid(SWIGLU_ALPHA * gate))
        y = jnp.dot(act.astype(BF16), wd_bf[...], preferred_element_type=F32) + bd_ref[0, 0]
        words = _pack_bf16_pairs(y)
        ys_ref[0, r0:TMOE, :] = words[:, 0:SC_ROW_WORDS]
        ys_ref[1, r0:TMOE, :] = words[:, SC_ROW_WORDS:2 * SC_ROW_WORDS]

    @pl.when(valid > half)
    def _():
        experts(TMOE)

    @pl.when((valid > 0) & (valid <= half))
    def _():
        experts(half)
        ys_ref[:, 0:half, :] = jnp.zeros((2, half, SC_ROW_WORDS), jnp.uint32)

    @pl.when(valid == 0)
    def _():
        ys_ref[...] = jnp.zeros_like(ys_ref)


def _moe_experts(xs, tiles, layer, w_gu, b_gu, w_d, b_d):
    n_rows = xs.shape[1]
    n_tiles = n_rows // TMOE
    depth = w_gu.shape[0]
    grid_spec = pltpu.PrefetchScalarGridSpec(
        num_scalar_prefetch=4,
        grid=(n_tiles,),
        in_specs=[pl.BlockSpec((2, TMOE, SC_ROW_WORDS), lambda t, te, tv, tf, tn: (0, t, 0)),
                  pl.BlockSpec(memory_space=pl.ANY),
                  pl.BlockSpec((1, 1, 1, 2 * MOE_FF), lambda t, te, tv, tf, tn: (layer, te[t], 0, 0)),
                  pl.BlockSpec(memory_space=pl.ANY),
                  pl.BlockSpec((1, 1, 1, D), lambda t, te, tv, tf, tn: (layer, te[t], 0, 0))],
        out_specs=pl.BlockSpec((2, TMOE, SC_ROW_WORDS), lambda t, te, tv, tf, tn: (0, t, 0)),
        scratch_shapes=[pltpu.VMEM((D, 2 * MOE_FF), BF16), pltpu.VMEM((MOE_FF, D), BF16),
                        pltpu.VMEM((D, 2 * MOE_FF), F32), pltpu.VMEM((MOE_FF, D), F32),
                        pltpu.SemaphoreType.DMA((2,))],
    )
    return pl.pallas_call(
        functools.partial(_moe_kernel, layer=layer),
        grid_spec=grid_spec,
        out_shape=jax.ShapeDtypeStruct((2, n_rows, SC_ROW_WORDS), jnp.uint32),
        compiler_params=_cparams(("arbitrary",)),
        name="moe_experts",
    )(*tiles, xs, w_gu, b_gu.reshape(depth, N_EXPERTS, 1, -1), w_d, b_d.reshape(depth, N_EXPERTS, 1, -1))


def _moe(hp, rtab, counts, layer, w_gu, b_gu, w_d, b_d, n_seq, row0):
    _, b, t, _ = hp.shape
    n_tok = b * n_seq
    n_tiles = -(-n_tok * TOP_K // TMOE) + N_EXPERTS
    n_rows = n_tiles * TMOE
    rtab = jnp.transpose(rtab[:, :, row0:row0 + n_seq], (1, 0, 2)).reshape(ROUTE_ROWS, n_tok)
    pos, tiles = _route_tables(rtab, counts, n_tiles)
    idx = (pos[:, None, :] + (jnp.arange(2, dtype=jnp.int32) * n_rows)[None, :, None]).reshape(-1)
    xs = _sc_scatter_rows(hp.reshape(2 * b * t, SC_ROW_WORDS), idx, 2 * n_rows, n_seq // SC_WINDOW, t // SC_WINDOW,
                          b * t // SC_WINDOW, row0 // SC_WINDOW)
    ys = _moe_experts(xs.reshape(2, n_rows, SC_ROW_WORDS), tiles, layer, w_gu, b_gu, w_d, b_d)
    g = _sc_gather_rows(ys.reshape(2 * n_rows, SC_ROW_WORDS), idx)
    return g.reshape(TOP_K, 2, b, n_seq, SC_ROW_WORDS)


def _combine(g_ref, slab, rows=slice(None)):
    acc_lo = acc_hi = None
    for k in range(TOP_K):
        lo, hi = _unpack_bf16_pairs(jnp.concatenate([g_ref[k, 0, 0, rows, :], g_ref[k, 1, 0, rows, :]], axis=1))
        w = slab[:, 2 * TOP_K + k:2 * TOP_K + k + 1]
        acc_lo = w * lo if acc_lo is None else acc_lo + w * lo
        acc_hi = w * hi if acc_hi is None else acc_hi + w * hi
    return jnp.concatenate([acc_lo, acc_hi], axis=1)


def _ssm_in_kernel(x_ref, g_ref, slab_ref, modsp_ref, mods_ref, ng_ref, w_ref, x1_ref, z_ref, xbc_ref, dt_ref, v_ref,
                   wbf_ref):
    hi = SSM_D + SSM_XBC + SSM_HEADS

    @pl.when((pl.program_id(0) == 0) & (pl.program_id(1) == 0))
    def _():
        def put(col0, block):
            wbf_ref[:, col0:col0 + LANES] = block.T.astype(BF16)

        lo = hi - SSM_HEADS
        for c0 in range(0, lo, LANES):
            put(c0, w_ref[0, c0:c0 + LANES, :])
        put(lo, jnp.concatenate([w_ref[0, lo:hi, :], jnp.zeros((LANES - SSM_HEADS, D), F32)], axis=0))
        for c0 in range(0, w_ref.shape[1] - hi, LANES):
            put(lo + LANES + c0, w_ref[0, hi + c0:hi + c0 + LANES, :])

    sh = mods_ref[0, 0, 0:1, :]
    sc = mods_ref[0, 0, 1:2, :]
    gate = modsp_ref[0, 0, 5:6, :]
    half = x_ref.shape[1] // 2
    for p in range(2):
        r = slice(p * half, (p + 1) * half)
        x = x_ref[0, r, :] + gate * _combine(g_ref, slab_ref[0, r, :], r)
        x1_ref[0, r, :] = x
        h = _rms(x, ng_ref[...]) * (1.0 + sc) + sh
        u = jnp.dot(h.astype(BF16), wbf_ref[...], preferred_element_type=F32)
        z_ref[0, r, :] = u[:, 0:1024]
        xbc_ref[0, r, :] = u[:, 1024:2560]
        dt_ref[0, r, :] = u[:, 2560:2688]
        v_ref[0, r, :] = u[:, 2688:3712] * _sigmoid(u[:, 3712:4736])


def _ssm_in(x_all, g_all, slab, mods_prev, mods, ng, w_in, nct):
    b, t, _ = x_all.shape
    w_t = jnp.swapaxes(w_in, 1, 2)
    tok = lambda bb, i: (bb, i, 0)
    modspec = pl.BlockSpec((1, 1, 6, D), lambda bb, i: (bb, jnp.where(i < nct, 1, 0), 0, 0))
    return pl.pallas_call(
        _ssm_in_kernel,
        grid=(b, t // TM),
        in_specs=[pl.BlockSpec((1, TM, D), tok),
                  pl.BlockSpec((TOP_K, 2, 1, TM, SC_ROW_WORDS), lambda bb, i: (0, 0, bb, i, 0)),
                  pl.BlockSpec((1, TM, ROUTE_LANES), tok), modspec, modspec,
                  _const_spec((1, D)),
                  pl.BlockSpec((1,) + w_t.shape[1:], lambda bb, i: (0, 0, 0), pipeline_mode=pl.Buffered(1))],
        scratch_shapes=[pltpu.VMEM((D, SSM_IN_AUG), BF16)],
        out_specs=[pl.BlockSpec((1, TM, D), tok), pl.BlockSpec((1, TM, 1024), tok), pl.BlockSpec((1, TM, SSM_XBC), tok),
                   pl.BlockSpec((1, TM, 128), tok), pl.BlockSpec((1, TM, 1024), tok)],
        out_shape=[jax.ShapeDtypeStruct((b, t, D), F32), jax.ShapeDtypeStruct((b, t, 1024), F32),
                   jax.ShapeDtypeStruct((b, t, SSM_XBC), F32), jax.ShapeDtypeStruct((b, t, 128), F32),
                   jax.ShapeDtypeStruct((b, t, 1024), F32)],
        compiler_params=_cparams(("arbitrary", "arbitrary")),
        name="ssm_in",
    )(x_all, g_all, slab, mods_prev, mods, ng.reshape(1, D), w_t)


def _conv_kernel(x_ref, w_ref, b_ref, o_ref, pad_ref, *, taps, ctx_len, silu):
    t = x_ref.shape[1]
    ct = x_ref.shape[2]
    half = taps // 2
    zeros = jnp.zeros((HALO, ct), F32)
    pad_ref[0:HALO, :] = zeros
    pad_ref[HALO:HALO + ctx_len, :] = x_ref[0, 0:ctx_len, :]
    pad_ref[HALO + ctx_len:2 * HALO + ctx_len, :] = zeros
    pad_ref[2 * HALO + ctx_len:2 * HALO + t, :] = x_ref[0, ctx_len:t, :]
    pad_ref[2 * HALO + t:3 * HALO + t, :] = zeros
    w = w_ref[...]
    bias = b_ref[...]
    rows = CHUNK + 2 * HALO

    def segment(out_start, length, pad_start):
        def body(c, carry):
            base = pl.multiple_of(c * CHUNK, CHUNK)
            win = pad_ref[pl.ds(pad_start - HALO + base, rows), :]
            acc = jnp.broadcast_to(bias, (CHUNK, ct))
            for r in range(8):
                ks = [k for k in range(taps) if (HALO + k - half) % 8 == r]
                if not ks:
                    continue
                rolled = win if r == 0 else pltpu.roll(win, rows - r, 0)
                for k in ks:
                    off = HALO + k - half - r
                    acc = acc + w[k:k + 1, :] * rolled[off:off + CHUNK, :]
            if silu:
                acc = acc * _sigmoid(acc)
            o_ref[0, pl.ds(out_start + base, CHUNK), :] = acc
            return carry
        lax.fori_loop(0, length // CHUNK, body, 0)

    segment(0, ctx_len, HALO)
    segment(ctx_len, t - ctx_len, 2 * HALO + ctx_len)


def _depthwise_conv(x, w, bias, ctx_len, silu):
    b, t, c = x.shape
    taps = w.shape[0]
    ct = 256
    return pl.pallas_call(
        functools.partial(_conv_kernel, taps=taps, ctx_len=ctx_len, silu=silu),
        grid=(b, c // ct),
        in_specs=[pl.BlockSpec((1, t, ct), lambda bb, j: (bb, 0, j)),
                  pl.BlockSpec((taps, ct), lambda bb, j: (0, j)),
                  pl.BlockSpec((1, ct), lambda bb, j: (0, j))],
        out_specs=pl.BlockSpec((1, t, ct), lambda bb, j: (bb, 0, j)),
        out_shape=jax.ShapeDtypeStruct((b, t, c), F32),
        scratch_shapes=[pltpu.VMEM((t + 3 * HALO, ct), F32)],
        compiler_params=_cparams(("arbitrary", "arbitrary")),
        name=f"depthwise_conv{taps}",
    )(x, w, bias.reshape(1, c))


def _ssd_chunk(xbc, dt_raw, a_row, bias_row, emat, state_ref, d, reverse):
    L = CHUNK
    ri = lax.broadcasted_iota(jnp.int32, (L, L), 0)
    ci = lax.broadcasted_iota(jnp.int32, (L, L), 1)
    lane_lo = lax.broadcasted_iota(jnp.int32, (L, LANES), 1) < 64
    mask = (ci >= ri) if reverse else (ri >= ci)
    tri = mask.astype(F32)

    xdt_in = dt_raw + bias_row
    dt = jnp.maximum(xdt_in, 0.0) + jnp.log1p(jnp.exp(-jnp.abs(xdt_in)))
    da = dt * a_row
    cs = jnp.dot(tri, da, precision=HIGHEST, preferred_element_type=F32)
    cs_t = cs.T
    end = 0 if reverse else L - 1
    a_end = cs[end:end + 1, :]
    dth = dt.astype(BF16)
    dtl = (dt - dth.astype(F32)).astype(BF16)
    dtx = jnp.dot(dth, emat, preferred_element_type=F32) + jnp.dot(dtl, emat, preferred_element_type=F32)

    def lanes(v, hh):
        return jnp.broadcast_to(v[:, hh:hh + 1], (v.shape[0], LANES))

    ys = []
    for g in range(2):
        bm = xbc[:, SSM_D + g * SSM_STATE:SSM_D + (g + 1) * SSM_STATE].astype(BF16)
        cm = xbc[:, SSM_D + 2 * SSM_STATE + g * SSM_STATE:SSM_D + 2 * SSM_STATE + (g + 1) * SSM_STATE].astype(BF16)
        cb = lax.dot_general(cm, bm, (((1,), (1,)), ((), ())), preferred_element_type=F32)
        state = state_ref[d, g]
        y_off = jnp.dot(cm, state.astype(BF16), preferred_element_type=F32)
        xdd_blocks, sdec_blocks = [], []
        for p in range(4):
            hp = g * 4 + p
            gmats, e_cols, d_outs = [], [], []
            for j in range(2):
                hh = 2 * hp + j
                col = lanes(cs, hh)
                row = jnp.broadcast_to(cs_t[hh:hh + 1, :], (L, L))
                dec = jnp.exp(jnp.where(mask, col - row, -jnp.inf))
                gmats.append((cb * dec).astype(BF16))
                e_cols.append(jnp.exp(col))
                d_outs.append(jnp.exp(lanes(a_end, hh) - col))
            din = jnp.where(lane_lo, e_cols[0], e_cols[1])
            dout = jnp.where(lane_lo, d_outs[0], d_outs[1])
            xdt = xbc[:, hp * 128:(hp + 1) * 128] * dtx[:, hp * 128:(hp + 1) * 128]
            xdd_blocks.append((xdt * dout).astype(BF16))
            sdec_blocks.append(din[end:end + 1, :])
            xdt = xdt.astype(BF16)
            ya = jnp.dot(gmats[0], xdt, preferred_element_type=F32)
            yb = jnp.dot(gmats[1], xdt, preferred_element_type=F32)
            ys.append(jnp.where(lane_lo, ya, yb) + din * y_off[:, p * 128:(p + 1) * 128])
        upd = lax.dot_general(bm, jnp.concatenate(xdd_blocks, axis=1), (((0,), (0,)), ((), ())),
                              preferred_element_type=F32)
        state_ref[d, g] = state * jnp.concatenate(sdec_blocks, axis=1) + upd
    return jnp.concatenate(ys, axis=1)


def _ssd_kernel(xf_ref, xb_ref, dtf_ref, dtb_ref, alog_ref, bias_ref, emat_ref, yf_ref, yb_ref, state_ref):
    @pl.when(pl.program_id(1) == 0)
    def _():
        state_ref[...] = jnp.zeros_like(state_ref)

    a = -jnp.exp(alog_ref[...])
    bias = bias_ref[...]
    emat = emat_ref[...]
    yf_ref[0] = _ssd_chunk(xf_ref[0], dtf_ref[0], a[0:1, :], bias[0:1, :], emat, state_ref, 0, False)
    yb_ref[0] = _ssd_chunk(xb_ref[0], dtb_ref[0], a[1:2, :], bias[1:2, :], emat, state_ref, 1, True)


def _ssd(xbc, dt, a_log, dt_bias, ctx_len):
    b, t, _ = xbc.shape
    nc = t // CHUNK
    ncc = ctx_len // CHUNK

    def fwd(bb, j):
        return (bb, j, 0)

    def bwd(bb, j):
        return (bb, jnp.where(j < ncc, ncc - 1 - j, nc - 1 + ncc - j), 0)

    pad = lambda v: jnp.pad(v, ((0, 0), (0, 128 - SSM_HEADS)))
    emat = (jnp.arange(LANES)[:, None] == jnp.arange(SSM_D)[None, :] // SSM_HEAD_DIM).astype(BF16)
    return pl.pallas_call(
        _ssd_kernel,
        grid=(b, nc),
        in_specs=[pl.BlockSpec((1, CHUNK, SSM_XBC), fwd), pl.BlockSpec((1, CHUNK, SSM_XBC), bwd),
                  pl.BlockSpec((1, CHUNK, 128), fwd), pl.BlockSpec((1, CHUNK, 128), bwd),
                  _const_spec((2, 128)), _const_spec((2, 128)), _const_spec((LANES, SSM_D))],
        out_specs=[pl.BlockSpec((1, CHUNK, SSM_D), fwd), pl.BlockSpec((1, CHUNK, SSM_D), bwd)],
        out_shape=[jax.ShapeDtypeStruct((b, t, SSM_D), F32), jax.ShapeDtypeStruct((b, t, SSM_D), F32)],
        scratch_shapes=[pltpu.VMEM((2, 2, SSM_STATE, 512), F32)],
        compiler_params=_cparams(("arbitrary", "arbitrary")),
        name="ssd_scan",
    )(xbc, xbc, dt, dt, pad(a_log), pad(dt_bias), emat)


def _post_ssm_kernel(x_ref, yf_ref, yb_ref, xs_ref, z_ref, v_ref, dsk_ref, sg_ref, lng_ref, lnb_ref, wout_ref,
                     mods_ref, g2_ref, wrh_ref, wrl_ref, br_ref, rt_ref, x1_ref, hp_ref, slab_ref, rtab_ref, cnt_ref,
                     carry_ref, wbf_ref, *, nct):
    @pl.when((pl.program_id(0) == 0) & (pl.program_id(1) == 0))
    def _():
        for r0 in range(0, wbf_ref.shape[0], 256):
            wbf_ref[r0:r0 + 256, :] = wout_ref[0, r0:r0 + 256, :].astype(BF16)

    y = yf_ref[0] + yb_ref[0] + dsk_ref[...] * xs_ref[0]
    z = z_ref[0]
    y_ssm = _rms(y * (z * _sigmoid(z)), sg_ref[...])
    v = v_ref[0]
    mu = jnp.mean(v, axis=-1, keepdims=True)
    vc = v - mu
    ln = vc * lax.rsqrt(jnp.mean(vc * vc, axis=-1, keepdims=True) + EPS) * lng_ref[...] + lnb_ref[...]
    y_conv = ln * _sigmoid(ln)
    o = (jnp.dot(y_ssm.astype(BF16), wbf_ref[0:1024, :], preferred_element_type=F32)
         + jnp.dot(y_conv.astype(BF16), wbf_ref[1024:2048, :], preferred_element_type=F32))
    _residual_norm_router(x_ref[0], o, mods_ref, g2_ref, wrh_ref, wrl_ref, br_ref, rt_ref, x1_ref, hp_ref, slab_ref,
                          rtab_ref, cnt_ref, carry_ref, nct=nct, route_ctx=False)


def _post_ssm(x_all, yf, yb, xbc_act, z, v, d_skip, ssm_norm_g, ln_g, ln_b, w_out, mods, g2, w_router, b_router, nct):
    b, t, _ = x_all.shape
    tok = lambda bb, i: (bb, i, 0)
    dsk = jnp.repeat(d_skip[0] + d_skip[1], SSM_HEAD_DIM).reshape(1, SSM_D)
    out_specs, out_shape = _post_out_specs(b, t)
    row = lambda: pl.BlockSpec((1, TM, 1024), tok)
    return pl.pallas_call(
        functools.partial(_post_ssm_kernel, nct=nct),
        grid=(b, t // TM),
        in_specs=[row(), row(), row(), row(), row(), row(),
                  _const_spec((1, SSM_D)), _const_spec((1, SSM_D)), _const_spec((1, D)), _const_spec((1, D)),
                  pl.BlockSpec((1,) + w_out.shape[1:], lambda bb, i: (0, 0, 0), pipeline_mode=pl.Buffered(1)),
                  pl.BlockSpec((1, 1, 6, D), lambda bb, i: (bb, jnp.where(i < nct, 1, 0), 0, 0)),
                  _const_spec((1, D))] + _router_specs(),
        out_specs=out_specs, out_shape=out_shape,
        scratch_shapes=[pltpu.VMEM((N_EXPERTS, LANES), F32), pltpu.VMEM(w_out.shape[1:], BF16)],
        compiler_params=_cparams(("arbitrary", "arbitrary")),
        name="post_ssm",
    )(x_all, yf, yb, xbc_act, z, v, dsk, ssm_norm_g.reshape(1, -1), ln_g.reshape(1, -1), ln_b.reshape(1, -1),
      w_out, mods, g2.reshape(1, D), *_router_operands(w_router, b_router))


def _final_kernel(x_ref, g_ref, slab_ref, mods_ref, fg_ref, o_ref):
    x = x_ref[0] + mods_ref[0, 0, 5:6, :] * _combine(g_ref, slab_ref[0])
    o_ref[0] = _rms(x, fg_ref[...])


def _final(x_all, g_lat, slab, mods, final_g, nct):
    b, s = g_lat.shape[2], g_lat.shape[3]
    return pl.pallas_call(
        _final_kernel,
        grid=(b, s // TM),
        in_specs=[pl.BlockSpec((1, TM, D), lambda bb, i: (bb, i + nct, 0)),
                  pl.BlockSpec((TOP_K, 2, 1, TM, SC_ROW_WORDS), lambda bb, i: (0, 0, bb, i, 0)),
                  pl.BlockSpec((1, TM, ROUTE_LANES), lambda bb, i: (bb, i + nct, 0)),
                  pl.BlockSpec((1, 1, 6, D), lambda bb, i: (bb, 0, 0, 0)),
                  _const_spec((1, D))],
        out_specs=pl.BlockSpec((1, TM, D), lambda bb, i: (bb, i, 0)),
        out_shape=jax.ShapeDtypeStruct((b, s, D), F32),
        compiler_params=_cparams(("arbitrary", "arbitrary")),
        name="final_norm",
    )(x_all, g_lat, slab, mods, final_g.reshape(1, D))


def kernel(x, c, ctx, c_ctx, w_mod, b_mod, norm_g, attn_w_in, mla_g_cq, mla_w_uq, mla_g_ckv, mla_w_ukv, gqa_g_q, gqa_g_k, attn_w_out, ssm_w_in, ssm_conv_w, ssm_conv_b, ssm_a_log, ssm_dt_bias, ssm_d, ssm_norm_g, conf_dw_w, conf_dw_b, conf_ln_g, conf_ln_b, ssm_w_out, moe_w_router, moe_b_router, moe_w_gate_up, moe_b_gate_up, moe_w_down, moe_b_down, final_g):
    b, s, _ = x.shape
    ctx_len = ctx.shape[1]
    t = ctx_len + s
    assert ctx_len % TM == 0 and s % TM == 0 and s % GRID_W == 0
    nct = ctx_len // TM

    c_rows = jnp.concatenate([c, c_ctx[None, :], jnp.zeros((-(b + 1) % 8, D), F32)], axis=0)
    mod_all = _modulations(c_rows, w_mod, b_mod)
    mods = []
    for i in range(w_mod.shape[0]):
        lat = mod_all[i, :b].reshape(b, 1, 6, D)
        cm = jnp.broadcast_to(mod_all[i, b].reshape(1, 1, 6, D), (b, 1, 6, D))
        mods.append(jnp.concatenate([lat, cm], axis=1))

    tab = _rope_tables(s, ctx_len)
    qm, km, vm, qg, kg, vgs = _attn_in(ctx, x, mods[0], norm_g[0, 0], attn_w_in[0], mla_g_cq[0], mla_w_uq[0], mla_g_ckv[0],
                                       mla_w_ukv[0], gqa_g_q[0], gqa_g_k[0], tab, nct)
    o_m = _mla_attention(qm, km, vm, nct, ctx_len)
    o_g = _gqa_attention(qg, kg, vgs, nct, ctx_len)
    x_all, hp, slab0, rtab, counts = _post_attn(ctx, x, o_m, o_g, attn_w_out[0], mods[0], norm_g[0, 1], moe_w_router[0],
                                                moe_b_router[0], nct)
    g_all = _moe(hp, rtab, counts, 0, moe_w_gate_up, moe_b_gate_up, moe_w_down, moe_b_down, t, 0)

    x_all, z, xbc, dt, v = _ssm_in(x_all, g_all, slab0, mods[0], mods[1], norm_g[1, 0], ssm_w_in, nct)
    xbc_act = _depthwise_conv(xbc, ssm_conv_w[0], ssm_conv_b[0], ctx_len, True)
    v_conv = _depthwise_conv(v, conf_dw_w[0], conf_dw_b[0], ctx_len, False)
    yf, yb = _ssd(xbc_act, dt, ssm_a_log[0], ssm_dt_bias[0], ctx_len)
    x_all, hp, slab1, rtab, counts = _post_ssm(x_all, yf, yb, xbc_act, z, v_conv, ssm_d[0], ssm_norm_g[0], conf_ln_g[0],
                                               conf_ln_b[0], ssm_w_out, mods[1], norm_g[1, 1], moe_w_router[1],
                                               moe_b_router[1], nct)
    g_lat = _moe(hp, rtab, counts, 1, moe_w_gate_up, moe_b_gate_up, moe_w_down, moe_b_down, s, ctx_len)
    return _final(x_all, g_lat, slab1, mods[1], final_g, nct)
```

```python
import functools

import jax
import jax.numpy as jnp
from jax import lax
from jax.experimental import pallas as pl
from jax.experimental.pallas import tpu as pltpu
from jax.experimental.pallas import tpu_sc as plsc

F32 = jnp.float32
BF16 = jnp.bfloat16
HIGHEST = lax.Precision.HIGHEST

D = 1024
EPS = 1e-6
GRID_W = 64
ROPE_THETA = 10000.0
LOG2E = 1.4426950408889634

MLA_HEADS = 8
MLA_NOPE = 64
MLA_ROPE = 32
MLA_Q_LORA = 256
MLA_KV_LORA = 128
MLA_SCALE = (MLA_NOPE + MLA_ROPE) ** -0.5
GQA_HEADS = 8
GQA_HEAD_DIM = 64
GQA_SCALE = GQA_HEAD_DIM ** -0.5
ATTN_IN_AUG = 1280

SSM_HEADS = 16
SSM_HEAD_DIM = 64
SSM_STATE = 128
SSM_D = 1024
SSM_XBC = 1536
SSM_CONV = 5
CONF_K = 31
CHUNK = 128
SSM_IN_AUG = 1024 + 1536 + 128 + 2048

N_EXPERTS = 32
TOP_K = 4
MOE_FF = 1024
SWIGLU_LIMIT = 7.0
SWIGLU_ALPHA = 1.702

TM = 256
TMOE = 512
ROUTE_LANES = 128
ROUTE_ROWS = 16
SC_WINDOW = 128
SC_ROW_WORDS = 256
LANES = 128
HALO = 16
VMEM_LIMIT = 56 * 1024 * 1024


def _cparams(sem):
    return pltpu.CompilerParams(dimension_semantics=sem, vmem_limit_bytes=VMEM_LIMIT)


def _rms(x, g):
    return x * lax.rsqrt(jnp.mean(x * x, axis=-1, keepdims=True) + EPS) * g


def _sigmoid(x):
    return 1.0 / (1.0 + jnp.exp(-x))


def _const_spec(shape):
    n = len(shape)
    return pl.BlockSpec(shape, lambda *_: (0,) * n)


def _mod_kernel(c_ref, w_ref, b_ref, o_ref):
    c = c_ref[...]
    o_ref[0] = jnp.dot(c * _sigmoid(c), w_ref[0], precision=HIGHEST, preferred_element_type=F32) + b_ref[0]


def _modulations(c_rows, w_mod, b_mod):
    depth, _, n = w_mod.shape
    tn = 512
    r = c_rows.shape[0]
    return pl.pallas_call(
        _mod_kernel,
        grid=(depth, n // tn),
        in_specs=[pl.BlockSpec((r, D), lambda l, j: (0, 0)),
                  pl.BlockSpec((1, D, tn), lambda l, j: (l, 0, j)),
                  pl.BlockSpec((1, 1, tn), lambda l, j: (l, 0, j))],
        out_specs=pl.BlockSpec((1, r, tn), lambda l, j: (l, 0, j)),
        out_shape=jax.ShapeDtypeStruct((depth, r, n), F32),
        compiler_params=_cparams(("arbitrary", "arbitrary")),
        name="modulations",
    )(c_rows, w_mod, b_mod.reshape(depth, 1, n))


def _stream_tile(ctx_ref, x_ref, nct):
    return jnp.where(pl.program_id(1) < nct, ctx_ref[0], x_ref[0])


def _stream_specs(nct):
    return [pl.BlockSpec((1, TM, D), lambda bb, i: (bb, jnp.minimum(i, nct - 1), 0)),
            pl.BlockSpec((1, TM, D), lambda bb, i: (bb, jnp.maximum(i - nct, 0), 0))]


def _attn_in_kernel(ctx_ref, x_ref, mods_ref, ng_ref, win_ref, gcq_ref, wuq_ref, gckv_ref, wkv_ref, gq_ref, gk_ref, bd_ref,
                    tab_ref, qm_ref, km_ref, vm_ref, qg_ref, kg_ref, vgs_ref, *, nct):
    x = _stream_tile(ctx_ref, x_ref, nct)
    sh = mods_ref[0, 0, 0:1, :]
    sc = mods_ref[0, 0, 1:2, :]
    h = _rms(x, ng_ref[...]) * (1.0 + sc) + sh
    u = jnp.dot(h.astype(BF16), win_ref[...], preferred_element_type=F32)

    tab = tab_ref[...]
    cg, sg = tab[:, 0:128], tab[:, 128:256]
    cm, sm = tab[:, 256:384], tab[:, 384:512]
    lane = lax.broadcasted_iota(jnp.int32, (x.shape[0], LANES), 1)
    even = (lane & 1) == 0
    lo = lane < 64

    def rope(v, c, s):
        partner = jnp.where(even, pltpu.roll(v, LANES - 1, 1), pltpu.roll(v, 1, 1))
        return v * c + partner * s

    cq = _rms(u[:, 0:256], gcq_ref[...])
    qm = jnp.dot(cq.astype(BF16), wuq_ref[...], preferred_element_type=F32)
    for hd in range(MLA_HEADS):
        qm_ref[0, hd] = rope(qm[:, hd * 128:(hd + 1) * 128], cm, sm).astype(BF16)
    ckv = _rms(u[:, 256:384], gckv_ref[...])
    kv = jnp.dot(ckv.astype(BF16), wkv_ref[...], preferred_element_type=F32)
    kr = rope(u[:, 384:512], cm, sm)
    for hd in range(MLA_HEADS):
        km_ref[0, hd] = (kv[:, hd * 128:(hd + 1) * 128] + kr).astype(BF16)
    one = jnp.ones((x.shape[0], LANES), F32)
    for p in range(MLA_HEADS // 2):
        blk = kv[:, 1024 + p * 128:1024 + (p + 1) * 128]
        vm_ref[0, 2 * p] = jnp.where(lo, blk, one).astype(BF16)
        vm_ref[0, 2 * p + 1] = jnp.where(lo, one, blk).astype(BF16)

    qg = u[:, 512:1024]
    ms = jnp.dot((qg * qg).astype(BF16), bd_ref[...], preferred_element_type=F32)
    qg = qg * lax.rsqrt(ms + EPS) * gq_ref[...]
    zero = jnp.zeros((x.shape[0], LANES), F32)
    for p in range(GQA_HEADS // 2):
        blk = rope(qg[:, p * 128:(p + 1) * 128], cg, sg)
        swp = pltpu.roll(blk, 64, 1)
        if p < 2:
            qg_ref[0, 2 * p] = jnp.where(lo, blk, zero).astype(BF16)
            qg_ref[0, 2 * p + 1] = jnp.where(lo, swp, zero).astype(BF16)
        else:
            qg_ref[0, 2 * p] = jnp.where(lo, zero, swp).astype(BF16)
            qg_ref[0, 2 * p + 1] = jnp.where(lo, zero, blk).astype(BF16)
    kg = u[:, 1024:1152]
    msk = jnp.dot((kg * kg).astype(BF16), bd_ref[0:128, 0:128], preferred_element_type=F32)
    kg_ref[0] = rope(kg * lax.rsqrt(msk + EPS) * gk_ref[...], cg, sg).astype(BF16)
    vg = u[:, 1152:1280]
    vsw = pltpu.roll(vg, 64, 1)
    vgs_ref[0, 0] = jnp.where(lo, vg, one).astype(BF16)
    vgs_ref[0, 1] = jnp.where(lo, one, vsw).astype(BF16)
    vgs_ref[0, 2] = jnp.where(lo, vsw, one).astype(BF16)
    vgs_ref[0, 3] = jnp.where(lo, one, vg).astype(BF16)


def _rope_tables(seq, ctx_len):
    rows = seq // GRID_W
    row = jnp.broadcast_to(jnp.arange(rows, dtype=F32)[:, None], (rows, GRID_W)).reshape(-1)
    col = jnp.broadcast_to(jnp.arange(GRID_W, dtype=F32)[None, :], (rows, GRID_W)).reshape(-1)

    def interleaved(rot_dim):
        n_freq = rot_dim // 4
        inv_freq = ROPE_THETA ** (-jnp.arange(n_freq, dtype=F32) / n_freq)
        ang = jnp.concatenate([row[:, None] * inv_freq, col[:, None] * inv_freq], axis=-1)
        cos = jnp.repeat(jnp.cos(ang), 2, axis=-1)
        sin = jnp.repeat(jnp.sin(ang), 2, axis=-1) * jnp.tile(jnp.array([-1.0, 1.0], F32), rot_dim // 2)
        return cos, sin

    cg, sg = interleaved(GQA_HEAD_DIM)
    cg, sg = jnp.tile(cg, (1, 2)), jnp.tile(sg, (1, 2))
    cm32, sm32 = interleaved(MLA_ROPE)
    ones, zeros = jnp.ones((seq, 64), F32), jnp.zeros((seq, 64), F32)
    cm = jnp.concatenate([ones, cm32, ones[:, :32]], axis=-1)
    sm = jnp.concatenate([zeros, sm32, zeros[:, :32]], axis=-1)
    lat = jnp.concatenate([cg, sg, cm, sm], axis=-1)
    ident = jnp.concatenate([jnp.ones((ctx_len, 128), F32), jnp.zeros((ctx_len, 128), F32)] * 2, axis=-1)
    return jnp.concatenate([ident, lat], axis=0)


def _attn_in(ctx, x, mods, ng, w_in, g_cq, w_uq, g_ckv, w_ukv, g_q, g_k, tab, nct):
    b = x.shape[0]
    t = ctx.shape[1] + x.shape[1]
    o1, o2, o3 = MLA_Q_LORA, MLA_Q_LORA + MLA_KV_LORA, MLA_Q_LORA + MLA_KV_LORA + MLA_ROPE
    zc = lambda n: jnp.zeros((D, n), F32)
    w_aug = jnp.concatenate([w_in[:, :o2], zc(64), w_in[:, o2:o3], zc(32), w_in[:, o3:]], axis=1).astype(BF16)
    wuq = jnp.pad(w_uq.reshape(MLA_Q_LORA, MLA_HEADS, 96), ((0, 0), (0, 0), (0, 32))).reshape(MLA_Q_LORA, 1024).astype(BF16)
    wukv = w_ukv.reshape(MLA_KV_LORA, MLA_HEADS, 128)
    wk = jnp.pad(wukv[:, :, :64], ((0, 0), (0, 0), (0, 64))).reshape(MLA_KV_LORA, 1024)
    wkv = jnp.concatenate([wk, wukv[:, :, 64:].reshape(MLA_KV_LORA, 512)], axis=1).astype(BF16)
    gcq = (g_cq * (MLA_SCALE * LOG2E)).reshape(1, -1)
    gq = (jnp.tile(g_q, GQA_HEADS) * (GQA_SCALE * LOG2E)).reshape(1, -1)
    gk = jnp.tile(g_k, 2).reshape(1, -1)
    bd = jnp.kron(jnp.eye(GQA_HEADS, dtype=F32), jnp.full((64, 64), 1.0 / 64, F32)).astype(BF16)
    nt = t // TM
    tok = lambda bb, i: (bb, i, 0)
    hm = lambda bb, i: (bb, 0, i, 0)
    return pl.pallas_call(
        functools.partial(_attn_in_kernel, nct=nct),
        grid=(b, nt),
        in_specs=_stream_specs(nct) + [
                  pl.BlockSpec((1, 1, 6, D), lambda bb, i: (bb, jnp.where(i < nct, 1, 0), 0, 0)),
                  _const_spec((1, D)), _const_spec((D, ATTN_IN_AUG)), _const_spec((1, MLA_Q_LORA)),
                  _const_spec((MLA_Q_LORA, 1024)), _const_spec((1, MLA_KV_LORA)), _const_spec((MLA_KV_LORA, 1536)),
                  _const_spec((1, 512)), _const_spec((1, 128)), _const_spec((512, 512)),
                  pl.BlockSpec((TM, 512), lambda bb, i: (i, 0))],
        out_specs=[pl.BlockSpec((1, 8, TM, 128), hm), pl.BlockSpec((1, 8, TM, 128), hm),
                   pl.BlockSpec((1, 8, TM, 128), hm), pl.BlockSpec((1, 8, TM, 128), hm),
                   pl.BlockSpec((1, TM, 128), tok), pl.BlockSpec((1, 4, TM, 128), hm)],
        out_shape=[jax.ShapeDtypeStruct((b, 8, t, 128), BF16), jax.ShapeDtypeStruct((b, 8, t, 128), BF16),
                   jax.ShapeDtypeStruct((b, 8, t, 128), BF16), jax.ShapeDtypeStruct((b, 8, t, 128), BF16),
                   jax.ShapeDtypeStruct((b, t, 128), BF16), jax.ShapeDtypeStruct((b, 4, t, 128), BF16)],
        compiler_params=_cparams(("arbitrary", "arbitrary")),
        name="attn_in",
    )(ctx, x, mods, ng.reshape(1, D), w_aug, gcq, wuq, g_ckv.reshape(1, -1), wkv, gq, gk, bd, tab)


def _softmax_pv(q, k, v, sum_lane):
    s = lax.dot_general(q, k, (((1,), (1,)), ((), ())), preferred_element_type=F32)
    m = jnp.max(s, axis=-1, keepdims=True)
    o = jnp.dot(jnp.exp2(s - m).astype(BF16), v, preferred_element_type=F32)
    return o / o[:, sum_lane:sum_lane + 1]


def _mla_kernel(q_ref, k_ref, v_ref, o_ref, *, nct, ctx_len):
    i = pl.program_id(1)
    lo = lax.broadcasted_iota(jnp.int32, (q_ref.shape[2], LANES), 1) < 64

    def run(nk):
        for pr in range(MLA_HEADS // 2):
            oa = _softmax_pv(q_ref[0, 2 * pr], k_ref[0, 2 * pr, 0:nk, :], v_ref[0, 2 * pr, 0:nk, :], 64)
            ob = _softmax_pv(q_ref[0, 2 * pr + 1], k_ref[0, 2 * pr + 1, 0:nk, :], v_ref[0, 2 * pr + 1, 0:nk, :], 0)
            o_ref[0, :, pr * 128:(pr + 1) * 128] = jnp.where(lo, oa, ob).astype(BF16)

    @pl.when(i < nct)
    def _():
        run(ctx_len)

    @pl.when(i >= nct)
    def _():
        run(k_ref.shape[2])


def _mla_attention(qm, km, vm, nct, ctx_len):
    b, _, t, _ = qm.shape
    nt = t // TM
    return pl.pallas_call(
        functools.partial(_mla_kernel, nct=nct, ctx_len=ctx_len),
        grid=(b, nt),
        in_specs=[pl.BlockSpec((1, 8, TM, 128), lambda bb, i: (bb, 0, i, 0)),
                  pl.BlockSpec((1, 8, t, 128), lambda bb, i: (bb, 0, 0, 0)),
                  pl.BlockSpec((1, 8, t, 128), lambda bb, i: (bb, 0, 0, 0))],
        out_specs=pl.BlockSpec((1, TM, 512), lambda bb, i: (bb, i, 0)),
        out_shape=jax.ShapeDtypeStruct((b, t, 512), BF16),
        compiler_params=_cparams(("arbitrary", "arbitrary")),
        name="mla_attention",
    )(qm, km, vm)


def _gqa_kernel(q_ref, k_ref, v_ref, o_ref, *, nct, ctx_len):
    i = pl.program_id(1)
    lo = lax.broadcasted_iota(jnp.int32, (q_ref.shape[2], LANES), 1) < 64

    def run(nk):
        k = k_ref[0, 0:nk, :]
        tq = q_ref.shape[2]
        for g in range(2):
            q4 = jnp.concatenate([q_ref[0, 4 * g + j] for j in (0, 2, 1, 3)], axis=0)
            s4 = lax.dot_general(q4, k, (((1,), (1,)), ((), ())), preferred_element_type=F32)
            res = []
            for par, sum_lane in ((0, 64), (1, 0)):
                s = s4[par * 2 * tq:(par + 1) * 2 * tq]
                m = jnp.max(s, axis=-1, keepdims=True)
                o = jnp.dot(jnp.exp2(s - m).astype(BF16), v_ref[0, 2 * g + par, 0:nk, :], preferred_element_type=F32)
                res.append(o / o[:, sum_lane:sum_lane + 1])
            for pr in range(2):
                o_ref[0, :, (2 * g + pr) * 128:(2 * g + pr + 1) * 128] = jnp.where(
                    lo, res[0][pr * tq:(pr + 1) * tq], res[1][pr * tq:(pr + 1) * tq]).astype(BF16)

    @pl.when(i < nct)
    def _():
        run(ctx_len)

    @pl.when(i >= nct)
    def _():
        run(k_ref.shape[1])


def _gqa_attention(qg, kg, vgs, nct, ctx_len):
    b, _, t, _ = qg.shape
    nt = t // TM
    return pl.pallas_call(
        functools.partial(_gqa_kernel, nct=nct, ctx_len=ctx_len),
        grid=(b, nt),
        in_specs=[pl.BlockSpec((1, 8, TM, 128), lambda bb, i: (bb, 0, i, 0)),
                  pl.BlockSpec((1, t, 128), lambda bb, i: (bb, 0, 0)),
                  pl.BlockSpec((1, 4, t, 128), lambda bb, i: (bb, 0, 0, 0))],
        out_specs=pl.BlockSpec((1, TM, 512), lambda bb, i: (bb, i, 0)),
        out_shape=jax.ShapeDtypeStruct((b, t, 512), BF16),
        compiler_params=_cparams(("arbitrary", "arbitrary")),
        name="gqa_attention",
    )(qg, kg, vgs)


def _pack_bf16_pairs(h):
    half = h.shape[1] // 2
    lo = pltpu.bitcast(h[:, :half].astype(BF16).astype(F32), jnp.uint32) >> 16
    hi = pltpu.bitcast(h[:, half:].astype(BF16).astype(F32), jnp.uint32) & jnp.uint32(0xFFFF0000)
    return lo | hi


def _unpack_bf16_pairs(w):
    return pltpu.bitcast(w << 16, F32), pltpu.bitcast(w & jnp.uint32(0xFFFF0000), F32)


def _store_planes(ref, words):
    ref[0, 0] = words[:, 0:SC_ROW_WORDS]
    ref[1, 0] = words[:, SC_ROW_WORDS:2 * SC_ROW_WORDS]


def _route_tile(lt, rt_ref, carry_ref, counted):
    n_e, rows = lt.shape
    eidx = lax.broadcasted_iota(jnp.int32, (n_e, rows), 0).astype(F32)
    work = lt
    vals, ids, hots = [], [], []
    for _ in range(TOP_K):
        m = jnp.max(work, axis=0, keepdims=True)
        idx = jnp.min(jnp.where(work == m, eidx, float(N_EXPERTS)), axis=0, keepdims=True)
        hot = eidx == idx
        vals.append(m)
        ids.append(idx)
        hots.append(hot)
        work = jnp.where(hot, -jnp.inf, work)
    exps = [jnp.exp(v - vals[0]) for v in vals]
    den = exps[0] + exps[1] + exps[2] + exps[3]
    mask = jnp.where(hots[0] | hots[1] | hots[2] | hots[3], 1.0, 0.0).astype(BF16)
    cum = jnp.dot(mask, rt_ref[...], preferred_element_type=F32)
    carry = carry_ref[...]
    before = cum[:, 0:rows] + jnp.concatenate([carry] * (rows // LANES), axis=1)
    ranks = [jnp.sum(jnp.where(h, before, 0.0), axis=0, keepdims=True) for h in hots]
    carry_ref[...] = carry + counted * cum[:, rows:rows + LANES]
    return jnp.concatenate(ids + ranks + [e / den for e in exps] + [jnp.zeros((ROUTE_LANES - 3 * TOP_K, rows), F32)],
                           axis=0)


def _residual_norm_router(x, o, mods_ref, g2_ref, wrh_ref, wrl_ref, br_ref, rt_ref, x1_ref, hp_ref, slab_ref,
                          rtab_ref, cnt_ref, carry_ref, *, nct, route_ctx):
    first = (pl.program_id(0) == 0) & (pl.program_id(1) == 0)

    @pl.when(first)
    def _():
        carry_ref[...] = jnp.zeros_like(carry_ref)

    g1 = mods_ref[0, 0, 2:3, :]
    sh2 = mods_ref[0, 0, 3:4, :]
    sc2 = mods_ref[0, 0, 4:5, :]
    x1 = x + g1 * o
    h2 = _rms(x1, g2_ref[...]) * (1.0 + sc2) + sh2
    x1_ref[0] = x1
    _store_planes(hp_ref, _pack_bf16_pairs(h2))
    hh = h2.astype(BF16)
    hl = (h2 - hh.astype(F32)).astype(BF16)
    logits = (jnp.dot(hh, wrh_ref[...], preferred_element_type=F32) + jnp.dot(hl, wrh_ref[...], preferred_element_type=F32)
              + jnp.dot(hh, wrl_ref[...], preferred_element_type=F32) + br_ref[...])
    counted = 1.0 if route_ctx else jnp.where(pl.program_id(1) >= nct, 1.0, 0.0)
    slab_t = _route_tile(logits.T[0:N_EXPERTS, :], rt_ref, carry_ref, counted)
    slab_ref[0] = slab_t.T
    rtab_ref[0] = slab_t[0:ROUTE_ROWS, :]
    cnt_ref[...] = carry_ref[...]


def _post_attn_kernel(ctx_ref, x_ref, om_ref, og_ref, wout_ref, mods_ref, g2_ref, wrh_ref, wrl_ref, br_ref, rt_ref,
                      x1_ref, hp_ref, slab_ref, rtab_ref, cnt_ref, carry_ref, *, nct):
    o = (jnp.dot(om_ref[0], wout_ref[0:512, :], preferred_element_type=F32)
         + jnp.dot(og_ref[0], wout_ref[512:1024, :], preferred_element_type=F32))
    _residual_norm_router(_stream_tile(ctx_ref, x_ref, nct), o, mods_ref, g2_ref, wrh_ref, wrl_ref, br_ref, rt_ref,
                          x1_ref, hp_ref, slab_ref, rtab_ref, cnt_ref, carry_ref, nct=nct, route_ctx=True)


def _post_out_specs(b, t):
    tok = lambda bb, i: (bb, i, 0)
    specs = [pl.BlockSpec((1, TM, D), tok), pl.BlockSpec((2, 1, TM, SC_ROW_WORDS), lambda bb, i: (0, bb, i, 0)),
             pl.BlockSpec((1, TM, ROUTE_LANES), tok), pl.BlockSpec((1, ROUTE_ROWS, TM), lambda bb, i: (bb, 0, i)),
             _const_spec((N_EXPERTS, LANES))]
    shapes = [jax.ShapeDtypeStruct((b, t, D), F32), jax.ShapeDtypeStruct((2, b, t, SC_ROW_WORDS), jnp.uint32),
              jax.ShapeDtypeStruct((b, t, ROUTE_LANES), F32), jax.ShapeDtypeStruct((b, ROUTE_ROWS, t), F32),
              jax.ShapeDtypeStruct((N_EXPERTS, LANES), F32)]
    return specs, shapes


def _router_operands(w_router, b_router):
    wp = jnp.pad(w_router, ((0, 0), (0, LANES - N_EXPERTS)))
    wrh = wp.astype(BF16)
    wrl = (wp - wrh.astype(F32)).astype(BF16)
    rt = jnp.concatenate([jnp.triu(jnp.ones((TM, TM), F32), 1), jnp.ones((TM, LANES), F32)], axis=1).astype(BF16)
    return wrh, wrl, jnp.pad(b_router, (0, LANES - N_EXPERTS)).reshape(1, LANES), rt


def _router_specs():
    return [_const_spec((D, LANES)), _const_spec((D, LANES)), _const_spec((1, LANES)), _const_spec((TM, TM + LANES))]


def _post_attn(ctx, x, o_m, o_g, w_out, mods, g2, w_router, b_router, nct):
    b, t, _ = o_m.shape
    tok = lambda bb, i: (bb, i, 0)
    out_specs, out_shape = _post_out_specs(b, t)
    return pl.pallas_call(
        functools.partial(_post_attn_kernel, nct=nct),
        grid=(b, t // TM),
        in_specs=_stream_specs(nct) + [
                  pl.BlockSpec((1, TM, 512), tok), pl.BlockSpec((1, TM, 512), tok),
                  _const_spec((1024, D)),
                  pl.BlockSpec((1, 1, 6, D), lambda bb, i: (bb, jnp.where(i < nct, 1, 0), 0, 0)),
                  _const_spec((1, D))] + _router_specs(),
        out_specs=out_specs, out_shape=out_shape,
        scratch_shapes=[pltpu.VMEM((N_EXPERTS, LANES), F32)],
        compiler_params=_cparams(("arbitrary", "arbitrary")),
        name="post_attn",
    )(ctx, x, o_m, o_g, w_out.astype(BF16), mods, g2.reshape(1, D), *_router_operands(w_router, b_router))


def _route_tables(rtab, counts, n_tiles):
    e = rtab[0:TOP_K].astype(jnp.int32)
    rank = rtab[TOP_K:2 * TOP_K].astype(jnp.int32)
    counts = counts[:, 0].astype(jnp.int32)
    padded = (counts + TMOE - 1) // TMOE * TMOE
    pend = jnp.cumsum(padded)
    pstart = pend - padded
    experts = jnp.arange(N_EXPERTS, dtype=jnp.int32)
    first_live = pend - counts
    pos = rank + jnp.sum(jnp.where(e[..., None] == experts, first_live, 0), axis=-1)
    tile_start = jnp.arange(n_tiles, dtype=jnp.int32) * TMOE
    tile_expert = jnp.minimum(jnp.sum(tile_start[:, None] >= pend[None, :], axis=-1), N_EXPERTS - 1).astype(jnp.int32)
    mine = tile_expert[:, None] == experts
    live = tile_start + TMOE - jnp.sum(jnp.where(mine, first_live, 0), axis=-1)
    tile_valid = jnp.where(tile_start < pend[-1], jnp.clip(live, 0, TMOE), 0).astype(jnp.int32)
    prev_expert = jnp.concatenate([jnp.full((1,), -1, jnp.int32), tile_expert[:-1]])
    tile_first = ((tile_valid > 0) & (tile_expert != prev_expert)).astype(jnp.int32)
    later = (experts[None, :] > experts[:, None]) & (counts[None, :] > 0)
    next_of = jnp.min(jnp.where(later, experts[None, :], N_EXPERTS), axis=-1)
    next_of = jnp.where(next_of < N_EXPERTS, next_of, -1)
    tile_next = jnp.sum(jnp.where(mine, next_of, 0), axis=-1).astype(jnp.int32)
    return pos, (tile_expert, tile_valid, tile_first, tile_next)


def _sc_mesh():
    return plsc.VectorSubcoreMesh(core_axis_name="core", subcore_axis_name="subcore")


def _sc_scatter_rows(src, idx, n_out, nb_tok, nb_src, nb_plane, off):
    m = idx.shape[0]
    steps_per_k = m // SC_WINDOW // TOP_K

    def src_block(i):
        q = i % steps_per_k
        plane = q // (steps_per_k // 2)
        r = q % (steps_per_k // 2)
        return (plane * nb_plane + (r // nb_tok) * nb_src + off + r % nb_tok, 0)

    @pl.kernel(out_type=jax.ShapeDtypeStruct((n_out, SC_ROW_WORDS), src.dtype), mesh=_sc_mesh(), scratch_types=[])
    def k(src_hbm, i_hbm, o_hbm):
        def body(x_vmem, i_vmem):
            pltpu.sync_copy(x_vmem, o_hbm.at[i_vmem.at[0]])

        pltpu.emit_pipeline(
            body,
            grid=(m // SC_WINDOW,),
            in_specs=[pl.BlockSpec((SC_WINDOW, SC_ROW_WORDS), src_block),
                      pl.BlockSpec((1, SC_WINDOW), lambda i: (0, i))],
            out_specs=[],
            core_axis_name=("core", "subcore"),
            dimension_semantics=(pltpu.PARALLEL,),
        )(src_hbm, i_hbm)

    return k(src, idx.reshape(1, m))


def _sc_gather_rows(src, idx):
    m = idx.shape[0]

    @pl.kernel(out_type=jax.ShapeDtypeStruct((m, SC_ROW_WORDS), src.dtype), mesh=_sc_mesh(), scratch_types=[])
    def k(src_hbm, i_hbm, o_hbm):
        def body(i_vmem, o_vmem):
            pltpu.sync_copy(src_hbm.at[i_vmem.at[0]], o_vmem)

        pltpu.emit_pipeline(
            body,
            grid=(m // SC_WINDOW,),
            in_specs=[pl.BlockSpec((1, SC_WINDOW), lambda i: (0, i))],
            out_specs=[pl.BlockSpec((SC_WINDOW, SC_ROW_WORDS), lambda i: (i, 0))],
            core_axis_name=("core", "subcore"),
            dimension_semantics=(pltpu.PARALLEL,),
        )(i_hbm, o_hbm)

    return k(src, idx.reshape(1, m))


def _moe_kernel(te_ref, tv_ref, tf_ref, tn_ref, xs_ref, wgu_hbm, bgu_ref, wd_hbm, bd_ref, ys_ref,
                wgu_bf, wd_bf, wgu_st, wd_st, sems, *, layer):
    t = pl.program_id(0)

    def fetch(e):
        return (pltpu.make_async_copy(wgu_hbm.at[layer, e], wgu_st, sems.at[0]),
                pltpu.make_async_copy(wd_hbm.at[layer, e], wd_st, sems.at[1]))

    @pl.when(t == 0)
    def _():
        for cp in fetch(te_ref[0]):
            cp.start()

    @pl.when(tf_ref[t] == 1)
    def _():
        for cp in fetch(te_ref[t]):
            cp.wait()
        wgu_bf[...] = wgu_st[...].astype(BF16)
        wd_bf[...] = wd_st[...].astype(BF16)

        @pl.when(tn_ref[t] >= 0)
        def _():
            for cp in fetch(tn_ref[t]):
                cp.start()

    valid = tv_ref[t]
    half = TMOE // 2

    def experts(rows):
        r0 = TMOE - rows
        lo, hi = _unpack_bf16_pairs(jnp.concatenate([xs_ref[0, r0:TMOE, :], xs_ref[1, r0:TMOE, :]], axis=1))
        live = lax.broadcasted_iota(jnp.int32, (rows, 1), 0) >= rows - valid
        x = jnp.where(live, jnp.concatenate([lo, hi], axis=1), 0.0).astype(BF16)
        gu = jnp.dot(x, wgu_bf[...], preferred_element_type=F32) + bgu_ref[0, 0]
        gate = jnp.minimum(gu[:, :MOE_FF], SWIGLU_LIMIT)
        up = jnp.clip(gu[:, MOE_FF:], -SWIGLU_LIMIT, SWIGLU_LIMIT)
        act = (up + 1.0) * (gate * _sigmoid(SWIGLU_ALPHA * gate))
        y = jnp.dot(act.astype(BF16), wd_bf[...], preferred_element_type=F32) + bd_ref[0, 0]
        words = _pack_bf16_pairs(y)
        ys_ref[0, r0:TMOE, :] = words[:, 0:SC_ROW_WORDS]
        ys_ref[1, r0:TMOE, :] = words[:, SC_ROW_WORDS:2 * SC_ROW_WORDS]

    @pl.when(valid > half)
    def _():
        experts(TMOE)

    @pl.when((valid > 0) & (valid <= half))
    def _():
        experts(half)
        ys_ref[:, 0:half, :] = jnp.zeros((2, half, SC_ROW_WORDS), jnp.uint32)

    @pl.when(valid == 0)
    def _():
        ys_ref[...] = jnp.zeros_like(ys_ref)


def _moe_experts(xs, tiles, layer, w_gu, b_gu, w_d, b_d):
    n_rows = xs.shape[1]
    n_tiles = n_rows // TMOE
    depth = w_gu.shape[0]
    grid_spec = pltpu.PrefetchScalarGridSpec(
        num_scalar_prefetch=4,
        grid=(n_tiles,),
        in_specs=[pl.BlockSpec((2, TMOE, SC_ROW_WORDS), lambda t, te, tv, tf, tn: (0, t, 0)),
                  pl.BlockSpec(memory_space=pl.ANY),
                  pl.BlockSpec((1, 1, 1, 2 * MOE_FF), lambda t, te, tv, tf, tn: (layer, te[t], 0, 0)),
                  pl.BlockSpec(memory_space=pl.ANY),
                  pl.BlockSpec((1, 1, 1, D), lambda t, te, tv, tf, tn: (layer, te[t], 0, 0))],
        out_specs=pl.BlockSpec((2, TMOE, SC_ROW_WORDS), lambda t, te, tv, tf, tn: (0, t, 0)),
        scratch_shapes=[pltpu.VMEM((D, 2 * MOE_FF), BF16), pltpu.VMEM((MOE_FF, D), BF16),
                        pltpu.VMEM((D, 2 * MOE_FF), F32), pltpu.VMEM((MOE_FF, D), F32),
                        pltpu.SemaphoreType.DMA((2,))],
    )
    return pl.pallas_call(
        functools.partial(_moe_kernel, layer=layer),
        grid_spec=grid_spec,
        out_shape=jax.ShapeDtypeStruct((2, n_rows, SC_ROW_WORDS), jnp.uint32),
        compiler_params=_cparams(("arbitrary",)),
        name="moe_experts",
    )(*tiles, xs, w_gu, b_gu.reshape(depth, N_EXPERTS, 1, -1), w_d, b_d.reshape(depth, N_EXPERTS, 1, -1))


def _moe(hp, rtab, counts, layer, w_gu, b_gu, w_d, b_d, n_seq, row0):
    _, b, t, _ = hp.shape
    n_tok = b * n_seq
    n_tiles = -(-n_tok * TOP_K // TMOE) + N_EXPERTS
    n_rows = n_tiles * TMOE
    rtab = jnp.transpose(rtab[:, :, row0:row0 + n_seq], (1, 0, 2)).reshape(ROUTE_ROWS, n_tok)
    pos, tiles = _route_tables(rtab, counts, n_tiles)
    idx = (pos[:, None, :] + (jnp.arange(2, dtype=jnp.int32) * n_rows)[None, :, None]).reshape(-1)
    xs = _sc_scatter_rows(hp.reshape(2 * b * t, SC_ROW_WORDS), idx, 2 * n_rows, n_seq // SC_WINDOW, t // SC_WINDOW,
                          b * t // SC_WINDOW, row0 // SC_WINDOW)
    ys = _moe_experts(xs.reshape(2, n_rows, SC_ROW_WORDS), tiles, layer, w_gu, b_gu, w_d, b_d)
    g = _sc_gather_rows(ys.reshape(2 * n_rows, SC_ROW_WORDS), idx)
    return g.reshape(TOP_K, 2, b, n_seq, SC_ROW_WORDS)


def _combine(g_ref, slab, rows=slice(None)):
    acc_lo = acc_hi = None
    for k in range(TOP_K):
        lo, hi = _unpack_bf16_pairs(jnp.concatenate([g_ref[k, 0, 0, rows, :], g_ref[k, 1, 0, rows, :]], axis=1))
        w = slab[:, 2 * TOP_K + k:2 * TOP_K + k + 1]
        acc_lo = w * lo if acc_lo is None else acc_lo + w * lo
        acc_hi = w * hi if acc_hi is None else acc_hi + w * hi
    return jnp.concatenate([acc_lo, acc_hi], axis=1)


def _ssm_in_kernel(x_ref, g_ref, slab_ref, modsp_ref, mods_ref, ng_ref, w_ref, x1_ref, z_ref, xbc_ref, dt_ref, v_ref,
                   wbf_ref):
    hi = SSM_D + SSM_XBC + SSM_HEADS

    @pl.when((pl.program_id(0) == 0) & (pl.program_id(1) == 0))
    def _():
        def put(col0, block):
            wbf_ref[:, col0:col0 + LANES] = block.T.astype(BF16)

        lo = hi - SSM_HEADS
        for c0 in range(0, lo, LANES):
            put(c0, w_ref[0, c0:c0 + LANES, :])
        put(lo, jnp.concatenate([w_ref[0, lo:hi, :], jnp.zeros((LANES - SSM_HEADS, D), F32)], axis=0))
        for c0 in range(0, w_ref.shape[1] - hi, LANES):
            put(lo + LANES + c0, w_ref[0, hi + c0:hi + c0 + LANES, :])

    sh = mods_ref[0, 0, 0:1, :]
    sc = mods_ref[0, 0, 1:2, :]
    gate = modsp_ref[0, 0, 5:6, :]
    half = x_ref.shape[1] // 2
    for p in range(2):
        r = slice(p * half, (p + 1) * half)
        x = x_ref[0, r, :] + gate * _combine(g_ref, slab_ref[0, r, :], r)
        x1_ref[0, r, :] = x
        h = _rms(x, ng_ref[...]) * (1.0 + sc) + sh
        u = jnp.dot(h.astype(BF16), wbf_ref[...], preferred_element_type=F32)
        z_ref[0, r, :] = u[:, 0:1024]
        xbc_ref[0, r, :] = u[:, 1024:2560]
        dt_ref[0, r, :] = u[:, 2560:2688]
        v_ref[0, r, :] = u[:, 2688:3712] * _sigmoid(u[:, 3712:4736])


def _ssm_in(x_all, g_all, slab, mods_prev, mods, ng, w_in, nct):
    b, t, _ = x_all.shape
    w_t = jnp.swapaxes(w_in, 1, 2)
    tok = lambda bb, i: (bb, i, 0)
    modspec = pl.BlockSpec((1, 1, 6, D), lambda bb, i: (bb, jnp.where(i < nct, 1, 0), 0, 0))
    return pl.pallas_call(
        _ssm_in_kernel,
        grid=(b, t // TM),
        in_specs=[pl.BlockSpec((1, TM, D), tok),
                  pl.BlockSpec((TOP_K, 2, 1, TM, SC_ROW_WORDS), lambda bb, i: (0, 0, bb, i, 0)),
                  pl.BlockSpec((1, TM, ROUTE_LANES), tok), modspec, modspec,
                  _const_spec((1, D)),
                  pl.BlockSpec((1,) + w_t.shape[1:], lambda bb, i: (0, 0, 0), pipeline_mode=pl.Buffered(1))],
        scratch_shapes=[pltpu.VMEM((D, SSM_IN_AUG), BF16)],
        out_specs=[pl.BlockSpec((1, TM, D), tok), pl.BlockSpec((1, TM, 1024), tok), pl.BlockSpec((1, TM, SSM_XBC), tok),
                   pl.BlockSpec((1, TM, 128), tok), pl.BlockSpec((1, TM, 1024), tok)],
        out_shape=[jax.ShapeDtypeStruct((b, t, D), F32), jax.ShapeDtypeStruct((b, t, 1024), F32),
                   jax.ShapeDtypeStruct((b, t, SSM_XBC), F32), jax.ShapeDtypeStruct((b, t, 128), F32),
                   jax.ShapeDtypeStruct((b, t, 1024), F32)],
        compiler_params=_cparams(("arbitrary", "arbitrary")),
        name="ssm_in",
    )(x_all, g_all, slab, mods_prev, mods, ng.reshape(1, D), w_t)


def _conv_kernel(x_ref, w_ref, b_ref, o_ref, pad_ref, *, taps, ctx_len, silu):
    t = x_ref.shape[1]
    ct = x_ref.shape[2]
    half = taps // 2
    zeros = jnp.zeros((HALO, ct), F32)
    pad_ref[0:HALO, :] = zeros
    pad_ref[HALO:HALO + ctx_len, :] = x_ref[0, 0:ctx_len, :]
    pad_ref[HALO + ctx_len:2 * HALO + ctx_len, :] = zeros
    pad_ref[2 * HALO + ctx_len:2 * HALO + t, :] = x_ref[0, ctx_len:t, :]
    pad_ref[2 * HALO + t:3 * HALO + t, :] = zeros
    w = w_ref[...]
    bias = b_ref[...]
    rows = CHUNK + 2 * HALO

    def segment(out_start, length, pad_start):
        def body(c, carry):
            base = pl.multiple_of(c * CHUNK, CHUNK)
            win = pad_ref[pl.ds(pad_start - HALO + base, rows), :]
            acc = jnp.broadcast_to(bias, (CHUNK, ct))
            for r in range(8):
                ks = [k for k in range(taps) if (HALO + k - half) % 8 == r]
                if not ks:
                    continue
                rolled = win if r == 0 else pltpu.roll(win, rows - r, 0)
                for k in ks:
                    off = HALO + k - half - r
                    acc = acc + w[k:k + 1, :] * rolled[off:off + CHUNK, :]
            if silu:
                acc = acc * _sigmoid(acc)
            o_ref[0, pl.ds(out_start + base, CHUNK), :] = acc
            return carry
        lax.fori_loop(0, length // CHUNK, body, 0)

    segment(0, ctx_len, HALO)
    segment(ctx_len, t - ctx_len, 2 * HALO + ctx_len)


def _depthwise_conv(x, w, bias, ctx_len, silu):
    b, t, c = x.shape
    taps = w.shape[0]
    ct = 256
    return pl.pallas_call(
        functools.partial(_conv_kernel, taps=taps, ctx_len=ctx_len, silu=silu),
        grid=(b, c // ct),
        in_specs=[pl.BlockSpec((1, t, ct), lambda bb, j: (bb, 0, j)),
                  pl.BlockSpec((taps, ct), lambda bb, j: (0, j)),
                  pl.BlockSpec((1, ct), lambda bb, j: (0, j))],
        out_specs=pl.BlockSpec((1, t, ct), lambda bb, j: (bb, 0, j)),
        out_shape=jax.ShapeDtypeStruct((b, t, c), F32),
        scratch_shapes=[pltpu.VMEM((t + 3 * HALO, ct), F32)],
        compiler_params=_cparams(("arbitrary", "arbitrary")),
        name=f"depthwise_conv{taps}",
    )(x, w, bias.reshape(1, c))


def _ssd_chunk(xbc, dt_raw, a_row, bias_row, emat, state_ref, d, reverse):
    L = CHUNK
    ri = lax.broadcasted_iota(jnp.int32, (L, L), 0)
    ci = lax.broadcasted_iota(jnp.int32, (L, L), 1)
    lane_lo = lax.broadcasted_iota(jnp.int32, (L, LANES), 1) < 64
    mask = (ci >= ri) if reverse else (ri >= ci)
    tri = mask.astype(F32)

    xdt_in = dt_raw + bias_row
    dt = jnp.maximum(xdt_in, 0.0) + jnp.log1p(jnp.exp(-jnp.abs(xdt_in)))
    da = dt * a_row
    cs = jnp.dot(tri, da, precision=HIGHEST, preferred_element_type=F32)
    cs_t = cs.T
    end = 0 if reverse else L - 1
    a_end = cs[end:end + 1, :]
    dth = dt.astype(BF16)
    dtl = (dt - dth.astype(F32)).astype(BF16)
    dtx = jnp.dot(dth, emat, preferred_element_type=F32) + jnp.dot(dtl, emat, preferred_element_type=F32)

    def lanes(v, hh):
        return jnp.broadcast_to(v[:, hh:hh + 1], (v.shape[0], LANES))

    ys = []
    for g in range(2):
        bm = xbc[:, SSM_D + g * SSM_STATE:SSM_D + (g + 1) * SSM_STATE].astype(BF16)
        cm = xbc[:, SSM_D + 2 * SSM_STATE + g * SSM_STATE:SSM_D + 2 * SSM_STATE + (g + 1) * SSM_STATE].astype(BF16)
        cb = lax.dot_general(cm, bm, (((1,), (1,)), ((), ())), preferred_element_type=F32)
        state = state_ref[d, g]
        y_off = jnp.dot(cm, state.astype(BF16), preferred_element_type=F32)
        xdd_blocks, sdec_blocks = [], []
        for p in range(4):
            hp = g * 4 + p
            gmats, e_cols, d_outs = [], [], []
            for j in range(2):
                hh = 2 * hp + j
                col = lanes(cs, hh)
                row = jnp.broadcast_to(cs_t[hh:hh + 1, :], (L, L))
                dec = jnp.exp(jnp.where(mask, col - row, -jnp.inf))
                gmats.append((cb * dec).astype(BF16))
                e_cols.append(jnp.exp(col))
                d_outs.append(jnp.exp(lanes(a_end, hh) - col))
            din = jnp.where(lane_lo, e_cols[0], e_cols[1])
            dout = jnp.where(lane_lo, d_outs[0], d_outs[1])
            xdt = xbc[:, hp * 128:(hp + 1) * 128] * dtx[:, hp * 128:(hp + 1) * 128]
            xdd_blocks.append((xdt * dout).astype(BF16))
            sdec_blocks.append(din[end:end + 1, :])
            xdt = xdt.astype(BF16)
            ya = jnp.dot(gmats[0], xdt, preferred_element_type=F32)
            yb = jnp.dot(gmats[1], xdt, preferred_element_type=F32)
            ys.append(jnp.where(lane_lo, ya, yb) + din * y_off[:, p * 128:(p + 1) * 128])
        upd = lax.dot_general(bm, jnp.concatenate(xdd_blocks, axis=1), (((0,), (0,)), ((), ())),
                              preferred_element_type=F32)
        state_ref[d, g] = state * jnp.concatenate(sdec_blocks, axis=1) + upd
    return jnp.concatenate(ys, axis=1)


def _ssd_kernel(xf_ref, xb_ref, dtf_ref, dtb_ref, alog_ref, bias_ref, emat_ref, yf_ref, yb_ref, state_ref):
    @pl.when(pl.program_id(1) == 0)
    def _():
        state_ref[...] = jnp.zeros_like(state_ref)

    a = -jnp.exp(alog_ref[...])
    bias = bias_ref[...]
    emat = emat_ref[...]
    yf_ref[0] = _ssd_chunk(xf_ref[0], dtf_ref[0], a[0:1, :], bias[0:1, :], emat, state_ref, 0, False)
    yb_ref[0] = _ssd_chunk(xb_ref[0], dtb_ref[0], a[1:2, :], bias[1:2, :], emat, state_ref, 1, True)


def _ssd(xbc, dt, a_log, dt_bias, ctx_len):
    b, t, _ = xbc.shape
    nc = t // CHUNK
    ncc = ctx_len // CHUNK

    def fwd(bb, j):
        return (bb, j, 0)

    def bwd(bb, j):
        return (bb, jnp.where(j < ncc, ncc - 1 - j, nc - 1 + ncc - j), 0)

    pad = lambda v: jnp.pad(v, ((0, 0), (0, 128 - SSM_HEADS)))
    emat = (jnp.arange(LANES)[:, None] == jnp.arange(SSM_D)[None, :] // SSM_HEAD_DIM).astype(BF16)
    return pl.pallas_call(
        _ssd_kernel,
        grid=(b, nc),
        in_specs=[pl.BlockSpec((1, CHUNK, SSM_XBC), fwd), pl.BlockSpec((1, CHUNK, SSM_XBC), bwd),
                  pl.BlockSpec((1, CHUNK, 128), fwd), pl.BlockSpec((1, CHUNK, 128), bwd),
                  _const_spec((2, 128)), _const_spec((2, 128)), _const_spec((LANES, SSM_D))],
        out_specs=[pl.BlockSpec((1, CHUNK, SSM_D), fwd), pl.BlockSpec((1, CHUNK, SSM_D), bwd)],
        out_shape=[jax.ShapeDtypeStruct((b, t, SSM_D), F32), jax.ShapeDtypeStruct((b, t, SSM_D), F32)],
        scratch_shapes=[pltpu.VMEM((2, 2, SSM_STATE, 512), F32)],
        compiler_params=_cparams(("arbitrary", "arbitrary")),
        name="ssd_scan",
    )(xbc, xbc, dt, dt, pad(a_log), pad(dt_bias), emat)


def _post_ssm_kernel(x_ref, yf_ref, yb_ref, xs_ref, z_ref, v_ref, dsk_ref, sg_ref, lng_ref, lnb_ref, wout_ref,
                     mods_ref, g2_ref, wrh_ref, wrl_ref, br_ref, rt_ref, x1_ref, hp_ref, slab_ref, rtab_ref, cnt_ref,
                     carry_ref, wbf_ref, *, nct):
    @pl.when((pl.program_id(0) == 0) & (pl.program_id(1) == 0))
    def _():
        for r0 in range(0, wbf_ref.shape[0], 256):
            wbf_ref[r0:r0 + 256, :] = wout_ref[0, r0:r0 + 256, :].astype(BF16)

    y = yf_ref[0] + yb_ref[0] + dsk_ref[...] * xs_ref[0]
    z = z_ref[0]
    y_ssm = _rms(y * (z * _sigmoid(z)), sg_ref[...])
    v = v_ref[0]
    mu = jnp.mean(v, axis=-1, keepdims=True)
    vc = v - mu
    ln = vc * lax.rsqrt(jnp.mean(vc * vc, axis=-1, keepdims=True) + EPS) * lng_ref[...] + lnb_ref[...]
    y_conv = ln * _sigmoid(ln)
    o = (jnp.dot(y_ssm.astype(BF16), wbf_ref[0:1024, :], preferred_element_type=F32)
         + jnp.dot(y_conv.astype(BF16), wbf_ref[1024:2048, :], preferred_element_type=F32))
    _residual_norm_router(x_ref[0], o, mods_ref, g2_ref, wrh_ref, wrl_ref, br_ref, rt_ref, x1_ref, hp_ref, slab_ref,
                          rtab_ref, cnt_ref, carry_ref, nct=nct, route_ctx=False)


def _post_ssm(x_all, yf, yb, xbc_act, z, v, d_skip, ssm_norm_g, ln_g, ln_b, w_out, mods, g2, w_router, b_router, nct):
    b, t, _ = x_all.shape
    tok = lambda bb, i: (bb, i, 0)
    dsk = jnp.repeat(d_skip[0] + d_skip[1], SSM_HEAD_DIM).reshape(1, SSM_D)
    out_specs, out_shape = _post_out_specs(b, t)
    row = lambda: pl.BlockSpec((1, TM, 1024), tok)
    return pl.pallas_call(
        functools.partial(_post_ssm_kernel, nct=nct),
        grid=(b, t // TM),
        in_specs=[row(), row(), row(), row(), row(), row(),
                  _const_spec((1, SSM_D)), _const_spec((1, SSM_D)), _const_spec((1, D)), _const_spec((1, D)),
                  pl.BlockSpec((1,) + w_out.shape[1:], lambda bb, i: (0, 0, 0), pipeline_mode=pl.Buffered(1)),
                  pl.BlockSpec((1, 1, 6, D), lambda bb, i: (bb, jnp.where(i < nct, 1, 0), 0, 0)),
                  _const_spec((1, D))] + _router_specs(),
        out_specs=out_specs, out_shape=out_shape,
        scratch_shapes=[pltpu.VMEM((N_EXPERTS, LANES), F32), pltpu.VMEM(w_out.shape[1:], BF16)],
        compiler_params=_cparams(("arbitrary", "arbitrary")),
        name="post_ssm",
    )(x_all, yf, yb, xbc_act, z, v, dsk, ssm_norm_g.reshape(1, -1), ln_g.reshape(1, -1), ln_b.reshape(1, -1),
      w_out, mods, g2.reshape(1, D), *_router_operands(w_router, b_router))


def _final_kernel(x_ref, g_ref, slab_ref, mods_ref, fg_ref, o_ref):
    x = x_ref[0] + mods_ref[0, 0, 5:6, :] * _combine(g_ref, slab_ref[0])
    o_ref[0] = _rms(x, fg_ref[...])


def _final(x_all, g_lat, slab, mods, final_g, nct):
    b, s = g_lat.shape[2], g_lat.shape[3]
    return pl.pallas_call(
        _final_kernel,
        grid=(b, s // TM),
        in_specs=[pl.BlockSpec((1, TM, D), lambda bb, i: (bb, i + nct, 0)),
                  pl.BlockSpec((TOP_K, 2, 1, TM, SC_ROW_WORDS), lambda bb, i: (0, 0, bb, i, 0)),
                  pl.BlockSpec((1, TM, ROUTE_LANES), lambda bb, i: (bb, i + nct, 0)),
                  pl.BlockSpec((1, 1, 6, D), lambda bb, i: (bb, 0, 0, 0)),
                  _const_spec((1, D))],
        out_specs=pl.BlockSpec((1, TM, D), lambda bb, i: (bb, i, 0)),
        out_shape=jax.ShapeDtypeStruct((b, s, D), F32),
        compiler_params=_cparams(("arbitrary", "arbitrary")),
        name="final_norm",
    )(x_all, g_lat, slab, mods, final_g.reshape(1, D))


def kernel(x, c, ctx, c_ctx, w_mod, b_mod, norm_g, attn_w_in, mla_g_cq, mla_w_uq, mla_g_ckv, mla_w_ukv, gqa_g_q, gqa_g_k, attn_w_out, ssm_w_in, ssm_conv_w, ssm_conv_b, ssm_a_log, ssm_dt_bias, ssm_d, ssm_norm_g, conf_dw_w, conf_dw_b, conf_ln_g, conf_ln_b, ssm_w_out, moe_w_router, moe_b_router, moe_w_gate_up, moe_b_gate_up, moe_w_down, moe_b_down, final_g):
    b, s, _ = x.shape
    ctx_len = ctx.shape[1]
    t = ctx_len + s
    assert ctx_len % TM == 0 and s % TM == 0 and s % GRID_W == 0
    nct = ctx_len // TM

    c_rows = jnp.concatenate([c, c_ctx[None, :], jnp.zeros((-(b + 1) % 8, D), F32)], axis=0)
    mod_all = _modulations(c_rows, w_mod, b_mod)
    mods = []
    for i in range(w_mod.shape[0]):
        lat = mod_all[i, :b].reshape(b, 1, 6, D)
        cm = jnp.broadcast_to(mod_all[i, b].reshape(1, 1, 6, D), (b, 1, 6, D))
        mods.append(jnp.concatenate([lat, cm], axis=1))

    tab = _rope_tables(s, ctx_len)
    qm, km, vm, qg, kg, vgs = _attn_in(ctx, x, mods[0], norm_g[0, 0], attn_w_in[0], mla_g_cq[0], mla_w_uq[0], mla_g_ckv[0],
                                       mla_w_ukv[0], gqa_g_q[0], gqa_g_k[0], tab, nct)
    o_m = _mla_attention(qm, km, vm, nct, ctx_len)
    o_g = _gqa_attention(qg, kg, vgs, nct, ctx_len)
    x_all, hp, slab0, rtab, counts = _post_attn(ctx, x, o_m, o_g, attn_w_out[0], mods[0], norm_g[0, 1], moe_w_router[0],
                                                moe_b_router[0], nct)
    g_all = _moe(hp, rtab, counts, 0, moe_w_gate_up, moe_b_gate_up, moe_w_down, moe_b_down, t, 0)

    x_all, z, xbc, dt, v = _ssm_in(x_all, g_all, slab0, mods[0], mods[1], norm_g[1, 0], ssm_w_in, nct)
    xbc_act = _depthwise_conv(xbc, ssm_conv_w[0], ssm_conv_b[0], ctx_len, True)
    v_conv = _depthwise_conv(v, conf_dw_w[0], conf_dw_b[0], ctx_len, False)
    yf, yb = _ssd(xbc_act, dt, ssm_a_log[0], ssm_dt_bias[0], ctx_len)
    x_all, hp, slab1, rtab, counts = _post_ssm(x_all, yf, yb, xbc_act, z, v_conv, ssm_d[0], ssm_norm_g[0], conf_ln_g[0],
                                               conf_ln_b[0], ssm_w_out, mods[1], norm_g[1, 1], moe_w_router[1],
                                               moe_b_router[1], nct)
    g_lat = _moe(hp, rtab, counts, 1, moe_w_gate_up, moe_b_gate_up, moe_w_down, moe_b_down, s, ctx_len)
    return _final(x_all, g_lat, slab1, mods[1], final_g, nct)
```
